```python
import math
import jax, jax.numpy as jnp
from jax import lax
import numpy as np

D_MODEL = 1024
BATCH = 32
SEQ = 2048
DEPTH = 4

HEAD_DIM = 64
CONV_CH = 512
CONV_WIDTH = 31
WB_HEADS = 8
WB_KV_HEADS = 2
WINDOW = 128
BLOCK = 128
NA_HEADS = 8
GRID_W = 64
NA_KH_MAX = 8
NA_KW = 16
REL_BUCKETS = 32
REL_MAX_DIST = 128
D_FF = 2816
N_EXPERTS = 8
TOP_K = 2
D_FF_EXPERT = 3584
N_DENSE = (DEPTH + 1) // 2
N_MOE = DEPTH // 2
EPS = 1e-6
NEG_INF = -1e30

A_IN = 2 * CONV_CH
B_Q = WB_HEADS * HEAD_DIM
B_KV = WB_KV_HEADS * HEAD_DIM
C_W = NA_HEADS * HEAD_DIM
GATE_COLS = 3 * D_MODEL
IN_COLS = A_IN + B_Q + 2 * B_KV + 3 * C_W + GATE_COLS
SPLITS = (A_IN, A_IN + B_Q, A_IN + B_Q + B_KV, A_IN + B_Q + 2 * B_KV, A_IN + B_Q + 2 * B_KV + 3 * C_W)

kernel_name = "hybrid_conv_window_neighbourhood_moe_encoder"


def rmsnorm(x, g):
    xf = x.astype(jnp.float32)
    y = xf * lax.rsqrt(jnp.mean(xf * xf, axis=-1, keepdims=True) + EPS)
    return (y * g.astype(jnp.float32)).astype(x.dtype)


def layernorm(x, g, b):
    xf = x.astype(jnp.float32)
    xc = xf - jnp.mean(xf, axis=-1, keepdims=True)
    var = jnp.mean(xc * xc, axis=-1, keepdims=True)
    return (xc * lax.rsqrt(var + EPS) * g.astype(jnp.float32) + b.astype(jnp.float32)).astype(x.dtype)


def swiglu(h, w1, w3, w2):
    return (jax.nn.silu(h @ w1) * (h @ w3)) @ w2


def t5_bucket(rel):
    half = REL_BUCKETS // 2
    max_exact = half // 2
    ret = jnp.where(rel > 0, half, 0)
    n = jnp.abs(rel)
    nf = jnp.maximum(n, 1).astype(jnp.float32)
    large = max_exact + (jnp.log(nf / max_exact) / math.log(REL_MAX_DIST / max_exact)
                         * (half - max_exact)).astype(jnp.int32)
    large = jnp.minimum(large, half - 1)
    return ret + jnp.where(n < max_exact, n, large)


def conv_mixer(u, dw_w, dw_b, ln_g, ln_b, w_out):
    a, g = jnp.split(u, 2, axis=-1)
    v = a * jax.nn.sigmoid(g)
    v = lax.conv_general_dilated(
        v, dw_w.astype(v.dtype)[:, None, :], window_strides=(1,),
        padding=[(CONV_WIDTH // 2, CONV_WIDTH // 2)],
        dimension_numbers=("NWC", "WIO", "NWC"),
        feature_group_count=CONV_CH) + dw_b
    v = jax.nn.silu(layernorm(v, ln_g, ln_b))
    return v @ w_out


def window_attention(q, k, v, qn_g, kn_g, sink, rel_table):
    bsz, s_len = q.shape[0], q.shape[1]
    nb = s_len // BLOCK
    grp = WB_HEADS // WB_KV_HEADS
    q = rmsnorm(q, qn_g)
    k = rmsnorm(k, kn_g)
    qb = q.reshape(bsz, nb, BLOCK, WB_KV_HEADS, grp, HEAD_DIM)
    pad = ((0, 0), (BLOCK, BLOCK), (0, 0), (0, 0))
    kp = jnp.pad(k, pad).reshape(bsz, nb + 2, BLOCK, WB_KV_HEADS, HEAD_DIM)
    vp = jnp.pad(v, pad).reshape(bsz, nb + 2, BLOCK, WB_KV_HEADS, HEAD_DIM)
    kw = jnp.concatenate([kp[:, :-2], kp[:, 1:-1], kp[:, 2:]], axis=2)
    vw = jnp.concatenate([vp[:, :-2], vp[:, 1:-1], vp[:, 2:]], axis=2)
    s = jnp.einsum("bnqkgd,bnjkd->bnkgqj", qb, kw,
                   preferred_element_type=jnp.float32) * (HEAD_DIM ** -0.5)
    rel = (jnp.arange(3 * BLOCK)[None, :] - BLOCK) - jnp.arange(BLOCK)[:, None]
    bias = rel_table.astype(jnp.float32)[t5_bucket(rel)]
    bias = bias.reshape(BLOCK, 3 * BLOCK, WB_KV_HEADS, grp).transpose(2, 3, 0, 1)
    kpos = jnp.arange(nb)[:, None] * BLOCK - BLOCK + jnp.arange(3 * BLOCK)[None, :]
    valid = (kpos >= 0) & (kpos < s_len)
    mask = (jnp.abs(rel) <= WINDOW)[None] & valid[:, None, :]
    s = jnp.where(mask[None, :, None, None], s + bias[None, None], NEG_INF)
    sk = sink.astype(jnp.float32).reshape(WB_KV_HEADS, grp)[None, None, :, :, None, None]
    m = jnp.maximum(jnp.max(s, axis=-1, keepdims=True), sk)
    p = jnp.exp(s - m)
    p = p / (jnp.sum(p, axis=-1, keepdims=True) + jnp.exp(sk - m))
    o = jnp.einsum("bnkgqj,bnjkd->bnqkgd", p.astype(vw.dtype), vw)
    return o.reshape(bsz, s_len, WB_HEADS * HEAD_DIM)


def neighbourhood_attention(q, k, v, qn_g, kn_g, rpb):
    bsz, s_len = q.shape[0], q.shape[1]
    rows = s_len // GRID_W
    kh = min(NA_KH_MAX, rows)
    q = rmsnorm(q, qn_g)
    k = rmsnorm(k, kn_g)
    qg = q.reshape(bsz, rows, GRID_W, NA_HEADS, HEAD_DIM)
    kg = k.reshape(bsz, rows, GRID_W, NA_HEADS, HEAD_DIM)
    vg = v.reshape(bsz, rows, GRID_W, NA_HEADS, HEAD_DIM)
    col = jnp.arange(GRID_W)
    col_start = jnp.clip(col - NA_KW // 2, 0, GRID_W - NA_KW)
    col_idx = col_start[:, None] + jnp.arange(NA_KW)[None, :]
    dc_idx = col_idx - col[:, None] + (NA_KW - 1)
    rpb_f = rpb.astype(jnp.float32)

    def row_block(r):
        r0 = jnp.clip(r - kh // 2, 0, rows - kh)
        k_rows = lax.dynamic_slice_in_dim(kg, r0, kh, axis=1)
        v_rows = lax.dynamic_slice_in_dim(vg, r0, kh, axis=1)
        k_win = k_rows[:, :, col_idx]
        v_win = v_rows[:, :, col_idx]
        q_r = lax.dynamic_index_in_dim(qg, r, axis=1, keepdims=False)
        s = jnp.einsum("bqhd,brqjhd->bhqrj", q_r, k_win,
                       preferred_element_type=jnp.float32) * (HEAD_DIM ** -0.5)
        dr_idx = r0 + jnp.arange(kh) - r + (NA_KH_MAX - 1)
        bias = rpb_f[dr_idx[:, None, None], dc_idx[None]]
        s = s + bias.transpose(3, 1, 0, 2)[None]
        p = jax.nn.softmax(s.reshape(bsz, NA_HEADS, GRID_W, kh * NA_KW), axis=-1).reshape(s.shape)
        return jnp.einsum("bhqrj,brqjhd->bqhd", p.astype(v_win.dtype), v_win)

    out = lax.map(row_block, jnp.arange(rows))
    return out.transpose(1, 0, 2, 3, 4).reshape(bsz, s_len, NA_HEADS * HEAD_DIM)


def moe_swiglu(h, w_router, b_router, w1, w3, w2):
    logits = (h @ w_router).astype(jnp.float32) + b_router.astype(jnp.float32)
    top_val, top_idx = lax.top_k(logits, TOP_K)
    top_w = jax.nn.softmax(top_val, axis=-1)
    gate = jnp.sum(jax.nn.one_hot(top_idx, N_EXPERTS, dtype=jnp.float32) * top_w[..., None], axis=-2)
    gate = gate.astype(h.dtype)
    out = jnp.zeros_like(h)
    for e in range(N_EXPERTS):
        out = out + gate[..., e:e + 1] * swiglu(h, w1[e], w3[e], w2[e])
    return out


def setup_inputs(seed: int = 0) -> dict:
    key = jax.random.key(seed)
    ks = iter(jax.random.split(key, 40))

    def nrm(shape, scale):
        return jax.random.normal(next(ks), shape, jnp.float32) * scale

    def gain(shape):
        return 1.0 + nrm(shape, 0.02)

    D = D_MODEL
    return {
        "x": nrm((BATCH, SEQ, D), 1.0),
        "t5_rel_bias": nrm((REL_BUCKETS, WB_HEADS), 0.1),
        "attn_norm_g": gain((DEPTH, D)),
        "w_in": nrm((DEPTH, D, IN_COLS), D ** -0.5),
        "gate_b": nrm((DEPTH, GATE_COLS), 0.02),
        "conv_dw_w": nrm((DEPTH, CONV_WIDTH, CONV_CH), CONV_WIDTH ** -0.5),
        "conv_dw_b": nrm((DEPTH, CONV_CH), 0.02),
        "conv_ln_g": gain((DEPTH, CONV_CH)),
        "conv_ln_b": nrm((DEPTH, CONV_CH), 0.02),
        "conv_w_out": nrm((DEPTH, CONV_CH, D), CONV_CH ** -0.5),
        "wb_qn_g": gain((DEPTH, HEAD_DIM)),
        "wb_kn_g": gain((DEPTH, HEAD_DIM)),
        "wb_sink": nrm((DEPTH, WB_HEADS), 0.5),
        "wb_w_out": nrm((DEPTH, B_Q, D), B_Q ** -0.5),
        "na_qn_g": gain((DEPTH, HEAD_DIM)),
        "na_kn_g": gain((DEPTH, HEAD_DIM)),
        "na_rpb": nrm((DEPTH, 2 * NA_KH_MAX - 1, 2 * NA_KW - 1, NA_HEADS), 0.1),
        "na_w_out": nrm((DEPTH, C_W, D), C_W ** -0.5),
        "w_o": nrm((DEPTH, D, D), D ** -0.5),
        "ffn_norm_g": gain((DEPTH, D)),
        "ffn_w1": nrm((N_DENSE, D, D_FF), D ** -0.5),
        "ffn_w3": nrm((N_DENSE, D, D_FF), D ** -0.5),
        "ffn_w2": nrm((N_DENSE, D_FF, D), D_FF ** -0.5),
        "moe_w_router": nrm((N_MOE, D, N_EXPERTS), D ** -0.5),
        "moe_b_router": nrm((N_MOE, N_EXPERTS), 0.01),
        "moe_w1": nrm((N_MOE, N_EXPERTS, D, D_FF_EXPERT), D ** -0.5),
        "moe_w3": nrm((N_MOE, N_EXPERTS, D, D_FF_EXPERT), D ** -0.5),
        "moe_w2": nrm((N_MOE, N_EXPERTS, D_FF_EXPERT, D), D_FF_EXPERT ** -0.5),
    }


def reference(x, t5_rel_bias, attn_norm_g, w_in, gate_b, conv_dw_w, conv_dw_b, conv_ln_g,
              conv_ln_b, conv_w_out, wb_qn_g, wb_kn_g, wb_sink, wb_w_out, na_qn_g, na_kn_g,
              na_rpb, na_w_out, w_o, ffn_norm_g, ffn_w1, ffn_w3, ffn_w2, moe_w_router,
              moe_b_router, moe_w1, moe_w3, moe_w2):
    bsz, s_len = x.shape[0], x.shape[1]
    for layer in range(DEPTH):
        h = rmsnorm(x, attn_norm_g[layer])
        z = h @ w_in[layer]
        za, zbq, zbk, zbv, zc, zg = jnp.split(z, SPLITS, axis=-1)
        y_a = conv_mixer(za, conv_dw_w[layer], conv_dw_b[layer], conv_ln_g[layer],
                         conv_ln_b[layer], conv_w_out[layer])
        ob = window_attention(
            zbq.reshape(bsz, s_len, WB_HEADS, HEAD_DIM),
            zbk.reshape(bsz, s_len, WB_KV_HEADS, HEAD_DIM),
            zbv.reshape(bsz, s_len, WB_KV_HEADS, HEAD_DIM),
            wb_qn_g[layer], wb_kn_g[layer], wb_sink[layer], t5_rel_bias)
        y_b = ob @ wb_w_out[layer]
        cq, ck, cv = jnp.split(zc, 3, axis=-1)
        oc = neighbourhood_attention(
            cq.reshape(bsz, s_len, NA_HEADS, HEAD_DIM),
            ck.reshape(bsz, s_len, NA_HEADS, HEAD_DIM),
            cv.reshape(bsz, s_len, NA_HEADS, HEAD_DIM),
            na_qn_g[layer], na_kn_g[layer], na_rpb[layer])
        y_c = oc @ na_w_out[layer]
        g = jax.nn.sigmoid(zg + gate_b[layer]).reshape(bsz, s_len, 3, D_MODEL)
        merged = g[:, :, 0] * y_a + g[:, :, 1] * y_b + g[:, :, 2] * y_c
        x = x + merged @ w_o[layer]
        hf = rmsnorm(x, ffn_norm_g[layer])
        if layer % 2 == 0:
            i = layer // 2
            x = x + swiglu(hf, ffn_w1[i], ffn_w3[i], ffn_w2[i])
        else:
            i = layer // 2
            x = x + moe_swiglu(hf, moe_w_router[i], moe_b_router[i], moe_w1[i], moe_w3[i], moe_w2[i])
    return x
```

```python
import functools
import math

import numpy as np
import jax
import jax.numpy as jnp
from jax import lax
from jax.experimental import pallas as pl
from jax.experimental.pallas import tpu as pltpu

D_MODEL = 1024
HEAD_DIM = 64
CONV_CH = 512
CONV_WIDTH = 31
WB_HEADS = 8
WB_KV_HEADS = 2
WINDOW = 128
BLOCK = 128
NA_HEADS = 8
GRID_W = 64
NA_KH = 8
NA_KW = 16
REL_BUCKETS = 32
REL_MAX_DIST = 128
N_EXPERTS = 8
EPS = 1e-6
NEG_INF = -1e30

LANES = 128

ZG = 0
ZA = 3072
ZBQ = 4096
ZBK = 4608
ZBV = 4864
ZCQ = 5120
ZCK = 5632
ZCV = 6144
ZC = 6656
ZCHUNK = 512

VMEM_LIMIT = 56 * 1024 * 1024

F32 = jnp.float32
BF16 = jnp.bfloat16


def _cparams(grid_rank, vmem=VMEM_LIMIT):
    return pltpu.CompilerParams(dimension_semantics=("arbitrary",) * grid_rank, vmem_limit_bytes=vmem)


def _resident(shape):
    nd = len(shape)
    return pl.BlockSpec(shape, lambda *_: (0,) * nd, pipeline_mode=pl.Buffered(1))


def _dot(a, b):
    return jnp.dot(a, b, preferred_element_type=F32)


def _dot_nt(a, b):
    return lax.dot_general(a, b, (((1,), (1,)), ((), ())), preferred_element_type=F32)


def _t5_bucket_np(rel):
    half = REL_BUCKETS // 2
    max_exact = half // 2
    ret = np.where(rel > 0, half, 0)
    n = np.abs(rel)
    nf = np.maximum(n, 1).astype(np.float32)
    large = max_exact + (np.log(nf / np.float32(max_exact)) / np.float32(math.log(REL_MAX_DIST / max_exact))
                         * np.float32(half - max_exact)).astype(np.int32)
    large = np.minimum(large, half - 1)
    return ret + np.where(n < max_exact, n, large)


def _window_bucket_table():
    rel = (np.arange(3 * BLOCK)[None, :] - BLOCK) - np.arange(BLOCK)[:, None]
    return np.where(np.abs(rel) <= WINDOW, _t5_bucket_np(rel), -1).astype(np.int32)


NA_QROWS = 2
NA_KROWS = 10


def _na_tables(rows):
    kh = min(NA_KH, rows)
    col = np.arange(GRID_W)
    col_start = np.clip(col - NA_KW // 2, 0, GRID_W - NA_KW)
    kc = np.arange(GRID_W)[None, :]
    inwin = (kc >= col_start[:, None]) & (kc < col_start[:, None] + NA_KW)
    dmat = np.where(inwin, kc - col[:, None] + NA_KW - 1, -1).astype(np.int32)
    variants, variant_of_pair, kstart = [], [], []
    for i in range(rows // NA_QROWS):
        ks = int(np.clip(NA_QROWS * i - kh // 2, 0, rows - NA_KROWS))
        tab = np.full((NA_QROWS, NA_KROWS), -1, np.int64)
        for a in range(NA_QROWS):
            r = NA_QROWS * i + a
            r0 = int(np.clip(r - kh // 2, 0, rows - kh))
            assert ks <= r0 and r0 + kh <= ks + NA_KROWS
            for kr in range(NA_KROWS):
                if r0 <= ks + kr < r0 + kh:
                    tab[a, kr] = ks + kr - r + NA_KH - 1
        for v, t in enumerate(variants):
            if np.array_equal(t, tab):
                variant_of_pair.append(v)
                break
        else:
            variant_of_pair.append(len(variants))
            variants.append(tab)
        kstart.append(ks)
    return dmat, variants, variant_of_pair, kstart


def _t5_bias_kernel(tab_ref, bucket_ref, o_ref):
    h = pl.program_id(0)
    bucket = bucket_ref[...]
    acc = jnp.full(bucket.shape, NEG_INF, F32)
    for b in range(REL_BUCKETS):
        acc = jnp.where(bucket == b, tab_ref[b, h], acc)
    o_ref[0] = acc


def _t5_bias(t5_rel_bias):
    bucket = jnp.asarray(_window_bucket_table())
    return pl.pallas_call(
        _t5_bias_kernel,
        grid=(WB_HEADS,),
        in_specs=[pl.BlockSpec(memory_space=pltpu.SMEM),
                  pl.BlockSpec((BLOCK, 3 * BLOCK), lambda h: (0, 0))],
        out_specs=pl.BlockSpec((1, BLOCK, 3 * BLOCK), lambda h: (h, 0, 0)),
        out_shape=jax.ShapeDtypeStruct((WB_HEADS, BLOCK, 3 * BLOCK), F32),
        compiler_params=_cparams(1),
        name="t5_bias",
    )(t5_rel_bias.astype(F32), bucket)


def _na_bias_kernel(variants, rpb_ref, dmat_ref, o_ref):
    lh = pl.program_id(0) * NA_HEADS + pl.program_id(1)
    dmat = dmat_ref[...]
    ncol = 2 * NA_KW - 1
    base = []
    for dr in range(2 * NA_KH - 1):
        acc = jnp.full(dmat.shape, NEG_INF, F32)
        for d in range(ncol):
            acc = jnp.where(dmat == d, rpb_ref[lh, dr * ncol + d], acc)
        base.append(acc)
    neg = jnp.full(dmat.shape, NEG_INF, F32)
    for v, tab in enumerate(variants):
        for a in range(NA_QROWS):
            for kr in range(NA_KROWS):
                dr = int(tab[a, kr])
                tile = neg if dr < 0 else base[dr]
                o_ref[0, v, 0, a * GRID_W:(a + 1) * GRID_W, kr * GRID_W:(kr + 1) * GRID_W] = tile


def _na_bias(na_rpb, rows):
    depth = na_rpb.shape[0]
    dmat, variants, _, _ = _na_tables(rows)
    nv = len(variants)
    rpb = jnp.transpose(na_rpb.astype(F32), (0, 3, 1, 2)).reshape(depth * NA_HEADS, -1)
    qn, kn = NA_QROWS * GRID_W, NA_KROWS * GRID_W
    return pl.pallas_call(
        functools.partial(_na_bias_kernel, variants),
        grid=(depth, NA_HEADS),
        in_specs=[pl.BlockSpec(memory_space=pltpu.SMEM),
                  pl.BlockSpec((GRID_W, GRID_W), lambda l, h: (0, 0))],
        out_specs=pl.BlockSpec((1, nv, 1, qn, kn), lambda l, h: (l, 0, h, 0, 0)),
        out_shape=jax.ShapeDtypeStruct((depth, nv, NA_HEADS, qn, kn), F32),
        compiler_params=_cparams(2),
        name="na_bias",
    )(rpb, jnp.asarray(dmat))


def _inproj_kernel(x_ref, ng_ref, w_ref, cg_ref, gb_ref, bd_ref, z_ref, h_ref):
    x = x_ref[...]
    inv = lax.rsqrt(jnp.mean(x * x, axis=-1, keepdims=True) + EPS)
    h_ref[...] = (x * inv * ng_ref[...]).astype(BF16)

    def qknorm(zc, c0):
        w = zc.shape[1]
        ms = _dot((zc * zc).astype(BF16), bd_ref[:w, :w])
        return zc * lax.rsqrt(ms + EPS) * cg_ref[:, c0:c0 + w]

    for c0 in range(0, ZC, ZCHUNK):
        zc = _dot(h_ref[...], w_ref[:, c0:c0 + ZCHUNK])
        if c0 < ZA:
            out = jax.nn.sigmoid(zc + gb_ref[:, c0:c0 + ZCHUNK])
        elif c0 in (ZBQ, ZCQ, ZCK):
            out = qknorm(zc, c0)
        elif c0 == ZBK:
            half = ZBV - ZBK
            out = jnp.concatenate([qknorm(zc[:, :half], c0), zc[:, half:]], axis=1)
        else:
            out = zc
        z_ref[:, c0:c0 + ZCHUNK] = out.astype(BF16)


def _inproj(x, ng, w, cg, gb, bd, tm=512):
    n = x.shape[0]
    return pl.pallas_call(
        _inproj_kernel,
        grid=(n // tm,),
        in_specs=[pl.BlockSpec((tm, D_MODEL), lambda i: (i, 0)),
                  _resident((1, D_MODEL)), _resident((D_MODEL, ZC)), _resident((1, ZC)),
                  _resident((1, ZA)), _resident((ZCHUNK, ZCHUNK))],
        out_specs=pl.BlockSpec((tm, ZC), lambda i: (i, 0)),
        out_shape=jax.ShapeDtypeStruct((n, ZC), BF16),
        scratch_shapes=[pltpu.VMEM((tm, D_MODEL), BF16)],
        compiler_params=_cparams(1),
        name="inproj",
    )(x, ng, w, cg, gb, bd)


SUBLANES = 8
CONV_PAD = 16
CONV_HALF = 256
CONV_ROWS = 64
CONV_SHIFT_ROWS = 104


def _conv_kernel(z_ref, dw_ref, db_ref, lg_ref, lb_ref, u_ref, vs_ref, c_ref):
    s = z_ref.shape[0]
    span = s + 2 * CONV_PAD
    first = CONV_PAD - CONV_WIDTH // 2
    glu_rows = 256

    for half in range(CONV_CH // CONV_HALF):
        l0 = half * CONV_HALF
        vs_ref[0, 0:CONV_PAD, :] = jnp.zeros((CONV_PAD, CONV_HALF), F32)
        vs_ref[0, CONV_PAD + s:span + SUBLANES, :] = jnp.zeros((CONV_PAD + SUBLANES, CONV_HALF), F32)

        def glu(i, carry):
            r = pl.multiple_of(i * glu_rows, glu_rows)
            a = z_ref[pl.ds(r, glu_rows), l0:l0 + CONV_HALF].astype(F32)
            g = z_ref[pl.ds(r, glu_rows), CONV_CH + l0:CONV_CH + l0 + CONV_HALF].astype(F32)
            vs_ref[0, pl.ds(CONV_PAD + r, glu_rows), :] = a * jax.nn.sigmoid(g)
            return carry

        lax.fori_loop(0, s // glu_rows, glu, 0)

        def shift(i, carry):
            r = pl.multiple_of(i * CONV_SHIFT_ROWS, SUBLANES)
            win = vs_ref[0, pl.ds(r, CONV_SHIFT_ROWS + SUBLANES), :]
            for k in range(1, SUBLANES):
                vs_ref[k, pl.ds(r, CONV_SHIFT_ROWS), :] = win[k:k + CONV_SHIFT_ROWS]
            return carry

        assert span % CONV_SHIFT_ROWS == 0
        lax.fori_loop(0, span // CONV_SHIFT_ROWS, shift, 0)

        def conv(i, carry):
            r = pl.multiple_of(i * CONV_ROWS, CONV_ROWS)
            acc = jnp.zeros((CONV_ROWS, CONV_HALF), F32) + db_ref[:, l0:l0 + CONV_HALF]
            for w in range(CONV_WIDTH):
                k, a = (first + w) % SUBLANES, (first + w) // SUBLANES
                tap = vs_ref[k, pl.ds(r + a * SUBLANES, CONV_ROWS), :]
                acc = acc + tap * dw_ref[w:w + 1, l0:l0 + CONV_HALF]
            c_ref[pl.ds(r, CONV_ROWS), l0:l0 + CONV_HALF] = acc
            return carry

        lax.fori_loop(0, s // CONV_ROWS, conv, 0)

    def norm(i, carry):
        r = pl.multiple_of(i * CONV_ROWS, CONV_ROWS)
        c = c_ref[pl.ds(r, CONV_ROWS), :]
        xc = c - jnp.mean(c, axis=-1, keepdims=True)
        var = jnp.mean(xc * xc, axis=-1, keepdims=True)
        y = xc * lax.rsqrt(var + EPS) * lg_ref[...] + lb_ref[...]
        u_ref[pl.ds(r, CONV_ROWS), :] = (y * jax.nn.sigmoid(y)).astype(BF16)
        return carry

    lax.fori_loop(0, s // CONV_ROWS, norm, 0)


def _conv_mixer(z, bsz, s, dw, db, lg, lb):
    return pl.pallas_call(
        _conv_kernel,
        grid=(bsz,),
        in_specs=[pl.BlockSpec((s, 2 * CONV_CH), lambda b: (b, ZA // (2 * CONV_CH))),
                  _resident((CONV_WIDTH, CONV_CH)), _resident((1, CONV_CH)),
                  _resident((1, CONV_CH)), _resident((1, CONV_CH))],
        out_specs=pl.BlockSpec((s, CONV_CH), lambda b: (b, 0)),
        out_shape=jax.ShapeDtypeStruct((bsz * s, CONV_CH), BF16),
        scratch_shapes=[pltpu.VMEM((SUBLANES, s + 2 * CONV_PAD + SUBLANES, CONV_HALF), F32),
                        pltpu.VMEM((s, CONV_CH), F32)],
        compiler_params=_cparams(1),
        name="conv_mixer",
    )(z, dw, db, lg, lb)


def _head_masks():
    lane = lax.broadcasted_iota(jnp.int32, (1, LANES), 1)
    return lane < HEAD_DIM


def _wattn_kernel(sink_ref, q_ref, k_ref, v_ref, bias_ref, o_ref):
    hp = pl.program_id(1)
    s = q_ref.shape[0]
    nb = s // BLOCK
    lo = _head_masks()
    zero = jnp.zeros((), BF16)

    def block(n, carry):
        r = pl.multiple_of(n * BLOCK, BLOCK)
        q = q_ref[pl.ds(r, BLOCK), :]
        ks, vs, valid = [], [], []
        for j in (-1, 0, 1):
            kb = jnp.clip(n + j, 0, nb - 1)
            rk = pl.multiple_of(kb * BLOCK, BLOCK)
            ks.append(k_ref[pl.ds(rk, BLOCK), :])
            vs.append(v_ref[pl.ds(rk, BLOCK), :])
            valid.append(jnp.logical_and(n + j >= 0, n + j < nb))
        outs = []
        for hh in range(2):
            qm = jnp.where(lo if hh == 0 else jnp.logical_not(lo), q, zero)
            sink = sink_ref[0, 2 * hp + hh]
            sc = []
            for j in range(3):
                sj = _dot_nt(qm, ks[j]) + bias_ref[hh, :, j * BLOCK:(j + 1) * BLOCK]
                if j != 1:
                    sj = jnp.where(valid[j], sj, NEG_INF)
                sc.append(sj)
            m = jnp.maximum(jnp.maximum(jnp.max(sc[0], axis=-1, keepdims=True),
                                        jnp.max(sc[1], axis=-1, keepdims=True)),
                            jnp.max(sc[2], axis=-1, keepdims=True))
            m = jnp.maximum(m, sink)
            l = jnp.exp(sink - m)
            o = jnp.zeros((BLOCK, LANES), F32)
            for j in range(3):
                p = jnp.exp(sc[j] - m)
                l = l + jnp.sum(p, axis=-1, keepdims=True)
                o = o + _dot(p.astype(BF16), vs[j])
            outs.append(o / l)
        o_ref[pl.ds(r, BLOCK), :] = jnp.where(lo, outs[0], outs[1]).astype(BF16)
        return carry

    lax.fori_loop(0, nb, block, 0)


def _window_attention(z, bsz, s, sink, bias):
    npair = WB_HEADS // 2
    pairs_per_kv = npair // WB_KV_HEADS
    return pl.pallas_call(
        _wattn_kernel,
        grid=(bsz, npair),
        in_specs=[pl.BlockSpec(memory_space=pltpu.SMEM),
                  pl.BlockSpec((s, LANES), lambda b, hp: (b, ZBQ // LANES + hp)),
                  pl.BlockSpec((s, LANES), lambda b, hp: (b, ZBK // LANES + hp // pairs_per_kv)),
                  pl.BlockSpec((s, LANES), lambda b, hp: (b, ZBV // LANES + hp // pairs_per_kv)),
                  pl.BlockSpec((2, BLOCK, 3 * BLOCK), lambda b, hp: (hp, 0, 0))],
        out_specs=pl.BlockSpec((s, LANES), lambda b, hp: (b, hp)),
        out_shape=jax.ShapeDtypeStruct((bsz * s, WB_HEADS * HEAD_DIM), BF16),
        compiler_params=_cparams(2),
        name="window_attention",
    )(sink, z, z, z, bias)


def _nattn_kernel(n_edge, n_pairs, q_ref, k_ref, v_ref, bias_ref, o_ref):
    lo = _head_masks()
    zero = jnp.zeros((), BF16)
    qn, kn = NA_QROWS * GRID_W, NA_KROWS * GRID_W
    max_ks = k_ref.shape[0] // GRID_W - NA_KROWS

    def pair(i, carry):
        r = pl.multiple_of(i * qn, qn)
        ks = jnp.clip(NA_QROWS * i - NA_KH // 2, 0, max_ks)
        rk = pl.multiple_of(ks * GRID_W, LANES)
        var = jnp.where(i < n_edge, i, jnp.where(i >= n_pairs - n_edge, i - (n_pairs - 2 * n_edge - 1), n_edge))
        q = q_ref[pl.ds(r, qn), :]
        kw = k_ref[pl.ds(rk, kn), :]
        vw = v_ref[pl.ds(rk, kn), :]
        outs = []
        for hh in range(2):
            qm = jnp.where(lo if hh == 0 else jnp.logical_not(lo), q, zero)
            sc = _dot_nt(qm, kw) + bias_ref[0, var, hh]
            m = jnp.max(sc, axis=-1, keepdims=True)
            p = jnp.exp(sc - m)
            l = jnp.sum(p, axis=-1, keepdims=True)
            outs.append(_dot(p.astype(BF16), vw) / l)
        o_ref[pl.ds(r, qn), :] = jnp.where(lo, outs[0], outs[1]).astype(BF16)
        return carry

    lax.fori_loop(0, n_pairs, pair, 0)


def _neighbourhood_attention(z, bsz, s, bias, layer):
    rows = s // GRID_W
    _, variants, variant_of_pair, kstart = _na_tables(rows)
    n_pairs = rows // NA_QROWS
    nv = len(variants)
    n_edge = (nv - 1) // 2
    for i in range(n_pairs):
        want = i if i < n_edge else (i - (n_pairs - 2 * n_edge - 1) if i >= n_pairs - n_edge else n_edge)
        assert variant_of_pair[i] == want
        assert kstart[i] == int(np.clip(NA_QROWS * i - NA_KH // 2, 0, rows - NA_KROWS))
        assert (kstart[i] * GRID_W) % LANES == 0
    npair = NA_HEADS // 2
    qn, kn = NA_QROWS * GRID_W, NA_KROWS * GRID_W
    return pl.pallas_call(
        functools.partial(_nattn_kernel, n_edge, n_pairs),
        grid=(npair, bsz),
        in_specs=[pl.BlockSpec((s, LANES), lambda hp, b: (b, ZCQ // LANES + hp)),
                  pl.BlockSpec((s, LANES), lambda hp, b: (b, ZCK // LANES + hp)),
                  pl.BlockSpec((s, LANES), lambda hp, b: (b, ZCV // LANES + hp)),
                  pl.BlockSpec((1, nv, 2, qn, kn), lambda hp, b: (layer, 0, hp, 0, 0))],
        out_specs=pl.BlockSpec((s, LANES), lambda hp, b: (b, hp)),
        out_shape=jax.ShapeDtypeStruct((bsz * s, NA_HEADS * HEAD_DIM), BF16),
        compiler_params=_cparams(2),
        name="neighbourhood_attention",
    )(z, z, z, bias)


ROUTE_IDX_LANE = N_EXPERTS
ROUTE_W_LANE = N_EXPERTS + 2


def _merge_kernel(with_router, u_ref, ob_ref, oc_ref, g_ref, x_ref, wa_ref, wb_ref, wc_ref, wo_ref,
                  fg_ref, *rest):
    if with_router:
        wr_ref, br_ref, xo_ref, hf_ref, slab_ref = rest
    else:
        xo_ref, hf_ref = rest
    d = D_MODEL
    m = g_ref[:, 0:d].astype(F32) * _dot(u_ref[...], wa_ref[...])
    m = m + g_ref[:, d:2 * d].astype(F32) * _dot(ob_ref[...], wb_ref[...])
    m = m + g_ref[:, 2 * d:3 * d].astype(F32) * _dot(oc_ref[...], wc_ref[...])
    xn = x_ref[...] + _dot(m.astype(BF16), wo_ref[...])
    xo_ref[...] = xn
    hf = xn * lax.rsqrt(jnp.mean(xn * xn, axis=-1, keepdims=True) + EPS) * fg_ref[...]
    hf_ref[...] = hf.astype(hf_ref.dtype)
    if with_router:
        logits = jnp.dot(hf, wr_ref[...], preferred_element_type=F32,
                         precision=lax.Precision.HIGHEST) + br_ref[...]
        lane = lax.broadcasted_iota(jnp.int32, logits.shape, 1)
        m1 = jnp.max(logits, axis=-1, keepdims=True)
        i1 = jnp.min(jnp.where(logits == m1, lane, LANES), axis=-1, keepdims=True)
        rest_l = jnp.where(lane == i1, -jnp.inf, logits)
        m2 = jnp.max(rest_l, axis=-1, keepdims=True)
        i2 = jnp.min(jnp.where(rest_l == m2, lane, LANES), axis=-1, keepdims=True)
        e2 = jnp.exp(m2 - m1)
        w1 = 1.0 / (1.0 + e2)
        w2 = e2 / (1.0 + e2)
        slab = jnp.where(lane == ROUTE_IDX_LANE, i1.astype(F32), 0.0)
        slab = jnp.where(lane == ROUTE_IDX_LANE + 1, i2.astype(F32), slab)
        slab = jnp.where(lane == ROUTE_W_LANE, w1, slab)
        slab = jnp.where(lane == ROUTE_W_LANE + 1, w2, slab)
        slab_ref[...] = slab


def _merge(u, ob, oc, z, x, wa, wb, wc, wo, fg, router=None, tm=512):
    n = x.shape[0]
    with_router = router is not None
    row = lambda w: pl.BlockSpec((tm, w), lambda i: (i, 0))
    in_specs = [row(CONV_CH), row(WB_HEADS * HEAD_DIM), row(NA_HEADS * HEAD_DIM),
                pl.BlockSpec((tm, ZA), lambda i: (i, 0)), row(D_MODEL),
                _resident(wa.shape), _resident(wb.shape), _resident(wc.shape), _resident(wo.shape),
                _resident((1, D_MODEL))]
    args = [u, ob, oc, z, x, wa, wb, wc, wo, fg]
    out_specs = [row(D_MODEL), row(D_MODEL)]
    out_shape = [jax.ShapeDtypeStruct((n, D_MODEL), F32),
                 jax.ShapeDtypeStruct((n, D_MODEL), F32 if with_router else BF16)]
    if with_router:
        in_specs += [_resident((D_MODEL, LANES)), _resident((1, LANES))]
        args += list(router)
        out_specs.append(row(LANES))
        out_shape.append(jax.ShapeDtypeStruct((n, LANES), F32))
    return pl.pallas_call(
        functools.partial(_merge_kernel, with_router),
        grid=(n // tm,),
        in_specs=in_specs, out_specs=out_specs, out_shape=out_shape,
        input_output_aliases={4: 0},
        compiler_params=_cparams(1),
        name="merge_router" if with_router else "merge",
    )(*args)


FFN_CHUNK = 1024


def _ffn_kernel(hf_ref, x_ref, w1_ref, w3_ref, w2_ref, o_ref):
    dff = w1_ref.shape[1]
    hf = hf_ref[...]
    acc = x_ref[...]
    for c0 in range(0, dff, FFN_CHUNK):
        c1 = min(c0 + FFN_CHUNK, dff)
        a = _dot(hf, w1_ref[:, c0:c1])
        b = _dot(hf, w3_ref[:, c0:c1])
        hm = (a * jax.nn.sigmoid(a) * b).astype(BF16)
        acc = acc + _dot(hm, w2_ref[c0:c1, :])
    o_ref[...] = acc


def _dense_ffn(hf, x, w1, w3, w2, tm=512):
    n = x.shape[0]
    row = pl.BlockSpec((tm, D_MODEL), lambda i: (i, 0))
    return pl.pallas_call(
        _ffn_kernel,
        grid=(n // tm,),
        in_specs=[row, row, _resident(w1.shape), _resident(w3.shape), _resident(w2.shape)],
        out_specs=row,
        out_shape=jax.ShapeDtypeStruct((n, D_MODEL), F32),
        input_output_aliases={1: 0},
        compiler_params=_cparams(1),
        name="dense_ffn",
    )(hf, x, w1, w3, w2)


MOE_TILE = 1024
MOE_FF_CHUNK = 512
ROUTE_ROWS = 256


def _row_copy(src, dst, sem):
    return pltpu.make_async_copy(src, dst, sem)


def _dispatch_kernel(dest_ref, hf_ref, init_ref, xs_ref, sem):
    del init_ref

    def start(r, carry):
        for k in range(2):
            d = dest_ref[0, 0, 2 * r + k]
            _row_copy(hf_ref.at[pl.ds(r, 1)], xs_ref.at[pl.ds(d, 1)], sem).start()
        return carry

    lax.fori_loop(0, ROUTE_ROWS, start, 0)

    def wait(r, carry):
        for k in range(2):
            _row_copy(hf_ref.at[pl.ds(0, 1)], xs_ref.at[pl.ds(0, 1)], sem).wait()
        return carry

    lax.fori_loop(0, ROUTE_ROWS, wait, 0)


def _dispatch(hf, dest, n_rows):
    n = hf.shape[0]
    steps = n // ROUTE_ROWS
    init = jnp.zeros((n_rows, D_MODEL), F32)
    return pl.pallas_call(
        _dispatch_kernel,
        grid=(steps,),
        in_specs=[pl.BlockSpec((1, 1, 2 * ROUTE_ROWS), lambda i: (i, 0, 0), memory_space=pltpu.SMEM),
                  pl.BlockSpec((ROUTE_ROWS, D_MODEL), lambda i: (i, 0)),
                  pl.BlockSpec(memory_space=pl.ANY)],
        out_specs=pl.BlockSpec(memory_space=pl.ANY),
        out_shape=jax.ShapeDtypeStruct((n_rows, D_MODEL), F32),
        scratch_shapes=[pltpu.SemaphoreType.DMA(())],
        input_output_aliases={2: 0},
        compiler_params=_cparams(1),
        name="moe_dispatch",
    )(dest.reshape(steps, 1, 2 * ROUTE_ROWS), hf, init)


def _experts_kernel(te_ref, na_ref, xs_ref, w1_ref, w3_ref, w2_ref, ys_ref, xb_ref):
    i = pl.program_id(0)
    c = pl.program_id(1)
    active = i < na_ref[0]

    @pl.when(jnp.logical_and(active, c == 0))
    def _():
        xb_ref[...] = xs_ref[...].astype(BF16)

    @pl.when(active)
    def _():
        xb = xb_ref[...]
        a = _dot(xb, w1_ref[0])
        b = _dot(xb, w3_ref[0])
        hm = (a * jax.nn.sigmoid(a) * b).astype(BF16)
        y = _dot(hm, w2_ref[0])

        @pl.when(c == 0)
        def _():
            ys_ref[...] = y

        @pl.when(c != 0)
        def _():
            ys_ref[...] += y

    @pl.when(jnp.logical_and(jnp.logical_not(active), c == 0))
    def _():
        ys_ref[...] = jnp.zeros_like(ys_ref)


def _experts(xs, tile_expert, n_active, w1, w3, w2):
    n_rows = xs.shape[0]
    dff = w1.shape[2]
    grid_spec = pltpu.PrefetchScalarGridSpec(
        num_scalar_prefetch=2,
        grid=(n_rows // MOE_TILE, dff // MOE_FF_CHUNK),
        in_specs=[pl.BlockSpec((MOE_TILE, D_MODEL), lambda i, c, te, na: (i, 0)),
                  pl.BlockSpec((1, D_MODEL, MOE_FF_CHUNK), lambda i, c, te, na: (te[i], 0, c)),
                  pl.BlockSpec((1, D_MODEL, MOE_FF_CHUNK), lambda i, c, te, na: (te[i], 0, c)),
                  pl.BlockSpec((1, MOE_FF_CHUNK, D_MODEL), lambda i, c, te, na: (te[i], c, 0))],
        out_specs=pl.BlockSpec((MOE_TILE, D_MODEL), lambda i, c, te, na: (i, 0)),
        scratch_shapes=[pltpu.VMEM((MOE_TILE, D_MODEL), BF16)],
    )
    return pl.pallas_call(
        _experts_kernel,
        grid_spec=grid_spec,
        out_shape=jax.ShapeDtypeStruct((n_rows, D_MODEL), F32),
        compiler_params=_cparams(2),
        name="moe_experts",
    )(tile_expert, n_active, xs, w1, w3, w2)


def _combine_kernel(dest_ref, x_ref, slab_ref, ys_ref, o_ref, y0_ref, y1_ref, sem):
    bufs = (y0_ref, y1_ref)

    def start(r, carry):
        for k in range(2):
            d = dest_ref[0, 0, 2 * r + k]
            _row_copy(ys_ref.at[pl.ds(d, 1)], bufs[k].at[pl.ds(r, 1)], sem).start()
        return carry

    lax.fori_loop(0, ROUTE_ROWS, start, 0)

    def wait(r, carry):
        for k in range(2):
            _row_copy(ys_ref.at[pl.ds(0, 1)], bufs[k].at[pl.ds(0, 1)], sem).wait()
        return carry

    lax.fori_loop(0, ROUTE_ROWS, wait, 0)
    w0 = slab_ref[:, ROUTE_W_LANE:ROUTE_W_LANE + 1]
    w1 = slab_ref[:, ROUTE_W_LANE + 1:ROUTE_W_LANE + 2]
    o_ref[...] = x_ref[...] + w0 * y0_ref[...] + w1 * y1_ref[...]


def _combine(x, slab, dest, ys):
    n = x.shape[0]
    steps = n // ROUTE_ROWS
    row = pl.BlockSpec((ROUTE_ROWS, D_MODEL), lambda i: (i, 0))
    return pl.pallas_call(
        _combine_kernel,
        grid=(steps,),
        in_specs=[pl.BlockSpec((1, 1, 2 * ROUTE_ROWS), lambda i: (i, 0, 0), memory_space=pltpu.SMEM),
                  row, pl.BlockSpec((ROUTE_ROWS, LANES), lambda i: (i, 0)),
                  pl.BlockSpec(memory_space=pl.ANY)],
        out_specs=row,
        out_shape=jax.ShapeDtypeStruct((n, D_MODEL), F32),
        scratch_shapes=[pltpu.VMEM((ROUTE_ROWS, D_MODEL), F32), pltpu.VMEM((ROUTE_ROWS, D_MODEL), F32),
                        pltpu.SemaphoreType.DMA(())],
        input_output_aliases={1: 0},
        compiler_params=_cparams(1),
        name="moe_combine",
    )(dest.reshape(steps, 1, 2 * ROUTE_ROWS), x, slab, ys)


def _routing_plan(slab, n_rows):
    n = slab.shape[0]
    idx = slab[:, ROUTE_IDX_LANE:ROUTE_IDX_LANE + 2].astype(jnp.int32)
    onehot = (idx[:, :, None] == jnp.arange(N_EXPERTS, dtype=jnp.int32)).astype(jnp.int32).sum(axis=1)
    csum = jnp.cumsum(onehot, axis=0)
    rank = csum - onehot
    counts = csum[-1]
    padded = ((counts + MOE_TILE - 1) // MOE_TILE) * MOE_TILE
    ends = jnp.cumsum(padded)
    offs = ends - padded
    dest = offs[idx] + jnp.take_along_axis(rank, idx, axis=1)
    tile_start = jnp.arange(n_rows // MOE_TILE, dtype=jnp.int32) * MOE_TILE
    tile_expert = jnp.minimum(jnp.sum(tile_start[:, None] >= ends[None, :], axis=1), N_EXPERTS - 1)
    n_active = (ends[-1] // MOE_TILE).reshape(1)
    return dest.reshape(-1).astype(jnp.int32), tile_expert.astype(jnp.int32), n_active.astype(jnp.int32)


def _moe(hf, x, slab, w1, w3, w2):
    n = x.shape[0]
    n_rows = 2 * n + N_EXPERTS * MOE_TILE
    dest, tile_expert, n_active = _routing_plan(slab, n_rows)
    xs = _dispatch(hf, dest, n_rows)
    ys = _experts(xs, tile_expert, n_active, w1, w3, w2)
    return _combine(x, slab, dest, ys)


def _prep_inproj(w_in_l, gate_b_l, qn_b, kn_b, qn_c, kn_c):
    a_in = 2 * CONV_CH
    b_q = WB_HEADS * HEAD_DIM
    b_kv = WB_KV_HEADS * HEAD_DIM
    c_w = NA_HEADS * HEAD_DIM
    o = 0
    wa = w_in_l[:, o:o + a_in]; o += a_in
    wbq = w_in_l[:, o:o + b_q]; o += b_q
    wbk = w_in_l[:, o:o + b_kv]; o += b_kv
    wbv = w_in_l[:, o:o + b_kv]; o += b_kv
    wcq = w_in_l[:, o:o + c_w]; o += c_w
    wck = w_in_l[:, o:o + c_w]; o += c_w
    wcv = w_in_l[:, o:o + c_w]; o += c_w
    wg = w_in_l[:, o:]

    def dup(w):
        return jnp.repeat(w.reshape(D_MODEL, WB_KV_HEADS, 1, HEAD_DIM), 2, axis=2).reshape(D_MODEL, -1)

    w = jnp.concatenate([wg, wa, wbq, dup(wbk), dup(wbv), wcq, wck, wcv], axis=1).astype(BF16)
    scale = HEAD_DIM ** -0.5
    ones = lambda k: jnp.ones((k,), F32)
    cg = jnp.concatenate([
        ones(ZBQ), jnp.tile(qn_b.astype(F32), WB_HEADS) * scale, jnp.tile(kn_b.astype(F32), 2 * WB_KV_HEADS),
        ones(ZCQ - ZBV), jnp.tile(qn_c.astype(F32), NA_HEADS) * scale, jnp.tile(kn_c.astype(F32), NA_HEADS),
        ones(ZC - ZCV)])
    return w, cg.reshape(1, ZC), gate_b_l.astype(F32).reshape(1, -1)


def _segment_mean_matrix():
    idx = np.arange(ZCHUNK) // HEAD_DIM
    return jnp.asarray((idx[:, None] == idx[None, :]).astype(np.float32) / HEAD_DIM, dtype=BF16)


def kernel(x, t5_rel_bias, attn_norm_g, w_in, gate_b, conv_dw_w, conv_dw_b, conv_ln_g, conv_ln_b,
           conv_w_out, wb_qn_g, wb_kn_g, wb_sink, wb_w_out, na_qn_g, na_kn_g, na_rpb, na_w_out, w_o,
           ffn_norm_g, ffn_w1, ffn_w3, ffn_w2, moe_w_router, moe_b_router, moe_w1, moe_w3, moe_w2):
    bsz, s, d = x.shape
    depth = w_in.shape[0]
    assert d == D_MODEL and s % BLOCK == 0 and s % (NA_QROWS * GRID_W) == 0
    n = bsz * s
    xf = x.reshape(n, d).astype(F32)
    row = lambda v: v.astype(F32).reshape(1, -1)

    t5_bias = _t5_bias(t5_rel_bias)
    na_bias = _na_bias(na_rpb, s // GRID_W)
    bd = _segment_mean_matrix()

    for layer in range(depth):
        w, cg, gb = _prep_inproj(w_in[layer], gate_b[layer], wb_qn_g[layer], wb_kn_g[layer],
                                 na_qn_g[layer], na_kn_g[layer])
        z = _inproj(xf, row(attn_norm_g[layer]), w, cg, gb, bd)
        u = _conv_mixer(z, bsz, s, conv_dw_w[layer].astype(F32), row(conv_dw_b[layer]),
                        row(conv_ln_g[layer]), row(conv_ln_b[layer]))
        ob = _window_attention(z, bsz, s, row(wb_sink[layer]), t5_bias)
        oc = _neighbourhood_attention(z, bsz, s, na_bias, layer)
        weights = (conv_w_out[layer].astype(BF16), wb_w_out[layer].astype(BF16),
                   na_w_out[layer].astype(BF16), w_o[layer].astype(BF16), row(ffn_norm_g[layer]))
        i = layer // 2
        if layer % 2 == 0:
            xf, hf = _merge(u, ob, oc, z, xf, *weights)
            xf = _dense_ffn(hf, xf, ffn_w1[i].astype(BF16), ffn_w3[i].astype(BF16), ffn_w2[i].astype(BF16))
        else:
            wr = jnp.zeros((D_MODEL, LANES), F32).at[:, :N_EXPERTS].set(moe_w_router[i].astype(F32))
            br = jnp.full((1, LANES), -jnp.inf, F32).at[0, :N_EXPERTS].set(moe_b_router[i].astype(F32))
            xf, hf, slab = _merge(u, ob, oc, z, xf, *weights, router=(wr, br))
            xf = _moe(hf, xf, slab, moe_w1[i].astype(BF16), moe_w3[i].astype(BF16), moe_w2[i].astype(BF16))
    return xf.reshape(bsz, s, d).astype(x.dtype)
```

```python
import functools
import math

import numpy as np
import jax
import jax.numpy as jnp
from jax import lax
from jax.experimental import pallas as pl
from jax.experimental.pallas import tpu as pltpu

D_MODEL = 1024
HEAD_DIM = 64
CONV_CH = 512
CONV_WIDTH = 31
WB_HEADS = 8
WB_KV_HEADS = 2
WINDOW = 128
BLOCK = 128
NA_HEADS = 8
GRID_W = 64
NA_KH = 8
NA_KW = 16
REL_BUCKETS = 32
REL_MAX_DIST = 128
N_EXPERTS = 8
EPS = 1e-6
NEG_INF = -1e30

LANES = 128

ZG = 0
ZA = 3072
ZBQ = 4096
ZBK = 4608
ZBV = 4864
ZCQ = 5120
ZCK = 5632
ZCV = 6144
ZC = 6656
ZCHUNK = 512

VMEM_LIMIT = 56 * 1024 * 1024

F32 = jnp.float32
BF16 = jnp.bfloat16


def _cparams(grid_rank, vmem=VMEM_LIMIT):
    return pltpu.CompilerParams(dimension_semantics=("arbitrary",) * grid_rank, vmem_limit_bytes=vmem)


def _resident(shape):
    nd = len(shape)
    return pl.BlockSpec(shape, lambda *_: (0,) * nd, pipeline_mode=pl.Buffered(1))


def _dot(a, b):
    return jnp.dot(a, b, preferred_element_type=F32)


def _dot_nt(a, b):
    return lax.dot_general(a, b, (((1,), (1,)), ((), ())), preferred_element_type=F32)


def _t5_bucket_np(rel):
    half = REL_BUCKETS // 2
    max_exact = half // 2
    ret = np.where(rel > 0, half, 0)
    n = np.abs(rel)
    nf = np.maximum(n, 1).astype(np.float32)
    large = max_exact + (np.log(nf / np.float32(max_exact)) / np.float32(math.log(REL_MAX_DIST / max_exact))
                         * np.float32(half - max_exact)).astype(np.int32)
    large = np.minimum(large, half - 1)
    return ret + np.where(n < max_exact, n, large)


def _window_bucket_table():
    col = np.arange(3 * BLOCK)[None, :]
    rel = (col - BLOCK) - np.arange(BLOCK)[:, None]
    band = np.where(np.abs(rel) <= WINDOW, _t5_bucket_np(rel), -1)
    first = np.where(col >= BLOCK, band, -1)
    last = np.where(col < 2 * BLOCK, band, -1)
    return np.stack([first, band, last]).astype(np.int32)


NA_QROWS = 2
NA_KROWS = 10


def _na_tables(rows):
    kh = min(NA_KH, rows)
    col = np.arange(GRID_W)
    col_start = np.clip(col - NA_KW // 2, 0, GRID_W - NA_KW)
    kc = np.arange(GRID_W)[None, :]
    inwin = (kc >= col_start[:, None]) & (kc < col_start[:, None] + NA_KW)
    dmat = np.where(inwin, kc - col[:, None] + NA_KW - 1, -1).astype(np.int32)
    variants, variant_of_pair, kstart = [], [], []
    for i in range(rows // NA_QROWS):
        ks = int(np.clip(NA_QROWS * i - kh // 2, 0, rows - NA_KROWS))
        tab = np.full((NA_QROWS, NA_KROWS), -1, np.int64)
        for a in range(NA_QROWS):
            r = NA_QROWS * i + a
            r0 = int(np.clip(r - kh // 2, 0, rows - kh))
            assert ks <= r0 and r0 + kh <= ks + NA_KROWS
            for kr in range(NA_KROWS):
                if r0 <= ks + kr < r0 + kh:
                    tab[a, kr] = ks + kr - r + NA_KH - 1
        for v, t in enumerate(variants):
            if np.array_equal(t, tab):
                variant_of_pair.append(v)
                break
        else:
            variant_of_pair.append(len(variants))
            variants.append(tab)
        kstart.append(ks)
    return dmat, variants, variant_of_pair, kstart


def _t5_bias_kernel(tab_ref, bucket_ref, o_ref):
    h = pl.program_id(0)
    for v in range(bucket_ref.shape[0]):
        bucket = bucket_ref[v]
        acc = jnp.full(bucket.shape, NEG_INF, F32)
        for b in range(REL_BUCKETS):
            acc = jnp.where(bucket == b, tab_ref[b, h], acc)
        o_ref[0, v] = acc


def _t5_bias(t5_rel_bias):
    bucket = jnp.asarray(_window_bucket_table())
    nv = bucket.shape[0]
    return pl.pallas_call(
        _t5_bias_kernel,
        grid=(WB_HEADS,),
        in_specs=[pl.BlockSpec(memory_space=pltpu.SMEM),
                  pl.BlockSpec((nv, BLOCK, 3 * BLOCK), lambda h: (0, 0, 0))],
        out_specs=pl.BlockSpec((1, nv, BLOCK, 3 * BLOCK), lambda h: (h, 0, 0, 0)),
        out_shape=jax.ShapeDtypeStruct((WB_HEADS, nv, BLOCK, 3 * BLOCK), F32),
        compiler_params=_cparams(1),
        name="t5_bias",
    )(t5_rel_bias.astype(F32), bucket)


def _na_bias_kernel(variants, rpb_ref, dmat_ref, o_ref):
    lh = pl.program_id(0) * NA_HEADS + pl.program_id(1)
    dmat = dmat_ref[...]
    ncol = 2 * NA_KW - 1
    base = []
    for dr in range(2 * NA_KH - 1):
        acc = jnp.full(dmat.shape, NEG_INF, F32)
        for d in range(ncol):
            acc = jnp.where(dmat == d, rpb_ref[lh, dr * ncol + d], acc)
        base.append(acc)
    neg = jnp.full(dmat.shape, NEG_INF, F32)
    for v, tab in enumerate(variants):
        for a in range(NA_QROWS):
            for kr in range(NA_KROWS):
                dr = int(tab[a, kr])
                tile = neg if dr < 0 else base[dr]
                o_ref[0, v, 0, a * GRID_W:(a + 1) * GRID_W, kr * GRID_W:(kr + 1) * GRID_W] = tile


def _na_bias(na_rpb, rows):
    depth = na_rpb.shape[0]
    dmat, variants, _, _ = _na_tables(rows)
    nv = len(variants)
    rpb = jnp.transpose(na_rpb.astype(F32), (0, 3, 1, 2)).reshape(depth * NA_HEADS, -1)
    qn, kn = NA_QROWS * GRID_W, NA_KROWS * GRID_W
    return pl.pallas_call(
        functools.partial(_na_bias_kernel, variants),
        grid=(depth, NA_HEADS),
        in_specs=[pl.BlockSpec(memory_space=pltpu.SMEM),
                  pl.BlockSpec((GRID_W, GRID_W), lambda l, h: (0, 0))],
        out_specs=pl.BlockSpec((1, nv, 1, qn, kn), lambda l, h: (l, 0, h, 0, 0)),
        out_shape=jax.ShapeDtypeStruct((depth, nv, NA_HEADS, qn, kn), F32),
        compiler_params=_cparams(2),
        name="na_bias",
    )(rpb, jnp.asarray(dmat))


def _inproj_kernel(x_ref, ng_ref, w_ref, cg_ref, gb_ref, bd_ref, z_ref, h_ref):
    x = x_ref[...]
    inv = lax.rsqrt(jnp.mean(x * x, axis=-1, keepdims=True) + EPS)
    h_ref[...] = (x * inv * ng_ref[...]).astype(BF16)

    def qknorm(zc, c0):
        w = zc.shape[1]
        ms = _dot((zc * zc).astype(BF16), bd_ref[:w, :w])
        return zc * lax.rsqrt(ms + EPS) * cg_ref[:, c0:c0 + w]

    for c0 in range(0, ZC, ZCHUNK):
        zc = _dot(h_ref[...], w_ref[:, c0:c0 + ZCHUNK])
        if c0 < ZA:
            out = jax.nn.sigmoid(zc + gb_ref[:, c0:c0 + ZCHUNK])
        elif c0 in (ZBQ, ZCQ, ZCK):
            out = qknorm(zc, c0)
        elif c0 == ZBK:
            half = ZBV - ZBK
            out = jnp.concatenate([qknorm(zc[:, :half], c0), zc[:, half:]], axis=1)
        else:
            out = zc
        z_ref[:, c0:c0 + ZCHUNK] = out.astype(BF16)


def _inproj(x, ng, w, cg, gb, bd, tm=512):
    n = x.shape[0]
    return pl.pallas_call(
        _inproj_kernel,
        grid=(n // tm,),
        in_specs=[pl.BlockSpec((tm, D_MODEL), lambda i: (i, 0)),
                  _resident((1, D_MODEL)), _resident((D_MODEL, ZC)), _resident((1, ZC)),
                  _resident((1, ZA)), _resident((ZCHUNK, ZCHUNK))],
        out_specs=pl.BlockSpec((tm, ZC), lambda i: (i, 0)),
        out_shape=jax.ShapeDtypeStruct((n, ZC), BF16),
        scratch_shapes=[pltpu.VMEM((tm, D_MODEL), BF16)],
        compiler_params=_cparams(1),
        name="inproj",
    )(x, ng, w, cg, gb, bd)


SUBLANES = 8
CONV_PAD = 16
CONV_HALF = 256
CONV_ROWS = 64
CONV_SHIFT_ROWS = 104


def _conv_kernel(z_ref, dw_ref, db_ref, lg_ref, lb_ref, u_ref, vs_ref, c_ref):
    s = z_ref.shape[0]
    span = s + 2 * CONV_PAD
    first = CONV_PAD - CONV_WIDTH // 2
    glu_rows = 256

    for half in range(CONV_CH // CONV_HALF):
        l0 = half * CONV_HALF
        vs_ref[0, 0:CONV_PAD, :] = jnp.zeros((CONV_PAD, CONV_HALF), F32)
        vs_ref[0, CONV_PAD + s:span + SUBLANES, :] = jnp.zeros((CONV_PAD + SUBLANES, CONV_HALF), F32)

        def glu(i, carry):
            r = pl.multiple_of(i * glu_rows, glu_rows)
            a = z_ref[pl.ds(r, glu_rows), l0:l0 + CONV_HALF].astype(F32)
            g = z_ref[pl.ds(r, glu_rows), CONV_CH + l0:CONV_CH + l0 + CONV_HALF].astype(F32)
            vs_ref[0, pl.ds(CONV_PAD + r, glu_rows), :] = a * jax.nn.sigmoid(g)
            return carry

        lax.fori_loop(0, s // glu_rows, glu, 0)

        def shift(i, carry):
            r = pl.multiple_of(i * CONV_SHIFT_ROWS, SUBLANES)
            win = vs_ref[0, pl.ds(r, CONV_SHIFT_ROWS + SUBLANES), :]
            for k in range(1, SUBLANES):
                vs_ref[k, pl.ds(r, CONV_SHIFT_ROWS), :] = win[k:k + CONV_SHIFT_ROWS]
            return carry

        assert span % CONV_SHIFT_ROWS == 0
        lax.fori_loop(0, span // CONV_SHIFT_ROWS, shift, 0)

        def conv(i, carry):
            r = pl.multiple_of(i * CONV_ROWS, CONV_ROWS)
            acc = jnp.zeros((CONV_ROWS, CONV_HALF), F32) + db_ref[:, l0:l0 + CONV_HALF]
            for w in range(CONV_WIDTH):
                k, a = (first + w) % SUBLANES, (first + w) // SUBLANES
                tap = vs_ref[k, pl.ds(r + a * SUBLANES, CONV_ROWS), :]
                acc = acc + tap * dw_ref[w:w + 1, l0:l0 + CONV_HALF]
            c_ref[pl.ds(r, CONV_ROWS), l0:l0 + CONV_HALF] = acc
            return carry

        lax.fori_loop(0, s // CONV_ROWS, conv, 0)

    def norm(i, carry):
        r = pl.multiple_of(i * CONV_ROWS, CONV_ROWS)
        c = c_ref[pl.ds(r, CONV_ROWS), :]
        xc = c - jnp.mean(c, axis=-1, keepdims=True)
        var = jnp.mean(xc * xc, axis=-1, keepdims=True)
        y = xc * lax.rsqrt(var + EPS) * lg_ref[...] + lb_ref[...]
        u_ref[pl.ds(r, CONV_ROWS), :] = (y * jax.nn.sigmoid(y)).astype(BF16)
        return carry

    lax.fori_loop(0, s // CONV_ROWS, norm, 0)


def _conv_mixer(z, bsz, s, dw, db, lg, lb):
    return pl.pallas_call(
        _conv_kernel,
        grid=(bsz,),
        in_specs=[pl.BlockSpec((s, 2 * CONV_CH), lambda b: (b, ZA // (2 * CONV_CH))),
                  _resident((CONV_WIDTH, CONV_CH)), _resident((1, CONV_CH)),
                  _resident((1, CONV_CH)), _resident((1, CONV_CH))],
        out_specs=pl.BlockSpec((s, CONV_CH), lambda b: (b, 0)),
        out_shape=jax.ShapeDtypeStruct((bsz * s, CONV_CH), BF16),
        scratch_shapes=[pltpu.VMEM((SUBLANES, s + 2 * CONV_PAD + SUBLANES, CONV_HALF), F32),
                        pltpu.VMEM((s, CONV_CH), F32)],
        compiler_params=_cparams(1),
        name="conv_mixer",
    )(z, dw, db, lg, lb)


def _head_masks():
    lane = lax.broadcasted_iota(jnp.int32, (1, LANES), 1)
    return lane < HEAD_DIM


ATTN_UNROLL = 4


def _softmax_pv(sc, vw, sink=None):
    m = jnp.max(sc, axis=-1, keepdims=True)
    if sink is not None:
        m = jnp.maximum(m, sink)
    p = jnp.exp(sc - m)
    l = jnp.sum(p, axis=-1, keepdims=True)
    if sink is not None:
        l = l + jnp.exp(sink - m)
    return _dot(p.astype(BF16), vw) * (1.0 / l)


def _wattn_kernel(pairs_per_kv, sink_ref, q_ref, k_ref, v_ref, bias_ref, o_ref, kp_ref, vp_ref):
    hp = pl.program_id(1)
    s = q_ref.shape[0]
    nb = s // BLOCK
    lo = _head_masks()
    zero = jnp.zeros((), BF16)

    @pl.when(hp % pairs_per_kv == 0)
    def _():
        pad = jnp.zeros((BLOCK, LANES), BF16)
        for src, dst in ((k_ref, kp_ref), (v_ref, vp_ref)):
            dst[0:BLOCK, :] = pad
            dst[BLOCK:BLOCK + s, :] = src[...]
            dst[BLOCK + s:2 * BLOCK + s, :] = pad

    def group(g, carry):
        rows, scores = [], []
        for u in range(ATTN_UNROLL):
            n = g * ATTN_UNROLL + u
            r = pl.multiple_of(n * BLOCK, BLOCK)
            var = jnp.where(n == 0, 0, jnp.where(n == nb - 1, 2, 1))
            q = q_ref[pl.ds(r, BLOCK), :]
            kw = kp_ref[pl.ds(r, 3 * BLOCK), :]
            rows.append(r)
            for hh in range(2):
                qm = jnp.where(lo if hh == 0 else jnp.logical_not(lo), q, zero)
                scores.append(_dot_nt(qm, kw) + bias_ref[hh, var])
        for u, r in enumerate(rows):
            vw = vp_ref[pl.ds(r, 3 * BLOCK), :]
            outs = [_softmax_pv(scores[2 * u + hh], vw, sink_ref[0, 2 * hp + hh]) for hh in range(2)]
            o_ref[pl.ds(r, BLOCK), :] = jnp.where(lo, outs[0], outs[1]).astype(BF16)
        return carry

    lax.fori_loop(0, nb // ATTN_UNROLL, group, 0)


def _window_attention(z, bsz, s, sink, bias):
    npair = WB_HEADS // 2
    pairs_per_kv = npair // WB_KV_HEADS
    nv = bias.shape[1]
    assert (s // BLOCK) % ATTN_UNROLL == 0 and s // BLOCK >= 2
    return pl.pallas_call(
        functools.partial(_wattn_kernel, pairs_per_kv),
        grid=(bsz, npair),
        in_specs=[pl.BlockSpec(memory_space=pltpu.SMEM),
                  pl.BlockSpec((s, LANES), lambda b, hp: (b, ZBQ // LANES + hp)),
                  pl.BlockSpec((s, LANES), lambda b, hp: (b, ZBK // LANES + hp // pairs_per_kv)),
                  pl.BlockSpec((s, LANES), lambda b, hp: (b, ZBV // LANES + hp // pairs_per_kv)),
                  pl.BlockSpec((2, nv, BLOCK, 3 * BLOCK), lambda b, hp: (hp, 0, 0, 0))],
        out_specs=pl.BlockSpec((s, LANES), lambda b, hp: (b, hp)),
        out_shape=jax.ShapeDtypeStruct((bsz * s, WB_HEADS * HEAD_DIM), BF16),
        scratch_shapes=[pltpu.VMEM((s + 2 * BLOCK, LANES), BF16), pltpu.VMEM((s + 2 * BLOCK, LANES), BF16)],
        compiler_params=_cparams(2),
        name="window_attention",
    )(sink, z, z, z, bias)


def _nattn_kernel(n_edge, n_pairs, q_ref, k_ref, v_ref, bias_ref, o_ref):
    lo = _head_masks()
    zero = jnp.zeros((), BF16)
    qn, kn = NA_QROWS * GRID_W, NA_KROWS * GRID_W
    max_ks = k_ref.shape[0] // GRID_W - NA_KROWS

    def group(g, carry):
        rows, scores = [], []
        for u in range(ATTN_UNROLL):
            i = g * ATTN_UNROLL + u
            r = pl.multiple_of(i * qn, qn)
            ks = jnp.clip(NA_QROWS * i - NA_KH // 2, 0, max_ks)
            rk = pl.multiple_of(ks * GRID_W, LANES)
            var = jnp.where(i < n_edge, i,
                            jnp.where(i >= n_pairs - n_edge, i - (n_pairs - 2 * n_edge - 1), n_edge))
            q = q_ref[pl.ds(r, qn), :]
            kw = k_ref[pl.ds(rk, kn), :]
            rows.append((r, rk))
            for hh in range(2):
                qm = jnp.where(lo if hh == 0 else jnp.logical_not(lo), q, zero)
                scores.append(_dot_nt(qm, kw) + bias_ref[0, var, hh])
        for u, (r, rk) in enumerate(rows):
            vw = v_ref[pl.ds(rk, kn), :]
            outs = [_softmax_pv(scores[2 * u + hh], vw) for hh in range(2)]
            o_ref[pl.ds(r, qn), :] = jnp.where(lo, outs[0], outs[1]).astype(BF16)
        return carry

    lax.fori_loop(0, n_pairs // ATTN_UNROLL, group, 0)


def _neighbourhood_attention(z, bsz, s, bias, layer):
    rows = s // GRID_W
    _, variants, variant_of_pair, kstart = _na_tables(rows)
    n_pairs = rows // NA_QROWS
    nv = len(variants)
    n_edge = (nv - 1) // 2
    assert n_pairs % ATTN_UNROLL == 0
    for i in range(n_pairs):
        want = i if i < n_edge else (i - (n_pairs - 2 * n_edge - 1) if i >= n_pairs - n_edge else n_edge)
        assert variant_of_pair[i] == want
        assert kstart[i] == int(np.clip(NA_QROWS * i - NA_KH // 2, 0, rows - NA_KROWS))
        assert (kstart[i] * GRID_W) % LANES == 0
    npair = NA_HEADS // 2
    qn, kn = NA_QROWS * GRID_W, NA_KROWS * GRID_W
    return pl.pallas_call(
        functools.partial(_nattn_kernel, n_edge, n_pairs),
        grid=(npair, bsz),
        in_specs=[pl.BlockSpec((s, LANES), lambda hp, b: (b, ZCQ // LANES + hp)),
                  pl.BlockSpec((s, LANES), lambda hp, b: (b, ZCK // LANES + hp)),
                  pl.BlockSpec((s, LANES), lambda hp, b: (b, ZCV // LANES + hp)),
                  pl.BlockSpec((1, nv, 2, qn, kn), lambda hp, b: (layer, 0, hp, 0, 0))],
        out_specs=pl.BlockSpec((s, LANES), lambda hp, b: (b, hp)),
        out_shape=jax.ShapeDtypeStruct((bsz * s, NA_HEADS * HEAD_DIM), BF16),
        compiler_params=_cparams(2),
        name="neighbourhood_attention",
    )(z, z, z, bias)


ROUTE_IDX_LANE = N_EXPERTS
ROUTE_W_LANE = N_EXPERTS + 2


def _merge_kernel(with_router, u_ref, ob_ref, oc_ref, g_ref, x_ref, wa_ref, wb_ref, wc_ref, wo_ref,
                  fg_ref, *rest):
    if with_router:
        wrh_ref, wrl_ref, br_ref, xo_ref, hf_ref, slab_ref = rest
    else:
        xo_ref, hf_ref = rest
    d = D_MODEL
    m = g_ref[:, 0:d].astype(F32) * _dot(u_ref[...], wa_ref[...])
    m = m + g_ref[:, d:2 * d].astype(F32) * _dot(ob_ref[...], wb_ref[...])
    m = m + g_ref[:, 2 * d:3 * d].astype(F32) * _dot(oc_ref[...], wc_ref[...])
    xn = x_ref[...] + _dot(m.astype(BF16), wo_ref[...])
    xo_ref[...] = xn
    hf = xn * lax.rsqrt(jnp.mean(xn * xn, axis=-1, keepdims=True) + EPS) * fg_ref[...]
    hf_ref[...] = hf.astype(hf_ref.dtype)
    if with_router:
        hf_hi = hf.astype(BF16)
        hf_lo = (hf - hf_hi.astype(F32)).astype(BF16)
        logits = (_dot(hf_hi, wrh_ref[...]) + _dot(hf_lo, wrh_ref[...]) + _dot(hf_hi, wrl_ref[...])
                  + br_ref[...])
        lane = lax.broadcasted_iota(jnp.int32, logits.shape, 1)
        m1 = jnp.max(logits, axis=-1, keepdims=True)
        i1 = jnp.min(jnp.where(logits == m1, lane, LANES), axis=-1, keepdims=True)
        rest_l = jnp.where(lane == i1, -jnp.inf, logits)
        m2 = jnp.max(rest_l, axis=-1, keepdims=True)
        i2 = jnp.min(jnp.where(rest_l == m2, lane, LANES), axis=-1, keepdims=True)
        e2 = jnp.exp(m2 - m1)
        w1 = 1.0 / (1.0 + e2)
        w2 = e2 / (1.0 + e2)
        slab = jnp.where(lane == ROUTE_IDX_LANE, i1.astype(F32), 0.0)
        slab = jnp.where(lane == ROUTE_IDX_LANE + 1, i2.astype(F32), slab)
        slab = jnp.where(lane == ROUTE_W_LANE, w1, slab)
        slab = jnp.where(lane == ROUTE_W_LANE + 1, w2, slab)
        slab_ref[...] = slab


def _merge(u, ob, oc, z, x, wa, wb, wc, wo, fg, router=None, tm=512):
    n = x.shape[0]
    with_router = router is not None
    row = lambda w: pl.BlockSpec((tm, w), lambda i: (i, 0))
    in_specs = [row(CONV_CH), row(WB_HEADS * HEAD_DIM), row(NA_HEADS * HEAD_DIM),
                pl.BlockSpec((tm, ZA), lambda i: (i, 0)), row(D_MODEL),
                _resident(wa.shape), _resident(wb.shape), _resident(wc.shape), _resident(wo.shape),
                _resident((1, D_MODEL))]
    args = [u, ob, oc, z, x, wa, wb, wc, wo, fg]
    out_specs = [row(D_MODEL), row(D_MODEL)]
    out_shape = [jax.ShapeDtypeStruct((n, D_MODEL), F32),
                 jax.ShapeDtypeStruct((n, D_MODEL), F32 if with_router else BF16)]
    if with_router:
        in_specs += [_resident((D_MODEL, LANES)), _resident((D_MODEL, LANES)), _resident((1, LANES))]
        args += list(router)
        out_specs.append(row(LANES))
        out_shape.append(jax.ShapeDtypeStruct((n, LANES), F32))
    return pl.pallas_call(
        functools.partial(_merge_kernel, with_router),
        grid=(n // tm,),
        in_specs=in_specs, out_specs=out_specs, out_shape=out_shape,
        input_output_aliases={4: 0},
        compiler_params=_cparams(1),
        name="merge_router" if with_router else "merge",
    )(*args)


FFN_CHUNK = 1024


def _ffn_kernel(hf_ref, x_ref, w1_ref, w3_ref, w2_ref, o_ref):
    dff = w1_ref.shape[1]
    hf = hf_ref[...]
    acc = x_ref[...]
    for c0 in range(0, dff, FFN_CHUNK):
        c1 = min(c0 + FFN_CHUNK, dff)
        a = _dot(hf, w1_ref[:, c0:c1])
        b = _dot(hf, w3_ref[:, c0:c1])
        hm = (a * jax.nn.sigmoid(a) * b).astype(BF16)
        acc = acc + _dot(hm, w2_ref[c0:c1, :])
    o_ref[...] = acc


def _dense_ffn(hf, x, w1, w3, w2, tm=512):
    n = x.shape[0]
    row = pl.BlockSpec((tm, D_MODEL), lambda i: (i, 0))
    return pl.pallas_call(
        _ffn_kernel,
        grid=(n // tm,),
        in_specs=[row, row, _resident(w1.shape), _resident(w3.shape), _resident(w2.shape)],
        out_specs=row,
        out_shape=jax.ShapeDtypeStruct((n, D_MODEL), F32),
        input_output_aliases={1: 0},
        compiler_params=_cparams(1),
        name="dense_ffn",
    )(hf, x, w1, w3, w2)


MOE_TILE = 1024
MOE_FF_CHUNK = 512
ROUTE_ROWS = 256


def _row_copy(src, dst, sem):
    return pltpu.make_async_copy(src, dst, sem)


def _dispatch_kernel(dest_ref, hf_ref, init_ref, xs_ref, sem):
    del init_ref
    i = pl.program_id(0)
    slot = i % 2
    base = i * ROUTE_ROWS

    def start(r, carry):
        for k in range(2):
            d = dest_ref[0, 0, 2 * r + k]
            _row_copy(hf_ref.at[pl.ds(base + r, 1)], xs_ref.at[pl.ds(d, 1)], sem.at[slot]).start()
        return carry

    lax.fori_loop(0, ROUTE_ROWS, start, 0, unroll=8)

    def drain(slot_):
        for _ in range(2 * ROUTE_ROWS):
            _row_copy(hf_ref.at[pl.ds(0, 1)], xs_ref.at[pl.ds(0, 1)], sem.at[slot_]).wait()

    @pl.when(i > 0)
    def _():
        drain(1 - slot)

    @pl.when(i == pl.num_programs(0) - 1)
    def _():
        drain(slot)


def _dispatch(hf, dest, n_rows):
    n = hf.shape[0]
    steps = n // ROUTE_ROWS
    init = jnp.zeros((n_rows, D_MODEL), F32)
    return pl.pallas_call(
        _dispatch_kernel,
        grid=(steps,),
        in_specs=[pl.BlockSpec((1, 1, 2 * ROUTE_ROWS), lambda i: (i, 0, 0), memory_space=pltpu.SMEM),
                  pl.BlockSpec(memory_space=pl.ANY),
                  pl.BlockSpec(memory_space=pl.ANY)],
        out_specs=pl.BlockSpec(memory_space=pl.ANY),
        out_shape=jax.ShapeDtypeStruct((n_rows, D_MODEL), F32),
        scratch_shapes=[pltpu.SemaphoreType.DMA((2,))],
        input_output_aliases={2: 0},
        compiler_params=_cparams(1),
        name="moe_dispatch",
    )(dest.reshape(steps, 1, 2 * ROUTE_ROWS), hf, init)


def _experts_kernel(te_ref, na_ref, xs_ref, w1_ref, w3_ref, w2_ref, ys_ref, xb_ref):
    i = pl.program_id(0)
    c = pl.program_id(1)
    active = i < na_ref[0]

    @pl.when(jnp.logical_and(active, c == 0))
    def _():
        xb_ref[...] = xs_ref[...].astype(BF16)

    @pl.when(active)
    def _():
        xb = xb_ref[...]
        a = _dot(xb, w1_ref[0])
        b = _dot(xb, w3_ref[0])
        hm = (a * jax.nn.sigmoid(a) * b).astype(BF16)
        y = _dot(hm, w2_ref[0])

        @pl.when(c == 0)
        def _():
            ys_ref[...] = y

        @pl.when(c != 0)
        def _():
            ys_ref[...] += y

    @pl.when(jnp.logical_and(jnp.logical_not(active), c == 0))
    def _():
        ys_ref[...] = jnp.zeros_like(ys_ref)


def _experts(xs, tile_expert, n_active, w1, w3, w2):
    n_rows = xs.shape[0]
    dff = w1.shape[2]
    grid_spec = pltpu.PrefetchScalarGridSpec(
        num_scalar_prefetch=2,
        grid=(n_rows // MOE_TILE, dff // MOE_FF_CHUNK),
        in_specs=[pl.BlockSpec((MOE_TILE, D_MODEL), lambda i, c, te, na: (i, 0)),
                  pl.BlockSpec((1, D_MODEL, MOE_FF_CHUNK), lambda i, c, te, na: (te[i], 0, c)),
                  pl.BlockSpec((1, D_MODEL, MOE_FF_CHUNK), lambda i, c, te, na: (te[i], 0, c)),
                  pl.BlockSpec((1, MOE_FF_CHUNK, D_MODEL), lambda i, c, te, na: (te[i], c, 0))],
        out_specs=pl.BlockSpec((MOE_TILE, D_MODEL), lambda i, c, te, na: (i, 0)),
        scratch_shapes=[pltpu.VMEM((MOE_TILE, D_MODEL), BF16)],
    )
    return pl.pallas_call(
        _experts_kernel,
        grid_spec=grid_spec,
        out_shape=jax.ShapeDtypeStruct((n_rows, D_MODEL), F32),
        compiler_params=_cparams(2),
        name="moe_experts",
    )(tile_expert, n_active, xs, w1, w3, w2)


def _combine_kernel(dest_ref, next_dest_ref, x_ref, slab_ref, ys_ref, o_ref, y_ref, sem):
    i = pl.program_id(0)
    slot = i % 2

    def gather(d_ref, slot_):
        def start(r, carry):
            for k in range(2):
                d = d_ref[0, 0, 2 * r + k]
                _row_copy(ys_ref.at[pl.ds(d, 1)], y_ref.at[slot_, k, pl.ds(r, 1)], sem.at[slot_]).start()
            return carry

        lax.fori_loop(0, ROUTE_ROWS, start, 0, unroll=8)

    @pl.when(i == 0)
    def _():
        gather(dest_ref, 0)

    @pl.when(i + 1 < pl.num_programs(0))
    def _():
        gather(next_dest_ref, 1 - slot)

    for _ in range(2 * ROUTE_ROWS):
        _row_copy(ys_ref.at[pl.ds(0, 1)], y_ref.at[slot, 0, pl.ds(0, 1)], sem.at[slot]).wait()
    w0 = slab_ref[:, ROUTE_W_LANE:ROUTE_W_LANE + 1]
    w1 = slab_ref[:, ROUTE_W_LANE + 1:ROUTE_W_LANE + 2]
    o_ref[...] = x_ref[...] + w0 * y_ref[slot, 0] + w1 * y_ref[slot, 1]


def _combine(x, slab, dest, ys):
    n = x.shape[0]
    steps = n // ROUTE_ROWS
    row = pl.BlockSpec((ROUTE_ROWS, D_MODEL), lambda i: (i, 0))
    dest3 = dest.reshape(steps, 1, 2 * ROUTE_ROWS)
    dest_block = (1, 1, 2 * ROUTE_ROWS)
    return pl.pallas_call(
        _combine_kernel,
        grid=(steps,),
        in_specs=[pl.BlockSpec(dest_block, lambda i: (i, 0, 0), memory_space=pltpu.SMEM),
                  pl.BlockSpec(dest_block, lambda i: (jnp.minimum(i + 1, steps - 1), 0, 0),
                               memory_space=pltpu.SMEM),
                  row, pl.BlockSpec((ROUTE_ROWS, LANES), lambda i: (i, 0)),
                  pl.BlockSpec(memory_space=pl.ANY)],
        out_specs=row,
        out_shape=jax.ShapeDtypeStruct((n, D_MODEL), F32),
        scratch_shapes=[pltpu.VMEM((2, 2, ROUTE_ROWS, D_MODEL), F32), pltpu.SemaphoreType.DMA((2,))],
        input_output_aliases={2: 0},
        compiler_params=_cparams(1),
        name="moe_combine",
    )(dest3, dest3, x, slab, ys)


def _routing_plan(slab, n_rows):
    n = slab.shape[0]
    idx = slab[:, ROUTE_IDX_LANE:ROUTE_IDX_LANE + 2].astype(jnp.int32)
    onehot = (idx[:, :, None] == jnp.arange(N_EXPERTS, dtype=jnp.int32)).astype(jnp.int32).sum(axis=1)
    csum = jnp.cumsum(onehot, axis=0)
    rank = csum - onehot
    counts = csum[-1]
    padded = ((counts + MOE_TILE - 1) // MOE_TILE) * MOE_TILE
    ends = jnp.cumsum(padded)
    offs = ends - padded
    dest = offs[idx] + jnp.take_along_axis(rank, idx, axis=1)
    tile_start = jnp.arange(n_rows // MOE_TILE, dtype=jnp.int32) * MOE_TILE
    tile_expert = jnp.minimum(jnp.sum(tile_start[:, None] >= ends[None, :], axis=1), N_EXPERTS - 1)
    n_active = (ends[-1] // MOE_TILE).reshape(1)
    return dest.reshape(-1).astype(jnp.int32), tile_expert.astype(jnp.int32), n_active.astype(jnp.int32)


def _moe(hf, x, slab, w1, w3, w2):
    n = x.shape[0]
    n_rows = 2 * n + N_EXPERTS * MOE_TILE
    dest, tile_expert, n_active = _routing_plan(slab, n_rows)
    xs = _dispatch(hf, dest, n_rows)
    ys = _experts(xs, tile_expert, n_active, w1, w3, w2)
    return _combine(x, slab, dest, ys)


def _prep_inproj(w_in_l, gate_b_l, qn_b, kn_b, qn_c, kn_c):
    a_in = 2 * CONV_CH
    b_q = WB_HEADS * HEAD_DIM
    b_kv = WB_KV_HEADS * HEAD_DIM
    c_w = NA_HEADS * HEAD_DIM
    o = 0
    wa = w_in_l[:, o:o + a_in]; o += a_in
    wbq = w_in_l[:, o:o + b_q]; o += b_q
    wbk = w_in_l[:, o:o + b_kv]; o += b_kv
    wbv = w_in_l[:, o:o + b_kv]; o += b_kv
    wcq = w_in_l[:, o:o + c_w]; o += c_w
    wck = w_in_l[:, o:o + c_w]; o += c_w
    wcv = w_in_l[:, o:o + c_w]; o += c_w
    wg = w_in_l[:, o:]

    def dup(w):
        return jnp.repeat(w.reshape(D_MODEL, WB_KV_HEADS, 1, HEAD_DIM), 2, axis=2).reshape(D_MODEL, -1)

    w = jnp.concatenate([wg, wa, wbq, dup(wbk), dup(wbv), wcq, wck, wcv], axis=1).astype(BF16)
    scale = HEAD_DIM ** -0.5
    ones = lambda k: jnp.ones((k,), F32)
    cg = jnp.concatenate([
        ones(ZBQ), jnp.tile(qn_b.astype(F32), WB_HEADS) * scale, jnp.tile(kn_b.astype(F32), 2 * WB_KV_HEADS),
        ones(ZCQ - ZBV), jnp.tile(qn_c.astype(F32), NA_HEADS) * scale, jnp.tile(kn_c.astype(F32), NA_HEADS),
        ones(ZC - ZCV)])
    return w, cg.reshape(1, ZC), gate_b_l.astype(F32).reshape(1, -1)


def _segment_mean_matrix():
    idx = np.arange(ZCHUNK) // HEAD_DIM
    return jnp.asarray((idx[:, None] == idx[None, :]).astype(np.float32) / HEAD_DIM, dtype=BF16)


def kernel(x, t5_rel_bias, attn_norm_g, w_in, gate_b, conv_dw_w, conv_dw_b, conv_ln_g, conv_ln_b,
           conv_w_out, wb_qn_g, wb_kn_g, wb_sink, wb_w_out, na_qn_g, na_kn_g, na_rpb, na_w_out, w_o,
           ffn_norm_g, ffn_w1, ffn_w3, ffn_w2, moe_w_router, moe_b_router, moe_w1, moe_w3, moe_w2):
    bsz, s, d = x.shape
    depth = w_in.shape[0]
    assert d == D_MODEL and s % BLOCK == 0 and s % (NA_QROWS * GRID_W) == 0
    n = bsz * s
    xf = x.reshape(n, d).astype(F32)
    row = lambda v: v.astype(F32).reshape(1, -1)

    t5_bias = _t5_bias(t5_rel_bias)
    na_bias = _na_bias(na_rpb, s // GRID_W)
    bd = _segment_mean_matrix()

    for layer in range(depth):
        w, cg, gb = _prep_inproj(w_in[layer], gate_b[layer], wb_qn_g[layer], wb_kn_g[layer],
                                 na_qn_g[layer], na_kn_g[layer])
        z = _inproj(xf, row(attn_norm_g[layer]), w, cg, gb, bd)
        u = _conv_mixer(z, bsz, s, conv_dw_w[layer].astype(F32), row(conv_dw_b[layer]),
                        row(conv_ln_g[layer]), row(conv_ln_b[layer]))
        ob = _window_attention(z, bsz, s, row(wb_sink[layer]), t5_bias)
        oc = _neighbourhood_attention(z, bsz, s, na_bias, layer)
        weights = (conv_w_out[layer].astype(BF16), wb_w_out[layer].astype(BF16),
                   na_w_out[layer].astype(BF16), w_o[layer].astype(BF16), row(ffn_norm_g[layer]))
        i = layer // 2
        if layer % 2 == 0:
            xf, hf = _merge(u, ob, oc, z, xf, *weights)
            xf = _dense_ffn(hf, xf, ffn_w1[i].astype(BF16), ffn_w3[i].astype(BF16), ffn_w2[i].astype(BF16))
        else:
            wr = jnp.zeros((D_MODEL, LANES), F32).at[:, :N_EXPERTS].set(moe_w_router[i].astype(F32))
            br = jnp.full((1, LANES), -jnp.inf, F32).at[0, :N_EXPERTS].set(moe_b_router[i].astype(F32))
            wr_hi = wr.astype(BF16)
            wr_lo = (wr - wr_hi.astype(F32)).astype(BF16)
            xf, hf, slab = _merge(u, ob, oc, z, xf, *weights, router=(wr_hi, wr_lo, br))
            xf = _moe(hf, xf, slab, moe_w1[i].astype(BF16), moe_w3[i].astype(BF16), moe_w2[i].astype(BF16))
    return xf.reshape(bsz, s, d).astype(x.dtype)
```

```python
import functools
import math

import numpy as np
import jax
import jax.numpy as jnp
from jax import lax
from jax.experimental import pallas as pl
from jax.experimental.pallas import tpu as pltpu

D_MODEL = 1024
HEAD_DIM = 64
CONV_CH = 512
CONV_WIDTH = 31
WB_HEADS = 8
WB_KV_HEADS = 2
WINDOW = 128
BLOCK = 128
NA_HEADS = 8
GRID_W = 64
NA_KH = 8
NA_KW = 16
REL_BUCKETS = 32
REL_MAX_DIST = 128
N_EXPERTS = 8
EPS = 1e-6
NEG_INF = -1e30

LANES = 128

ZG = 0
ZA = 3072
ZBQ = 4096
ZBK = 4608
ZBV = 4864
ZCQ = 5120
ZCK = 5632
ZCV = 6144
ZC = 6656
ZCHUNK = 512

VMEM_LIMIT = 56 * 1024 * 1024

F32 = jnp.float32
BF16 = jnp.bfloat16


def _cparams(grid_rank, vmem=VMEM_LIMIT):
    return pltpu.CompilerParams(dimension_semantics=("arbitrary",) * grid_rank, vmem_limit_bytes=vmem)


def _resident(shape):
    nd = len(shape)
    return pl.BlockSpec(shape, lambda *_: (0,) * nd, pipeline_mode=pl.Buffered(1))


def _dot(a, b):
    return jnp.dot(a, b, preferred_element_type=F32)


def _dot_nt(a, b):
    return lax.dot_general(a, b, (((1,), (1,)), ((), ())), preferred_element_type=F32)


def _t5_bucket_np(rel):
    half = REL_BUCKETS // 2
    max_exact = half // 2
    ret = np.where(rel > 0, half, 0)
    n = np.abs(rel)
    nf = np.maximum(n, 1).astype(np.float32)
    large = max_exact + (np.log(nf / np.float32(max_exact)) / np.float32(math.log(REL_MAX_DIST / max_exact))
                         * np.float32(half - max_exact)).astype(np.int32)
    large = np.minimum(large, half - 1)
    return ret + np.where(n < max_exact, n, large)


def _window_bucket_table():
    col = np.arange(3 * BLOCK)[None, :]
    rel = (col - BLOCK) - np.arange(BLOCK)[:, None]
    band = np.where(np.abs(rel) <= WINDOW, _t5_bucket_np(rel), -1)
    first = np.where(col >= BLOCK, band, -1)
    last = np.where(col < 2 * BLOCK, band, -1)
    return np.stack([first, band, last]).astype(np.int32)


NA_QROWS = 2
NA_KROWS = 10


def _na_tables(rows):
    kh = min(NA_KH, rows)
    col = np.arange(GRID_W)
    col_start = np.clip(col - NA_KW // 2, 0, GRID_W - NA_KW)
    kc = np.arange(GRID_W)[None, :]
    inwin = (kc >= col_start[:, None]) & (kc < col_start[:, None] + NA_KW)
    dmat = np.where(inwin, kc - col[:, None] + NA_KW - 1, -1).astype(np.int32)
    variants, variant_of_pair, kstart = [], [], []
    for i in range(rows // NA_QROWS):
        ks = int(np.clip(NA_QROWS * i - kh // 2, 0, rows - NA_KROWS))
        tab = np.full((NA_QROWS, NA_KROWS), -1, np.int64)
        for a in range(NA_QROWS):
            r = NA_QROWS * i + a
            r0 = int(np.clip(r - kh // 2, 0, rows - kh))
            assert ks <= r0 and r0 + kh <= ks + NA_KROWS
            for kr in range(NA_KROWS):
                if r0 <= ks + kr < r0 + kh:
                    tab[a, kr] = ks + kr - r + NA_KH - 1
        for v, t in enumerate(variants):
            if np.array_equal(t, tab):
                variant_of_pair.append(v)
                break
        else:
            variant_of_pair.append(len(variants))
            variants.append(tab)
        kstart.append(ks)
    return dmat, variants, variant_of_pair, kstart


def _t5_bias_kernel(tab_ref, bucket_ref, o_ref):
    h = pl.program_id(0)
    for v in range(bucket_ref.shape[0]):
        bucket = bucket_ref[v]
        acc = jnp.full(bucket.shape, NEG_INF, F32)
        for b in range(REL_BUCKETS):
            acc = jnp.where(bucket == b, tab_ref[b, h], acc)
        o_ref[0, v] = acc


def _t5_bias(t5_rel_bias):
    bucket = jnp.asarray(_window_bucket_table())
    nv = bucket.shape[0]
    return pl.pallas_call(
        _t5_bias_kernel,
        grid=(WB_HEADS,),
        in_specs=[pl.BlockSpec(memory_space=pltpu.SMEM),
                  pl.BlockSpec((nv, BLOCK, 3 * BLOCK), lambda h: (0, 0, 0))],
        out_specs=pl.BlockSpec((1, nv, BLOCK, 3 * BLOCK), lambda h: (h, 0, 0, 0)),
        out_shape=jax.ShapeDtypeStruct((WB_HEADS, nv, BLOCK, 3 * BLOCK), F32),
        compiler_params=_cparams(1),
        name="t5_bias",
    )(t5_rel_bias.astype(F32), bucket)


def _na_bias_kernel(variants, rpb_ref, dmat_ref, o_ref):
    lh = pl.program_id(0) * NA_HEADS + pl.program_id(1)
    dmat = dmat_ref[...]
    ncol = 2 * NA_KW - 1
    base = []
    for dr in range(2 * NA_KH - 1):
        acc = jnp.full(dmat.shape, NEG_INF, F32)
        for d in range(ncol):
            acc = jnp.where(dmat == d, rpb_ref[lh, dr * ncol + d], acc)
        base.append(acc)
    neg = jnp.full(dmat.shape, NEG_INF, F32)
    for v, tab in enumerate(variants):
        for a in range(NA_QROWS):
            for kr in range(NA_KROWS):
                dr = int(tab[a, kr])
                tile = neg if dr < 0 else base[dr]
                o_ref[0, v, 0, a * GRID_W:(a + 1) * GRID_W, kr * GRID_W:(kr + 1) * GRID_W] = tile


def _na_bias(na_rpb, rows):
    depth = na_rpb.shape[0]
    dmat, variants, _, _ = _na_tables(rows)
    nv = len(variants)
    rpb = jnp.transpose(na_rpb.astype(F32), (0, 3, 1, 2)).reshape(depth * NA_HEADS, -1)
    qn, kn = NA_QROWS * GRID_W, NA_KROWS * GRID_W
    return pl.pallas_call(
        functools.partial(_na_bias_kernel, variants),
        grid=(depth, NA_HEADS),
        in_specs=[pl.BlockSpec(memory_space=pltpu.SMEM),
                  pl.BlockSpec((GRID_W, GRID_W), lambda l, h: (0, 0))],
        out_specs=pl.BlockSpec((1, nv, 1, qn, kn), lambda l, h: (l, 0, h, 0, 0)),
        out_shape=jax.ShapeDtypeStruct((depth, nv, NA_HEADS, qn, kn), F32),
        compiler_params=_cparams(2),
        name="na_bias",
    )(rpb, jnp.asarray(dmat))


def _inproj_kernel(x_ref, ng_ref, w_ref, cg_ref, gb_ref, bd_ref, z_ref, h_ref):
    x = x_ref[...]
    inv = lax.rsqrt(jnp.mean(x * x, axis=-1, keepdims=True) + EPS)
    h_ref[...] = (x * inv * ng_ref[...]).astype(BF16)

    def qknorm(zc, c0):
        w = bd_ref.shape[0]
        sq = (zc * zc).astype(BF16)
        ms = jnp.concatenate([_dot(sq[:, j:j + w], bd_ref[...]) for j in range(0, zc.shape[1], w)], axis=1)
        return zc * lax.rsqrt(ms + EPS) * cg_ref[:, c0:c0 + zc.shape[1]]

    for c0 in range(0, ZC, ZCHUNK):
        zc = _dot(h_ref[...], w_ref[:, c0:c0 + ZCHUNK])
        if c0 < ZA:
            out = jax.nn.sigmoid(zc + gb_ref[:, c0:c0 + ZCHUNK])
        elif c0 in (ZBQ, ZCQ, ZCK):
            out = qknorm(zc, c0)
        elif c0 == ZBK:
            half = ZBV - ZBK
            out = jnp.concatenate([qknorm(zc[:, :half], c0), zc[:, half:]], axis=1)
        else:
            out = zc
        z_ref[:, c0:c0 + ZCHUNK] = out.astype(BF16)


def _inproj(x, ng, w, cg, gb, bd, tm=512):
    n = x.shape[0]
    return pl.pallas_call(
        _inproj_kernel,
        grid=(n // tm,),
        in_specs=[pl.BlockSpec((tm, D_MODEL), lambda i: (i, 0)),
                  _resident((1, D_MODEL)), _resident((D_MODEL, ZC)), _resident((1, ZC)),
                  _resident((1, ZA)), _resident(bd.shape)],
        out_specs=pl.BlockSpec((tm, ZC), lambda i: (i, 0)),
        out_shape=jax.ShapeDtypeStruct((n, ZC), BF16),
        scratch_shapes=[pltpu.VMEM((tm, D_MODEL), BF16)],
        compiler_params=_cparams(1),
        name="inproj",
    )(x, ng, w, cg, gb, bd)


SUBLANES = 8
CONV_PAD = 16
CONV_HALF = 256
CONV_ROWS = 64
CONV_SHIFT_ROWS = 104


def _conv_kernel(z_ref, dw_ref, db_ref, lg_ref, lb_ref, u_ref, vs_ref, c_ref):
    s = z_ref.shape[0]
    span = s + 2 * CONV_PAD
    first = CONV_PAD - CONV_WIDTH // 2
    glu_rows = 256

    for half in range(CONV_CH // CONV_HALF):
        l0 = half * CONV_HALF
        vs_ref[0, 0:CONV_PAD, :] = jnp.zeros((CONV_PAD, CONV_HALF), F32)
        vs_ref[0, CONV_PAD + s:span + SUBLANES, :] = jnp.zeros((CONV_PAD + SUBLANES, CONV_HALF), F32)

        def glu(i, carry):
            r = pl.multiple_of(i * glu_rows, glu_rows)
            a = z_ref[pl.ds(r, glu_rows), l0:l0 + CONV_HALF].astype(F32)
            g = z_ref[pl.ds(r, glu_rows), CONV_CH + l0:CONV_CH + l0 + CONV_HALF].astype(F32)
            vs_ref[0, pl.ds(CONV_PAD + r, glu_rows), :] = a * jax.nn.sigmoid(g)
            return carry

        lax.fori_loop(0, s // glu_rows, glu, 0)

        def shift(i, carry):
            r = pl.multiple_of(i * CONV_SHIFT_ROWS, SUBLANES)
            win = vs_ref[0, pl.ds(r, CONV_SHIFT_ROWS + SUBLANES), :]
            for k in range(1, SUBLANES):
                vs_ref[k, pl.ds(r, CONV_SHIFT_ROWS), :] = win[k:k + CONV_SHIFT_ROWS]
            return carry

        assert span % CONV_SHIFT_ROWS == 0
        lax.fori_loop(0, span // CONV_SHIFT_ROWS, shift, 0)

        def conv(i, carry):
            r = pl.multiple_of(i * CONV_ROWS, CONV_ROWS)
            acc = jnp.zeros((CONV_ROWS, CONV_HALF), F32) + db_ref[:, l0:l0 + CONV_HALF]
            for w in range(CONV_WIDTH):
                k, a = (first + w) % SUBLANES, (first + w) // SUBLANES
                tap = vs_ref[k, pl.ds(r + a * SUBLANES, CONV_ROWS), :]
                acc = acc + tap * dw_ref[w:w + 1, l0:l0 + CONV_HALF]
            c_ref[pl.ds(r, CONV_ROWS), l0:l0 + CONV_HALF] = acc
            return carry

        lax.fori_loop(0, s // CONV_ROWS, conv, 0)

    def norm(i, carry):
        r = pl.multiple_of(i * CONV_ROWS, CONV_ROWS)
        c = c_ref[pl.ds(r, CONV_ROWS), :]
        xc = c - jnp.mean(c, axis=-1, keepdims=True)
        var = jnp.mean(xc * xc, axis=-1, keepdims=True)
        y = xc * lax.rsqrt(var + EPS) * lg_ref[...] + lb_ref[...]
        u_ref[pl.ds(r, CONV_ROWS), :] = (y * jax.nn.sigmoid(y)).astype(BF16)
        return carry

    lax.fori_loop(0, s // CONV_ROWS, norm, 0)


def _conv_mixer(z, bsz, s, dw, db, lg, lb):
    return pl.pallas_call(
        _conv_kernel,
        grid=(bsz,),
        in_specs=[pl.BlockSpec((s, 2 * CONV_CH), lambda b: (b, ZA // (2 * CONV_CH))),
                  _resident((CONV_WIDTH, CONV_CH)), _resident((1, CONV_CH)),
                  _resident((1, CONV_CH)), _resident((1, CONV_CH))],
        out_specs=pl.BlockSpec((s, CONV_CH), lambda b: (b, 0)),
        out_shape=jax.ShapeDtypeStruct((bsz * s, CONV_CH), BF16),
        scratch_shapes=[pltpu.VMEM((SUBLANES, s + 2 * CONV_PAD + SUBLANES, CONV_HALF), F32),
                        pltpu.VMEM((s, CONV_CH), F32)],
        compiler_params=_cparams(1),
        name="conv_mixer",
    )(z, dw, db, lg, lb)


def _head_masks():
    lane = lax.broadcasted_iota(jnp.int32, (1, LANES), 1)
    return lane < HEAD_DIM


ATTN_UNROLL = 8


def _softmax_pv(sc, vw, sink=None):
    m = jnp.max(sc, axis=-1, keepdims=True)
    if sink is not None:
        m = jnp.maximum(m, sink)
    p = jnp.exp(sc - m)
    l = jnp.sum(p, axis=-1, keepdims=True)
    if sink is not None:
        l = l + jnp.exp(sink - m)
    return _dot(p.astype(BF16), vw) * (1.0 / l)


def _wattn_kernel(pairs_per_kv, sink_ref, q_ref, k_ref, v_ref, bias_ref, o_ref, kp_ref, vp_ref):
    hp = pl.program_id(1)
    s = q_ref.shape[0]
    nb = s // BLOCK
    lo = _head_masks()
    zero = jnp.zeros((), BF16)

    @pl.when(hp % pairs_per_kv == 0)
    def _():
        pad = jnp.zeros((BLOCK, LANES), BF16)
        for src, dst in ((k_ref, kp_ref), (v_ref, vp_ref)):
            dst[0:BLOCK, :] = pad
            dst[BLOCK:BLOCK + s, :] = src[...]
            dst[BLOCK + s:2 * BLOCK + s, :] = pad

    def group(g, carry):
        rows, scores = [], []
        for u in range(ATTN_UNROLL):
            n = g * ATTN_UNROLL + u
            r = pl.multiple_of(n * BLOCK, BLOCK)
            var = jnp.where(n == 0, 0, jnp.where(n == nb - 1, 2, 1))
            q = q_ref[pl.ds(r, BLOCK), :]
            kw = kp_ref[pl.ds(r, 3 * BLOCK), :]
            rows.append(r)
            for hh in range(2):
                qm = jnp.where(lo if hh == 0 else jnp.logical_not(lo), q, zero)
                scores.append(_dot_nt(qm, kw) + bias_ref[hh, var])
        for u, r in enumerate(rows):
            vw = vp_ref[pl.ds(r, 3 * BLOCK), :]
            outs = [_softmax_pv(scores[2 * u + hh], vw, sink_ref[0, 2 * hp + hh]) for hh in range(2)]
            o_ref[pl.ds(r, BLOCK), :] = jnp.where(lo, outs[0], outs[1]).astype(BF16)
        return carry

    lax.fori_loop(0, nb // ATTN_UNROLL, group, 0)


def _window_attention(z, bsz, s, sink, bias):
    npair = WB_HEADS // 2
    pairs_per_kv = npair // WB_KV_HEADS
    nv = bias.shape[1]
    assert (s // BLOCK) % ATTN_UNROLL == 0 and s // BLOCK >= 2
    return pl.pallas_call(
        functools.partial(_wattn_kernel, pairs_per_kv),
        grid=(bsz, npair),
        in_specs=[pl.BlockSpec(memory_space=pltpu.SMEM),
                  pl.BlockSpec((s, LANES), lambda b, hp: (b, ZBQ // LANES + hp)),
                  pl.BlockSpec((s, LANES), lambda b, hp: (b, ZBK // LANES + hp // pairs_per_kv)),
                  pl.BlockSpec((s, LANES), lambda b, hp: (b, ZBV // LANES + hp // pairs_per_kv)),
                  pl.BlockSpec((2, nv, BLOCK, 3 * BLOCK), lambda b, hp: (hp, 0, 0, 0))],
        out_specs=pl.BlockSpec((s, LANES), lambda b, hp: (b, hp)),
        out_shape=jax.ShapeDtypeStruct((bsz * s, WB_HEADS * HEAD_DIM), BF16),
        scratch_shapes=[pltpu.VMEM((s + 2 * BLOCK, LANES), BF16), pltpu.VMEM((s + 2 * BLOCK, LANES), BF16)],
        compiler_params=_cparams(2),
        name="window_attention",
    )(sink, z, z, z, bias)


def _nattn_kernel(n_edge, n_pairs, q_ref, k_ref, v_ref, bias_ref, o_ref):
    lo = _head_masks()
    zero = jnp.zeros((), BF16)
    qn, kn = NA_QROWS * GRID_W, NA_KROWS * GRID_W
    max_ks = k_ref.shape[0] // GRID_W - NA_KROWS

    def group(g, carry):
        rows, scores = [], []
        for u in range(ATTN_UNROLL):
            i = g * ATTN_UNROLL + u
            r = pl.multiple_of(i * qn, qn)
            ks = jnp.clip(NA_QROWS * i - NA_KH // 2, 0, max_ks)
            rk = pl.multiple_of(ks * GRID_W, LANES)
            var = jnp.where(i < n_edge, i,
                            jnp.where(i >= n_pairs - n_edge, i - (n_pairs - 2 * n_edge - 1), n_edge))
            q = q_ref[pl.ds(r, qn), :]
            kw = k_ref[pl.ds(rk, kn), :]
            rows.append((r, rk))
            for hh in range(2):
                qm = jnp.where(lo if hh == 0 else jnp.logical_not(lo), q, zero)
                scores.append(_dot_nt(qm, kw) + bias_ref[0, var, hh])
        for u, (r, rk) in enumerate(rows):
            vw = v_ref[pl.ds(rk, kn), :]
            outs = [_softmax_pv(scores[2 * u + hh], vw) for hh in range(2)]
            o_ref[pl.ds(r, qn), :] = jnp.where(lo, outs[0], outs[1]).astype(BF16)
        return carry

    lax.fori_loop(0, n_pairs // ATTN_UNROLL, group, 0)


def _neighbourhood_attention(z, bsz, s, bias, layer):
    rows = s // GRID_W
    _, variants, variant_of_pair, kstart = _na_tables(rows)
    n_pairs = rows // NA_QROWS
    nv = len(variants)
    n_edge = (nv - 1) // 2
    assert n_pairs % ATTN_UNROLL == 0
    for i in range(n_pairs):
        want = i if i < n_edge else (i - (n_pairs - 2 * n_edge - 1) if i >= n_pairs - n_edge else n_edge)
        assert variant_of_pair[i] == want
        assert kstart[i] == int(np.clip(NA_QROWS * i - NA_KH // 2, 0, rows - NA_KROWS))
        assert (kstart[i] * GRID_W) % LANES == 0
    npair = NA_HEADS // 2
    qn, kn = NA_QROWS * GRID_W, NA_KROWS * GRID_W
    return pl.pallas_call(
        functools.partial(_nattn_kernel, n_edge, n_pairs),
        grid=(npair, bsz),
        in_specs=[pl.BlockSpec((s, LANES), lambda hp, b: (b, ZCQ // LANES + hp)),
                  pl.BlockSpec((s, LANES), lambda hp, b: (b, ZCK // LANES + hp)),
                  pl.BlockSpec((s, LANES), lambda hp, b: (b, ZCV // LANES + hp)),
                  pl.BlockSpec((1, nv, 2, qn, kn), lambda hp, b: (layer, 0, hp, 0, 0))],
        out_specs=pl.BlockSpec((s, LANES), lambda hp, b: (b, hp)),
        out_shape=jax.ShapeDtypeStruct((bsz * s, NA_HEADS * HEAD_DIM), BF16),
        compiler_params=_cparams(2),
        name="neighbourhood_attention",
    )(z, z, z, bias)


ROUTE_IDX_LANE = N_EXPERTS
ROUTE_W_LANE = N_EXPERTS + 2


def _merge_kernel(with_router, u_ref, ob_ref, oc_ref, g_ref, x_ref, wa_ref, wb_ref, wc_ref, wo_ref,
                  fg_ref, *rest):
    if with_router:
        wrh_ref, wrl_ref, br_ref, xo_ref, hf_ref, slab_ref = rest
    else:
        xo_ref, hf_ref = rest
    d = D_MODEL
    m = g_ref[:, 0:d].astype(F32) * _dot(u_ref[...], wa_ref[...])
    m = m + g_ref[:, d:2 * d].astype(F32) * _dot(ob_ref[...], wb_ref[...])
    m = m + g_ref[:, 2 * d:3 * d].astype(F32) * _dot(oc_ref[...], wc_ref[...])
    xn = x_ref[...] + _dot(m.astype(BF16), wo_ref[...])
    xo_ref[...] = xn
    hf = xn * lax.rsqrt(jnp.mean(xn * xn, axis=-1, keepdims=True) + EPS) * fg_ref[...]
    hf_ref[...] = hf.astype(hf_ref.dtype)
    if with_router:
        hf_hi = hf.astype(BF16)
        hf_lo = (hf - hf_hi.astype(F32)).astype(BF16)
        logits = (_dot(hf_hi, wrh_ref[...]) + _dot(hf_lo, wrh_ref[...]) + _dot(hf_hi, wrl_ref[...])
                  + br_ref[...])
        lane = lax.broadcasted_iota(jnp.int32, logits.shape, 1)
        m1 = jnp.max(logits, axis=-1, keepdims=True)
        i1 = jnp.min(jnp.where(logits == m1, lane, LANES), axis=-1, keepdims=True)
        rest_l = jnp.where(lane == i1, -jnp.inf, logits)
        m2 = jnp.max(rest_l, axis=-1, keepdims=True)
        i2 = jnp.min(jnp.where(rest_l == m2, lane, LANES), axis=-1, keepdims=True)
        e2 = jnp.exp(m2 - m1)
        w1 = 1.0 / (1.0 + e2)
        w2 = e2 / (1.0 + e2)
        slab = jnp.where(lane == ROUTE_IDX_LANE, i1.astype(F32), 0.0)
        slab = jnp.where(lane == ROUTE_IDX_LANE + 1, i2.astype(F32), slab)
        slab = jnp.where(lane == ROUTE_W_LANE, w1, slab)
        slab = jnp.where(lane == ROUTE_W_LANE + 1, w2, slab)
        slab_ref[...] = slab


def _merge(u, ob, oc, z, x, wa, wb, wc, wo, fg, router=None, tm=512):
    n = x.shape[0]
    with_router = router is not None
    row = lambda w: pl.BlockSpec((tm, w), lambda i: (i, 0))
    in_specs = [row(CONV_CH), row(WB_HEADS * HEAD_DIM), row(NA_HEADS * HEAD_DIM),
                pl.BlockSpec((tm, ZA), lambda i: (i, 0)), row(D_MODEL),
                _resident(wa.shape), _resident(wb.shape), _resident(wc.shape), _resident(wo.shape),
                _resident((1, D_MODEL))]
    args = [u, ob, oc, z, x, wa, wb, wc, wo, fg]
    out_specs = [row(D_MODEL), row(D_MODEL)]
    out_shape = [jax.ShapeDtypeStruct((n, D_MODEL), F32),
                 jax.ShapeDtypeStruct((n, D_MODEL), F32 if with_router else BF16)]
    if with_router:
        in_specs += [_resident((D_MODEL, LANES)), _resident((D_MODEL, LANES)), _resident((1, LANES))]
        args += list(router)
        out_specs.append(row(LANES))
        out_shape.append(jax.ShapeDtypeStruct((n, LANES), F32))
    return pl.pallas_call(
        functools.partial(_merge_kernel, with_router),
        grid=(n // tm,),
        in_specs=in_specs, out_specs=out_specs, out_shape=out_shape,
        input_output_aliases={4: 0},
        compiler_params=_cparams(1),
        name="merge_router" if with_router else "merge",
    )(*args)


FFN_CHUNK = 1024


def _ffn_kernel(hf_ref, x_ref, w1_ref, w3_ref, w2_ref, o_ref):
    dff = w1_ref.shape[1]
    hf = hf_ref[...]
    acc = x_ref[...]
    for c0 in range(0, dff, FFN_CHUNK):
        c1 = min(c0 + FFN_CHUNK, dff)
        a = _dot(hf, w1_ref[:, c0:c1])
        b = _dot(hf, w3_ref[:, c0:c1])
        hm = (a * jax.nn.sigmoid(a) * b).astype(BF16)
        acc = acc + _dot(hm, w2_ref[c0:c1, :])
    o_ref[...] = acc


def _dense_ffn(hf, x, w1, w3, w2, tm=512):
    n = x.shape[0]
    row = pl.BlockSpec((tm, D_MODEL), lambda i: (i, 0))
    return pl.pallas_call(
        _ffn_kernel,
        grid=(n // tm,),
        in_specs=[row, row, _resident(w1.shape), _resident(w3.shape), _resident(w2.shape)],
        out_specs=row,
        out_shape=jax.ShapeDtypeStruct((n, D_MODEL), F32),
        input_output_aliases={1: 0},
        compiler_params=_cparams(1),
        name="dense_ffn",
    )(hf, x, w1, w3, w2)


MOE_TILE = 1024
MOE_FF_CHUNK = 512
ROUTE_ROWS = 256


def _row_copy(src, dst, sem):
    return pltpu.make_async_copy(src, dst, sem)


def _dispatch_kernel(dest_ref, hf_ref, init_ref, xs_ref, buf_ref, load_sem, row_sem):
    del init_ref
    i = pl.program_id(0)
    last = pl.num_programs(0) - 1
    slot = i % 2

    def load(step, slot_):
        src = hf_ref.at[pl.ds(pl.multiple_of(step * ROUTE_ROWS, ROUTE_ROWS), ROUTE_ROWS)]
        return pltpu.make_async_copy(src, buf_ref.at[slot_], load_sem.at[slot_])

    def drain(slot_):
        for _ in range(2 * ROUTE_ROWS):
            _row_copy(buf_ref.at[slot_, pl.ds(0, 1)], xs_ref.at[pl.ds(0, 1)], row_sem.at[slot_]).wait()

    @pl.when(i == 0)
    def _():
        load(0, 0).start()

    load(i, slot).wait()

    @pl.when(i > 0)
    def _():
        drain(1 - slot)

    @pl.when(i < last)
    def _():
        load(i + 1, 1 - slot).start()

    def start(r, carry):
        for k in range(2):
            d = dest_ref[0, 0, 2 * r + k]
            _row_copy(buf_ref.at[slot, pl.ds(r, 1)], xs_ref.at[pl.ds(d, 1)], row_sem.at[slot]).start(priority=k)
        return carry

    lax.fori_loop(0, ROUTE_ROWS, start, 0, unroll=8)

    @pl.when(i == last)
    def _():
        drain(slot)


def _dispatch(hf, dest, n_rows):
    n = hf.shape[0]
    steps = n // ROUTE_ROWS
    init = jnp.zeros((n_rows, D_MODEL), F32)
    return pl.pallas_call(
        _dispatch_kernel,
        grid=(steps,),
        in_specs=[pl.BlockSpec((1, 1, 2 * ROUTE_ROWS), lambda i: (i, 0, 0), memory_space=pltpu.SMEM),
                  pl.BlockSpec(memory_space=pl.ANY),
                  pl.BlockSpec(memory_space=pl.ANY)],
        out_specs=pl.BlockSpec(memory_space=pl.ANY),
        out_shape=jax.ShapeDtypeStruct((n_rows, D_MODEL), F32),
        scratch_shapes=[pltpu.VMEM((2, ROUTE_ROWS, D_MODEL), F32), pltpu.SemaphoreType.DMA((2,)),
                        pltpu.SemaphoreType.DMA((2,))],
        input_output_aliases={2: 0},
        compiler_params=_cparams(1),
        name="moe_dispatch",
    )(dest.reshape(steps, 1, 2 * ROUTE_ROWS), hf, init)


def _experts_kernel(te_ref, na_ref, xs_ref, w1_ref, w3_ref, w2_ref, ys_ref, xb_ref):
    i = pl.program_id(0)
    c = pl.program_id(1)
    active = i < na_ref[0]

    @pl.when(c == 0)
    def _():
        ys_ref[...] = jnp.zeros_like(ys_ref)

    @pl.when(jnp.logical_and(active, c == 0))
    def _():
        xb_ref[...] = xs_ref[...].astype(BF16)

    @pl.when(active)
    def _():
        half = MOE_TILE // 2
        gates = []
        for h in range(2):
            xb = xb_ref[h * half:(h + 1) * half, :]
            gates.append((_dot(xb, w1_ref[0]), _dot(xb, w3_ref[0])))
        for h, (a, b) in enumerate(gates):
            hm = (a * jax.nn.sigmoid(a) * b).astype(BF16)
            ys_ref[h * half:(h + 1) * half, :] += _dot(hm, w2_ref[0])


def _experts(xs, tile_expert, n_active, w1, w3, w2):
    n_rows = xs.shape[0]
    dff = w1.shape[2]
    grid_spec = pltpu.PrefetchScalarGridSpec(
        num_scalar_prefetch=2,
        grid=(n_rows // MOE_TILE, dff // MOE_FF_CHUNK),
        in_specs=[pl.BlockSpec((MOE_TILE, D_MODEL), lambda i, c, te, na: (i, 0)),
                  pl.BlockSpec((1, D_MODEL, MOE_FF_CHUNK), lambda i, c, te, na: (te[i], 0, c)),
                  pl.BlockSpec((1, D_MODEL, MOE_FF_CHUNK), lambda i, c, te, na: (te[i], 0, c)),
                  pl.BlockSpec((1, MOE_FF_CHUNK, D_MODEL), lambda i, c, te, na: (te[i], c, 0))],
        out_specs=pl.BlockSpec((MOE_TILE, D_MODEL), lambda i, c, te, na: (i, 0)),
        scratch_shapes=[pltpu.VMEM((MOE_TILE, D_MODEL), BF16)],
    )
    return pl.pallas_call(
        _experts_kernel,
        grid_spec=grid_spec,
        out_shape=jax.ShapeDtypeStruct((n_rows, D_MODEL), F32),
        compiler_params=_cparams(2),
        name="moe_experts",
    )(tile_expert, n_active, xs, w1, w3, w2)


def _combine_kernel(dest_ref, next_dest_ref, x_ref, slab_ref, ys_ref, o_ref, y_ref, sem):
    i = pl.program_id(0)
    slot = i % 2

    def gather(d_ref, slot_):
        def start(r, carry):
            for k in range(2):
                d = d_ref[0, 0, 2 * r + k]
                _row_copy(ys_ref.at[pl.ds(d, 1)], y_ref.at[slot_, k, pl.ds(r, 1)], sem.at[slot_]).start()
            return carry

        lax.fori_loop(0, ROUTE_ROWS, start, 0, unroll=8)

    @pl.when(i == 0)
    def _():
        gather(dest_ref, 0)

    @pl.when(i + 1 < pl.num_programs(0))
    def _():
        gather(next_dest_ref, 1 - slot)

    for _ in range(2 * ROUTE_ROWS):
        _row_copy(ys_ref.at[pl.ds(0, 1)], y_ref.at[slot, 0, pl.ds(0, 1)], sem.at[slot]).wait()
    w0 = slab_ref[:, ROUTE_W_LANE:ROUTE_W_LANE + 1]
    w1 = slab_ref[:, ROUTE_W_LANE + 1:ROUTE_W_LANE + 2]
    o_ref[...] = x_ref[...] + w0 * y_ref[slot, 0] + w1 * y_ref[slot, 1]


def _combine(x, slab, dest, ys):
    n = x.shape[0]
    steps = n // ROUTE_ROWS
    row = pl.BlockSpec((ROUTE_ROWS, D_MODEL), lambda i: (i, 0))
    dest3 = dest.reshape(steps, 1, 2 * ROUTE_ROWS)
    dest_block = (1, 1, 2 * ROUTE_ROWS)
    return pl.pallas_call(
        _combine_kernel,
        grid=(steps,),
        in_specs=[pl.BlockSpec(dest_block, lambda i: (i, 0, 0), memory_space=pltpu.SMEM),
                  pl.BlockSpec(dest_block, lambda i: (jnp.minimum(i + 1, steps - 1), 0, 0),
                               memory_space=pltpu.SMEM),
                  row, pl.BlockSpec((ROUTE_ROWS, LANES), lambda i: (i, 0)),
                  pl.BlockSpec(memory_space=pl.ANY)],
        out_specs=row,
        out_shape=jax.ShapeDtypeStruct((n, D_MODEL), F32),
        scratch_shapes=[pltpu.VMEM((2, 2, ROUTE_ROWS, D_MODEL), F32), pltpu.SemaphoreType.DMA((2,))],
        input_output_aliases={2: 0},
        compiler_params=_cparams(1),
        name="moe_combine",
    )(dest3, dest3, x, slab, ys)


def _routing_plan(slab, n_rows):
    n = slab.shape[0]
    idx = slab[:, ROUTE_IDX_LANE:ROUTE_IDX_LANE + 2].astype(jnp.int32)
    onehot = (idx[:, :, None] == jnp.arange(N_EXPERTS, dtype=jnp.int32)).astype(jnp.int32).sum(axis=1)
    csum = jnp.cumsum(onehot, axis=0)
    rank = csum - onehot
    counts = csum[-1]
    padded = ((counts + MOE_TILE - 1) // MOE_TILE) * MOE_TILE
    ends = jnp.cumsum(padded)
    offs = ends - padded
    dest = offs[idx] + jnp.take_along_axis(rank, idx, axis=1)
    tile_start = jnp.arange(n_rows // MOE_TILE, dtype=jnp.int32) * MOE_TILE
    tile_expert = jnp.minimum(jnp.sum(tile_start[:, None] >= ends[None, :], axis=1), N_EXPERTS - 1)
    n_active = (ends[-1] // MOE_TILE).reshape(1)
    return dest.reshape(-1).astype(jnp.int32), tile_expert.astype(jnp.int32), n_active.astype(jnp.int32)


def _moe(hf, x, slab, w1, w3, w2):
    n = x.shape[0]
    n_rows = 2 * n + N_EXPERTS * MOE_TILE
    dest, tile_expert, n_active = _routing_plan(slab, n_rows)
    xs = _dispatch(hf, dest, n_rows)
    ys = _experts(xs, tile_expert, n_active, w1, w3, w2)
    return _combine(x, slab, dest, ys)


def _prep_inproj(w_in_l, gate_b_l, qn_b, kn_b, qn_c, kn_c):
    a_in = 2 * CONV_CH
    b_q = WB_HEADS * HEAD_DIM
    b_kv = WB_KV_HEADS * HEAD_DIM
    c_w = NA_HEADS * HEAD_DIM
    o = 0
    wa = w_in_l[:, o:o + a_in]; o += a_in
    wbq = w_in_l[:, o:o + b_q]; o += b_q
    wbk = w_in_l[:, o:o + b_kv]; o += b_kv
    wbv = w_in_l[:, o:o + b_kv]; o += b_kv
    wcq = w_in_l[:, o:o + c_w]; o += c_w
    wck = w_in_l[:, o:o + c_w]; o += c_w
    wcv = w_in_l[:, o:o + c_w]; o += c_w
    wg = w_in_l[:, o:]

    def dup(w):
        return jnp.repeat(w.reshape(D_MODEL, WB_KV_HEADS, 1, HEAD_DIM), 2, axis=2).reshape(D_MODEL, -1)

    w = jnp.concatenate([wg, wa, wbq, dup(wbk), dup(wbv), wcq, wck, wcv], axis=1).astype(BF16)
    scale = HEAD_DIM ** -0.5
    ones = lambda k: jnp.ones((k,), F32)
    cg = jnp.concatenate([
        ones(ZBQ), jnp.tile(qn_b.astype(F32), WB_HEADS) * scale, jnp.tile(kn_b.astype(F32), 2 * WB_KV_HEADS),
        ones(ZCQ - ZBV), jnp.tile(qn_c.astype(F32), NA_HEADS) * scale, jnp.tile(kn_c.astype(F32), NA_HEADS),
        ones(ZC - ZCV)])
    return w, cg.reshape(1, ZC), gate_b_l.astype(F32).reshape(1, -1)


MXU_TILE = 256


def _segment_mean_matrix():
    idx = np.arange(MXU_TILE) // HEAD_DIM
    return jnp.asarray((idx[:, None] == idx[None, :]).astype(np.float32) / HEAD_DIM, dtype=BF16)


def kernel(x, t5_rel_bias, attn_norm_g, w_in, gate_b, conv_dw_w, conv_dw_b, conv_ln_g, conv_ln_b,
           conv_w_out, wb_qn_g, wb_kn_g, wb_sink, wb_w_out, na_qn_g, na_kn_g, na_rpb, na_w_out, w_o,
           ffn_norm_g, ffn_w1, ffn_w3, ffn_w2, moe_w_router, moe_b_router, moe_w1, moe_w3, moe_w2):
    bsz, s, d = x.shape
    depth = w_in.shape[0]
    assert d == D_MODEL and s % BLOCK == 0 and s % (NA_QROWS * GRID_W) == 0
    n = bsz * s
    xf = x.reshape(n, d).astype(F32)
    row = lambda v: v.astype(F32).reshape(1, -1)

    t5_bias = _t5_bias(t5_rel_bias)
    na_bias = _na_bias(na_rpb, s // GRID_W)
    bd = _segment_mean_matrix()

    for layer in range(depth):
        w, cg, gb = _prep_inproj(w_in[layer], gate_b[layer], wb_qn_g[layer], wb_kn_g[layer],
                                 na_qn_g[layer], na_kn_g[layer])
        z = _inproj(xf, row(attn_norm_g[layer]), w, cg, gb, bd)
        u = _conv_mixer(z, bsz, s, conv_dw_w[layer].astype(F32), row(conv_dw_b[layer]),
                        row(conv_ln_g[layer]), row(conv_ln_b[layer]))
        ob = _window_attention(z, bsz, s, row(wb_sink[layer]), t5_bias)
        oc = _neighbourhood_attention(z, bsz, s, na_bias, layer)
        weights = (conv_w_out[layer].astype(BF16), wb_w_out[layer].astype(BF16),
                   na_w_out[layer].astype(BF16), w_o[layer].astype(BF16), row(ffn_norm_g[layer]))
        i = layer // 2
        if layer % 2 == 0:
            xf, hf = _merge(u, ob, oc, z, xf, *weights)
            xf = _dense_ffn(hf, xf, ffn_w1[i].astype(BF16), ffn_w3[i].astype(BF16), ffn_w2[i].astype(BF16))
        else:
            wr = jnp.zeros((D_MODEL, LANES), F32).at[:, :N_EXPERTS].set(moe_w_router[i].astype(F32))
            br = jnp.full((1, LANES), -jnp.inf, F32).at[0, :N_EXPERTS].set(moe_b_router[i].astype(F32))
            wr_hi = wr.astype(BF16)
            wr_lo = (wr - wr_hi.astype(F32)).astype(BF16)
            xf, hf, slab = _merge(u, ob, oc, z, xf, *weights, router=(wr_hi, wr_lo, br))
            xf = _moe(hf, xf, slab, moe_w1[i].astype(BF16), moe_w3[i].astype(BF16), moe_w2[i].astype(BF16))
    return xf.reshape(bsz, s, d).astype(x.dtype)
```

```python
import functools
import math

import numpy as np
import jax
import jax.numpy as jnp
from jax import lax
from jax.experimental import pallas as pl
from jax.experimental.pallas import tpu as pltpu

D_MODEL = 1024
HEAD_DIM = 64
CONV_CH = 512
CONV_WIDTH = 31
WB_HEADS = 8
WB_KV_HEADS = 2
WINDOW = 128
BLOCK = 128
NA_HEADS = 8
GRID_W = 64
NA_KH = 8
NA_KW = 16
REL_BUCKETS = 32
REL_MAX_DIST = 128
N_EXPERTS = 8
EPS = 1e-6
NEG_INF = -1e30

LANES = 128

ZG = 0
ZA = 3072
ZBQ = 4096
ZBK = 4608
ZBV = 4864
ZCQ = 5120
ZCK = 5632
ZCV = 6144
ZC = 6656
ZCHUNK = 512

VMEM_LIMIT = 56 * 1024 * 1024

F32 = jnp.float32
BF16 = jnp.bfloat16


def _cparams(grid_rank, vmem=VMEM_LIMIT):
    return pltpu.CompilerParams(dimension_semantics=("arbitrary",) * grid_rank, vmem_limit_bytes=vmem)


def _resident(shape):
    nd = len(shape)
    return pl.BlockSpec(shape, lambda *_: (0,) * nd, pipeline_mode=pl.Buffered(1))


def _dot(a, b):
    return jnp.dot(a, b, preferred_element_type=F32)


def _dot_nt(a, b):
    return lax.dot_general(a, b, (((1,), (1,)), ((), ())), preferred_element_type=F32)


def _t5_bucket_np(rel):
    half = REL_BUCKETS // 2
    max_exact = half // 2
    ret = np.where(rel > 0, half, 0)
    n = np.abs(rel)
    nf = np.maximum(n, 1).astype(np.float32)
    large = max_exact + (np.log(nf / np.float32(max_exact)) / np.float32(math.log(REL_MAX_DIST / max_exact))
                         * np.float32(half - max_exact)).astype(np.int32)
    large = np.minimum(large, half - 1)
    return ret + np.where(n < max_exact, n, large)


def _window_bucket_table():
    col = np.arange(3 * BLOCK)[None, :]
    rel = (col - BLOCK) - np.arange(BLOCK)[:, None]
    band = np.where(np.abs(rel) <= WINDOW, _t5_bucket_np(rel), -1)
    first = np.where(col >= BLOCK, band, -1)
    last = np.where(col < 2 * BLOCK, band, -1)
    return np.stack([first, band, last]).astype(np.int32)


NA_QROWS = 2
NA_KROWS = 10


def _na_tables(rows):
    kh = min(NA_KH, rows)
    col = np.arange(GRID_W)
    col_start = np.clip(col - NA_KW // 2, 0, GRID_W - NA_KW)
    kc = np.arange(GRID_W)[None, :]
    inwin = (kc >= col_start[:, None]) & (kc < col_start[:, None] + NA_KW)
    dmat = np.where(inwin, kc - col[:, None] + NA_KW - 1, -1).astype(np.int32)
    variants, variant_of_pair, kstart = [], [], []
    for i in range(rows // NA_QROWS):
        ks = int(np.clip(NA_QROWS * i - kh // 2, 0, rows - NA_KROWS))
        tab = np.full((NA_QROWS, NA_KROWS), -1, np.int64)
        for a in range(NA_QROWS):
            r = NA_QROWS * i + a
            r0 = int(np.clip(r - kh // 2, 0, rows - kh))
            assert ks <= r0 and r0 + kh <= ks + NA_KROWS
            for kr in range(NA_KROWS):
                if r0 <= ks + kr < r0 + kh:
                    tab[a, kr] = ks + kr - r + NA_KH - 1
        for v, t in enumerate(variants):
            if np.array_equal(t, tab):
                variant_of_pair.append(v)
                break
        else:
            variant_of_pair.append(len(variants))
            variants.append(tab)
        kstart.append(ks)
    return dmat, variants, variant_of_pair, kstart


def _t5_bias_kernel(tab_ref, bucket_ref, o_ref):
    h = pl.program_id(0)
    for v in range(bucket_ref.shape[0]):
        bucket = bucket_ref[v]
        acc = jnp.full(bucket.shape, NEG_INF, F32)
        for b in range(REL_BUCKETS):
            acc = jnp.where(bucket == b, tab_ref[b, h], acc)
        o_ref[0, v] = acc


def _t5_bias(t5_rel_bias):
    bucket = jnp.asarray(_window_bucket_table())
    nv = bucket.shape[0]
    return pl.pallas_call(
        _t5_bias_kernel,
        grid=(WB_HEADS,),
        in_specs=[pl.BlockSpec(memory_space=pltpu.SMEM),
                  pl.BlockSpec((nv, BLOCK, 3 * BLOCK), lambda h: (0, 0, 0))],
        out_specs=pl.BlockSpec((1, nv, BLOCK, 3 * BLOCK), lambda h: (h, 0, 0, 0)),
        out_shape=jax.ShapeDtypeStruct((WB_HEADS, nv, BLOCK, 3 * BLOCK), F32),
        compiler_params=_cparams(1),
        name="t5_bias",
    )(t5_rel_bias.astype(F32), bucket)


def _na_bias_kernel(variants, rpb_ref, dmat_ref, o_ref):
    lh = pl.program_id(0) * NA_HEADS + pl.program_id(1)
    dmat = dmat_ref[...]
    ncol = 2 * NA_KW - 1
    base = []
    for dr in range(2 * NA_KH - 1):
        acc = jnp.full(dmat.shape, NEG_INF, F32)
        for d in range(ncol):
            acc = jnp.where(dmat == d, rpb_ref[lh, dr * ncol + d], acc)
        base.append(acc)
    neg = jnp.full(dmat.shape, NEG_INF, F32)
    for v, tab in enumerate(variants):
        for a in range(NA_QROWS):
            for kr in range(NA_KROWS):
                dr = int(tab[a, kr])
                tile = neg if dr < 0 else base[dr]
                o_ref[0, v, 0, a * GRID_W:(a + 1) * GRID_W, kr * GRID_W:(kr + 1) * GRID_W] = tile


def _na_bias(na_rpb, rows):
    depth = na_rpb.shape[0]
    dmat, variants, _, _ = _na_tables(rows)
    nv = len(variants)
    rpb = jnp.transpose(na_rpb.astype(F32), (0, 3, 1, 2)).reshape(depth * NA_HEADS, -1)
    qn, kn = NA_QROWS * GRID_W, NA_KROWS * GRID_W
    return pl.pallas_call(
        functools.partial(_na_bias_kernel, variants),
        grid=(depth, NA_HEADS),
        in_specs=[pl.BlockSpec(memory_space=pltpu.SMEM),
                  pl.BlockSpec((GRID_W, GRID_W), lambda l, h: (0, 0))],
        out_specs=pl.BlockSpec((1, nv, 1, qn, kn), lambda l, h: (l, 0, h, 0, 0)),
        out_shape=jax.ShapeDtypeStruct((depth, nv, NA_HEADS, qn, kn), F32),
        compiler_params=_cparams(2),
        name="na_bias",
    )(rpb, jnp.asarray(dmat))


def _inproj_kernel(x_ref, ng_ref, w_ref, cg_ref, gb_ref, bd_ref, z_ref, h_ref):
    x = x_ref[...]
    inv = lax.rsqrt(jnp.mean(x * x, axis=-1, keepdims=True) + EPS)
    h_ref[...] = (x * inv * ng_ref[...]).astype(BF16)

    def qknorm(zc, c0):
        w = bd_ref.shape[0]
        sq = (zc * zc).astype(BF16)
        ms = jnp.concatenate([_dot(sq[:, j:j + w], bd_ref[...]) for j in range(0, zc.shape[1], w)], axis=1)
        return zc * lax.rsqrt(ms + EPS) * cg_ref[:, c0:c0 + zc.shape[1]]

    for c0 in range(0, ZC, ZCHUNK):
        zc = _dot(h_ref[...], w_ref[:, c0:c0 + ZCHUNK])
        if c0 < ZA:
            out = jax.nn.sigmoid(zc + gb_ref[:, c0:c0 + ZCHUNK])
        elif c0 in (ZBQ, ZCQ, ZCK):
            out = qknorm(zc, c0)
        elif c0 == ZBK:
            half = ZBV - ZBK
            out = jnp.concatenate([qknorm(zc[:, :half], c0), zc[:, half:]], axis=1)
        else:
            out = zc
        z_ref[:, c0:c0 + ZCHUNK] = out.astype(BF16)


def _inproj(x, ng, w, cg, gb, bd, tm=512):
    n = x.shape[0]
    return pl.pallas_call(
        _inproj_kernel,
        grid=(n // tm,),
        in_specs=[pl.BlockSpec((tm, D_MODEL), lambda i: (i, 0)),
                  _resident((1, D_MODEL)), _resident((D_MODEL, ZC)), _resident((1, ZC)),
                  _resident((1, ZA)), _resident(bd.shape)],
        out_specs=pl.BlockSpec((tm, ZC), lambda i: (i, 0)),
        out_shape=jax.ShapeDtypeStruct((n, ZC), BF16),
        scratch_shapes=[pltpu.VMEM((tm, D_MODEL), BF16)],
        compiler_params=_cparams(1),
        name="inproj",
    )(x, ng, w, cg, gb, bd)


SUBLANES = 8
CONV_PAD = 16
CONV_HALF = 256
CONV_ROWS = 64
CONV_SHIFT_ROWS = 104
CONV_NORM_TILES = 4


def _conv_kernel(z_ref, dw_ref, db_ref, lg_ref, lb_ref, u_ref, vs_ref, c_ref):
    s = z_ref.shape[0]
    span = s + 2 * CONV_PAD
    first = CONV_PAD - CONV_WIDTH // 2
    glu_rows = 256

    for half in range(CONV_CH // CONV_HALF):
        l0 = half * CONV_HALF
        vs_ref[0, 0:CONV_PAD, :] = jnp.zeros((CONV_PAD, CONV_HALF), F32)
        vs_ref[0, CONV_PAD + s:span + SUBLANES, :] = jnp.zeros((CONV_PAD + SUBLANES, CONV_HALF), F32)

        def glu(i, carry):
            r = pl.multiple_of(i * glu_rows, glu_rows)
            a = z_ref[pl.ds(r, glu_rows), l0:l0 + CONV_HALF].astype(F32)
            g = z_ref[pl.ds(r, glu_rows), CONV_CH + l0:CONV_CH + l0 + CONV_HALF].astype(F32)
            vs_ref[0, pl.ds(CONV_PAD + r, glu_rows), :] = a * jax.nn.sigmoid(g)
            return carry

        lax.fori_loop(0, s // glu_rows, glu, 0)

        def shift(i, carry):
            r = pl.multiple_of(i * CONV_SHIFT_ROWS, SUBLANES)
            win = vs_ref[0, pl.ds(r, CONV_SHIFT_ROWS + SUBLANES), :]
            for k in range(1, SUBLANES):
                vs_ref[k, pl.ds(r, CONV_SHIFT_ROWS), :] = win[k:k + CONV_SHIFT_ROWS]
            return carry

        assert span % CONV_SHIFT_ROWS == 0
        lax.fori_loop(0, span // CONV_SHIFT_ROWS, shift, 0)

        def conv(i, carry):
            r = pl.multiple_of(i * CONV_ROWS, CONV_ROWS)
            acc = jnp.zeros((CONV_ROWS, CONV_HALF), F32) + db_ref[:, l0:l0 + CONV_HALF]
            for w in range(CONV_WIDTH):
                k, a = (first + w) % SUBLANES, (first + w) // SUBLANES
                tap = vs_ref[k, pl.ds(r + a * SUBLANES, CONV_ROWS), :]
                acc = acc + tap * dw_ref[w:w + 1, l0:l0 + CONV_HALF]
            c_ref[pl.ds(r, CONV_ROWS), l0:l0 + CONV_HALF] = acc
            return carry

        lax.fori_loop(0, s // CONV_ROWS, conv, 0)

    def norm(i, carry):
        for t in range(CONV_NORM_TILES):
            r = pl.multiple_of((i * CONV_NORM_TILES + t) * CONV_ROWS, CONV_ROWS)
            c = c_ref[pl.ds(r, CONV_ROWS), :]
            xc = c - jnp.mean(c, axis=-1, keepdims=True)
            var = jnp.mean(xc * xc, axis=-1, keepdims=True)
            y = xc * lax.rsqrt(var + EPS) * lg_ref[...] + lb_ref[...]
            u_ref[pl.ds(r, CONV_ROWS), :] = (y * jax.nn.sigmoid(y)).astype(BF16)
        return carry

    assert s % (CONV_ROWS * CONV_NORM_TILES) == 0
    lax.fori_loop(0, s // (CONV_ROWS * CONV_NORM_TILES), norm, 0)


def _conv_mixer(z, bsz, s, dw, db, lg, lb):
    return pl.pallas_call(
        _conv_kernel,
        grid=(bsz,),
        in_specs=[pl.BlockSpec((s, 2 * CONV_CH), lambda b: (b, ZA // (2 * CONV_CH))),
                  _resident((CONV_WIDTH, CONV_CH)), _resident((1, CONV_CH)),
                  _resident((1, CONV_CH)), _resident((1, CONV_CH))],
        out_specs=pl.BlockSpec((s, CONV_CH), lambda b: (b, 0)),
        out_shape=jax.ShapeDtypeStruct((bsz * s, CONV_CH), BF16),
        scratch_shapes=[pltpu.VMEM((SUBLANES, s + 2 * CONV_PAD + SUBLANES, CONV_HALF), F32),
                        pltpu.VMEM((s, CONV_CH), F32)],
        compiler_params=_cparams(1),
        name="conv_mixer",
    )(z, dw, db, lg, lb)


def _head_masks():
    lane = lax.broadcasted_iota(jnp.int32, (1, LANES), 1)
    return lane < HEAD_DIM


ATTN_UNROLL = 8


def _softmax_pv(sc, vw, sink=None):
    m = jnp.max(sc, axis=-1, keepdims=True)
    if sink is not None:
        m = jnp.maximum(m, sink)
    p = jnp.exp(sc - m)
    l = jnp.sum(p, axis=-1, keepdims=True)
    if sink is not None:
        l = l + jnp.exp(sink - m)
    return _dot(p.astype(BF16), vw) * (1.0 / l)


def _wattn_kernel(pairs_per_kv, sink_ref, q_ref, k_ref, v_ref, bias_ref, o_ref, kp_ref, vp_ref):
    hp = pl.program_id(1)
    s = q_ref.shape[0]
    nb = s // BLOCK
    lo = _head_masks()
    zero = jnp.zeros((), BF16)

    @pl.when(hp % pairs_per_kv == 0)
    def _():
        pad = jnp.zeros((BLOCK, LANES), BF16)
        for src, dst in ((k_ref, kp_ref), (v_ref, vp_ref)):
            dst[0:BLOCK, :] = pad
            dst[BLOCK:BLOCK + s, :] = src[...]
            dst[BLOCK + s:2 * BLOCK + s, :] = pad

    def group(g, carry):
        rows, scores = [], []
        for u in range(ATTN_UNROLL):
            n = g * ATTN_UNROLL + u
            r = pl.multiple_of(n * BLOCK, BLOCK)
            var = jnp.where(n == 0, 0, jnp.where(n == nb - 1, 2, 1))
            q = q_ref[pl.ds(r, BLOCK), :]
            kw = kp_ref[pl.ds(r, 3 * BLOCK), :]
            rows.append(r)
            for hh in range(2):
                qm = jnp.where(lo if hh == 0 else jnp.logical_not(lo), q, zero)
                scores.append(_dot_nt(qm, kw) + bias_ref[hh, var])
        for u, r in enumerate(rows):
            vw = vp_ref[pl.ds(r, 3 * BLOCK), :]
            outs = [_softmax_pv(scores[2 * u + hh], vw, sink_ref[0, 2 * hp + hh]) for hh in range(2)]
            o_ref[pl.ds(r, BLOCK), :] = jnp.where(lo, outs[0], outs[1]).astype(BF16)
        return carry

    lax.fori_loop(0, nb // ATTN_UNROLL, group, 0)


def _window_attention(z, bsz, s, sink, bias):
    npair = WB_HEADS // 2
    pairs_per_kv = npair // WB_KV_HEADS
    nv = bias.shape[1]
    assert (s // BLOCK) % ATTN_UNROLL == 0 and s // BLOCK >= 2
    return pl.pallas_call(
        functools.partial(_wattn_kernel, pairs_per_kv),
        grid=(bsz, npair),
        in_specs=[pl.BlockSpec(memory_space=pltpu.SMEM),
                  pl.BlockSpec((s, LANES), lambda b, hp: (b, ZBQ // LANES + hp)),
                  pl.BlockSpec((s, LANES), lambda b, hp: (b, ZBK // LANES + hp // pairs_per_kv)),
                  pl.BlockSpec((s, LANES), lambda b, hp: (b, ZBV // LANES + hp // pairs_per_kv)),
                  pl.BlockSpec((2, nv, BLOCK, 3 * BLOCK), lambda b, hp: (hp, 0, 0, 0))],
        out_specs=pl.BlockSpec((s, LANES), lambda b, hp: (b, hp)),
        out_shape=jax.ShapeDtypeStruct((bsz * s, WB_HEADS * HEAD_DIM), BF16),
        scratch_shapes=[pltpu.VMEM((s + 2 * BLOCK, LANES), BF16), pltpu.VMEM((s + 2 * BLOCK, LANES), BF16)],
        compiler_params=_cparams(2),
        name="window_attention",
    )(sink, z, z, z, bias)


def _nattn_kernel(n_edge, n_pairs, q_ref, k_ref, v_ref, bias_ref, o_ref):
    lo = _head_masks()
    zero = jnp.zeros((), BF16)
    qn, kn = NA_QROWS * GRID_W, NA_KROWS * GRID_W
    max_ks = k_ref.shape[0] // GRID_W - NA_KROWS

    def group(g, carry):
        rows, scores = [], []
        for u in range(ATTN_UNROLL):
            i = g * ATTN_UNROLL + u
            r = pl.multiple_of(i * qn, qn)
            ks = jnp.clip(NA_QROWS * i - NA_KH // 2, 0, max_ks)
            rk = pl.multiple_of(ks * GRID_W, LANES)
            var = jnp.where(i < n_edge, i,
                            jnp.where(i >= n_pairs - n_edge, i - (n_pairs - 2 * n_edge - 1), n_edge))
            q = q_ref[pl.ds(r, qn), :]
            kw = k_ref[pl.ds(rk, kn), :]
            rows.append((r, rk))
            for hh in range(2):
                qm = jnp.where(lo if hh == 0 else jnp.logical_not(lo), q, zero)
                scores.append(_dot_nt(qm, kw) + bias_ref[0, var, hh])
        for u, (r, rk) in enumerate(rows):
            vw = v_ref[pl.ds(rk, kn), :]
            outs = [_softmax_pv(scores[2 * u + hh], vw) for hh in range(2)]
            o_ref[pl.ds(r, qn), :] = jnp.where(lo, outs[0], outs[1]).astype(BF16)
        return carry

    lax.fori_loop(0, n_pairs // ATTN_UNROLL, group, 0)


def _neighbourhood_attention(z, bsz, s, bias, layer):
    rows = s // GRID_W
    _, variants, variant_of_pair, kstart = _na_tables(rows)
    n_pairs = rows // NA_QROWS
    nv = len(variants)
    n_edge = (nv - 1) // 2
    assert n_pairs % ATTN_UNROLL == 0
    for i in range(n_pairs):
        want = i if i < n_edge else (i - (n_pairs - 2 * n_edge - 1) if i >= n_pairs - n_edge else n_edge)
        assert variant_of_pair[i] == want
        assert kstart[i] == int(np.clip(NA_QROWS * i - NA_KH // 2, 0, rows - NA_KROWS))
        assert (kstart[i] * GRID_W) % LANES == 0
    npair = NA_HEADS // 2
    qn, kn = NA_QROWS * GRID_W, NA_KROWS * GRID_W
    return pl.pallas_call(
        functools.partial(_nattn_kernel, n_edge, n_pairs),
        grid=(npair, bsz),
        in_specs=[pl.BlockSpec((s, LANES), lambda hp, b: (b, ZCQ // LANES + hp)),
                  pl.BlockSpec((s, LANES), lambda hp, b: (b, ZCK // LANES + hp)),
                  pl.BlockSpec((s, LANES), lambda hp, b: (b, ZCV // LANES + hp)),
                  pl.BlockSpec((1, nv, 2, qn, kn), lambda hp, b: (layer, 0, hp, 0, 0))],
        out_specs=pl.BlockSpec((s, LANES), lambda hp, b: (b, hp)),
        out_shape=jax.ShapeDtypeStruct((bsz * s, NA_HEADS * HEAD_DIM), BF16),
        compiler_params=_cparams(2),
        name="neighbourhood_attention",
    )(z, z, z, bias)


ROUTE_IDX_LANE = N_EXPERTS
ROUTE_W_LANE = N_EXPERTS + 2


def _merge_kernel(with_router, u_ref, ob_ref, oc_ref, g_ref, x_ref, wa_ref, wb_ref, wc_ref, wo_ref,
                  fg_ref, *rest):
    if with_router:
        wrh_ref, wrl_ref, br_ref, xo_ref, hf_ref, slab_ref = rest
    else:
        xo_ref, hf_ref = rest
    d = D_MODEL
    m = g_ref[:, 0:d].astype(F32) * _dot(u_ref[...], wa_ref[...])
    m = m + g_ref[:, d:2 * d].astype(F32) * _dot(ob_ref[...], wb_ref[...])
    m = m + g_ref[:, 2 * d:3 * d].astype(F32) * _dot(oc_ref[...], wc_ref[...])
    xn = x_ref[...] + _dot(m.astype(BF16), wo_ref[...])
    xo_ref[...] = xn
    hf = xn * lax.rsqrt(jnp.mean(xn * xn, axis=-1, keepdims=True) + EPS) * fg_ref[...]
    hf_ref[...] = hf.astype(hf_ref.dtype)
    if with_router:
        hf_hi = hf.astype(BF16)
        hf_lo = (hf - hf_hi.astype(F32)).astype(BF16)
        logits = (_dot(hf_hi, wrh_ref[...]) + _dot(hf_lo, wrh_ref[...]) + _dot(hf_hi, wrl_ref[...])
                  + br_ref[...])
        lane = lax.broadcasted_iota(jnp.int32, logits.shape, 1)
        m1 = jnp.max(logits, axis=-1, keepdims=True)
        i1 = jnp.min(jnp.where(logits == m1, lane, LANES), axis=-1, keepdims=True)
        rest_l = jnp.where(lane == i1, -jnp.inf, logits)
        m2 = jnp.max(rest_l, axis=-1, keepdims=True)
        i2 = jnp.min(jnp.where(rest_l == m2, lane, LANES), axis=-1, keepdims=True)
        e2 = jnp.exp(m2 - m1)
        w1 = 1.0 / (1.0 + e2)
        w2 = e2 / (1.0 + e2)
        slab = jnp.where(lane == ROUTE_IDX_LANE, i1.astype(F32), 0.0)
        slab = jnp.where(lane == ROUTE_IDX_LANE + 1, i2.astype(F32), slab)
        slab = jnp.where(lane == ROUTE_W_LANE, w1, slab)
        slab = jnp.where(lane == ROUTE_W_LANE + 1, w2, slab)
        slab_ref[...] = slab


def _merge(u, ob, oc, z, x, wa, wb, wc, wo, fg, router=None, tm=512):
    n = x.shape[0]
    with_router = router is not None
    row = lambda w: pl.BlockSpec((tm, w), lambda i: (i, 0))
    in_specs = [row(CONV_CH), row(WB_HEADS * HEAD_DIM), row(NA_HEADS * HEAD_DIM),
                pl.BlockSpec((tm, ZA), lambda i: (i, 0)), row(D_MODEL),
                _resident(wa.shape), _resident(wb.shape), _resident(wc.shape), _resident(wo.shape),
                _resident((1, D_MODEL))]
    args = [u, ob, oc, z, x, wa, wb, wc, wo, fg]
    out_specs = [row(D_MODEL), row(D_MODEL)]
    out_shape = [jax.ShapeDtypeStruct((n, D_MODEL), F32),
                 jax.ShapeDtypeStruct((n, D_MODEL), F32 if with_router else BF16)]
    if with_router:
        in_specs += [_resident((D_MODEL, LANES)), _resident((D_MODEL, LANES)), _resident((1, LANES))]
        args += list(router)
        out_specs.append(row(LANES))
        out_shape.append(jax.ShapeDtypeStruct((n, LANES), F32))
    return pl.pallas_call(
        functools.partial(_merge_kernel, with_router),
        grid=(n // tm,),
        in_specs=in_specs, out_specs=out_specs, out_shape=out_shape,
        compiler_params=_cparams(1),
        name="merge_router" if with_router else "merge",
    )(*args)


FFN_CHUNK = 1024


def _ffn_kernel(hf_ref, x_ref, w1_ref, w3_ref, w2_ref, o_ref):
    dff = w1_ref.shape[1]
    hf = hf_ref[...]
    acc = x_ref[...]
    for c0 in range(0, dff, FFN_CHUNK):
        c1 = min(c0 + FFN_CHUNK, dff)
        a = _dot(hf, w1_ref[:, c0:c1])
        b = _dot(hf, w3_ref[:, c0:c1])
        hm = (a * jax.nn.sigmoid(a) * b).astype(BF16)
        acc = acc + _dot(hm, w2_ref[c0:c1, :])
    o_ref[...] = acc


def _dense_ffn(hf, x, w1, w3, w2, tm=512):
    n = x.shape[0]
    row = pl.BlockSpec((tm, D_MODEL), lambda i: (i, 0))
    return pl.pallas_call(
        _ffn_kernel,
        grid=(n // tm,),
        in_specs=[row, row, _resident(w1.shape), _resident(w3.shape), _resident(w2.shape)],
        out_specs=row,
        out_shape=jax.ShapeDtypeStruct((n, D_MODEL), F32),
        input_output_aliases={1: 0},
        compiler_params=_cparams(1),
        name="dense_ffn",
    )(hf, x, w1, w3, w2)


MOE_TILE = 1024
MOE_FF_CHUNK = 512
ROUTE_ROWS = 256


def _row_copy(src, dst, sem):
    return pltpu.make_async_copy(src, dst, sem)


def _dispatch_kernel(dest_ref, pad_base_ref, pad_cnt_ref, hf_ref, xs_ref, buf_ref, zrow_ref, load_sem,
                     row_sem, zero_sem):
    i = pl.program_id(0)
    last = pl.num_programs(0) - 1
    slot = i % 2

    def load(step, slot_):
        src = hf_ref.at[pl.ds(pl.multiple_of(step * ROUTE_ROWS, ROUTE_ROWS), ROUTE_ROWS)]
        return pltpu.make_async_copy(src, buf_ref.at[slot_], load_sem.at[slot_])

    def drain(slot_):
        for _ in range(2 * ROUTE_ROWS):
            _row_copy(buf_ref.at[slot_, pl.ds(0, 1)], xs_ref.at[pl.ds(0, 1)], row_sem.at[slot_]).wait()

    def zero_row(dst_row):
        return _row_copy(zrow_ref.at[pl.ds(0, 1)], xs_ref.at[pl.ds(dst_row, 1)], zero_sem)

    @pl.when(i == 0)
    def _():
        load(0, 0).start()
        zrow_ref[...] = jnp.zeros_like(zrow_ref)
        for e in range(pad_cnt_ref.shape[0]):
            def zstart(j, carry):
                zero_row(pad_base_ref[e] + j).start()
                return carry

            lax.fori_loop(0, pad_cnt_ref[e], zstart, 0)

    load(i, slot).wait()

    @pl.when(i > 0)
    def _():
        drain(1 - slot)

    @pl.when(i < last)
    def _():
        load(i + 1, 1 - slot).start()

    def start(r, carry):
        for k in range(2):
            d = dest_ref[0, 0, 2 * r + k]
            _row_copy(buf_ref.at[slot, pl.ds(r, 1)], xs_ref.at[pl.ds(d, 1)], row_sem.at[slot]).start(priority=k)
        return carry

    lax.fori_loop(0, ROUTE_ROWS, start, 0, unroll=8)

    @pl.when(i == last)
    def _():
        drain(slot)
        for e in range(pad_cnt_ref.shape[0]):
            def zwait(j, carry):
                zero_row(0).wait()
                return carry

            lax.fori_loop(0, pad_cnt_ref[e], zwait, 0)


def _dispatch(hf, dest, pad_base, pad_cnt, n_rows):
    n = hf.shape[0]
    steps = n // ROUTE_ROWS
    smem = pl.BlockSpec(memory_space=pltpu.SMEM)
    return pl.pallas_call(
        _dispatch_kernel,
        grid=(steps,),
        in_specs=[pl.BlockSpec((1, 1, 2 * ROUTE_ROWS), lambda i: (i, 0, 0), memory_space=pltpu.SMEM),
                  smem, smem, pl.BlockSpec(memory_space=pl.ANY)],
        out_specs=pl.BlockSpec(memory_space=pl.ANY),
        out_shape=jax.ShapeDtypeStruct((n_rows, D_MODEL), F32),
        scratch_shapes=[pltpu.VMEM((2, ROUTE_ROWS, D_MODEL), F32), pltpu.VMEM((SUBLANES, D_MODEL), F32),
                        pltpu.SemaphoreType.DMA((2,)), pltpu.SemaphoreType.DMA((2,)),
                        pltpu.SemaphoreType.DMA(())],
        compiler_params=_cparams(1),
        name="moe_dispatch",
    )(dest.reshape(steps, 1, 2 * ROUTE_ROWS), pad_base, pad_cnt, hf)


F8 = jnp.float8_e4m3fn
F32_EXP_BIAS = 127
F32_MANT_BITS = 23
F8_TOP_EXP = 7


def _pow2_scale(amax):
    e = jnp.right_shift(lax.bitcast_convert_type(amax, jnp.int32), F32_MANT_BITS) & 0xFF
    se = jnp.clip(2 * F32_EXP_BIAS + F8_TOP_EXP - e, 1, 2 * F32_EXP_BIAS)
    scale = lax.bitcast_convert_type(jnp.left_shift(se, F32_MANT_BITS), F32)
    inv = lax.bitcast_convert_type(jnp.left_shift(2 * F32_EXP_BIAS - se, F32_MANT_BITS), F32)
    return scale, inv


def _quantize_rows(v):
    scale, inv = _pow2_scale(jnp.max(jnp.abs(v), axis=-1, keepdims=True))
    return (v * scale).astype(F8), inv


def _experts_kernel(te_ref, na_ref, winv_ref, xs_ref, w1_ref, w3_ref, w2_ref, ys_ref, xb_ref, xinv_ref):
    i = pl.program_id(0)
    c = pl.program_id(1)
    active = i < na_ref[0]
    e = te_ref[i]

    @pl.when(c == 0)
    def _():
        ys_ref[...] = jnp.zeros_like(ys_ref)

    @pl.when(jnp.logical_and(active, c == 0))
    def _():
        xb_ref[...], xinv_ref[...] = _quantize_rows(xs_ref[...])

    @pl.when(active)
    def _():
        half = MOE_TILE // 2
        gates = []
        for h in range(2):
            rows = slice(h * half, (h + 1) * half)
            xb = xb_ref[rows, :]
            xinv = xinv_ref[rows, :]
            gates.append((_dot(xb, w1_ref[0]) * (xinv * winv_ref[e, 0]),
                          _dot(xb, w3_ref[0]) * (xinv * winv_ref[e, 1])))
        for h, (a, b) in enumerate(gates):
            hm, hinv = _quantize_rows(a * jax.nn.sigmoid(a) * b)
            ys_ref[h * half:(h + 1) * half, :] += _dot(hm, w2_ref[0]) * (hinv * winv_ref[e, 2])


def _quantize_expert_weights(w):
    scale, inv = _pow2_scale(jnp.max(jnp.abs(w.astype(F32)), axis=(1, 2)))
    return (w * scale[:, None, None]).astype(F8), inv


def _experts(xs, tile_expert, n_active, w1, w3, w2):
    n_rows = xs.shape[0]
    (w1, i1), (w3, i3), (w2, i2) = (_quantize_expert_weights(w) for w in (w1, w3, w2))
    winv = jnp.stack([i1, i3, i2], axis=1)
    dff = w1.shape[2]
    n_chunks = dff // MOE_FF_CHUNK

    def chunk(i, c, na):
        return jnp.where(i < na[0], c, n_chunks - 1)

    grid_spec = pltpu.PrefetchScalarGridSpec(
        num_scalar_prefetch=2,
        grid=(n_rows // MOE_TILE, dff // MOE_FF_CHUNK),
        in_specs=[pl.BlockSpec(memory_space=pltpu.SMEM),
                  pl.BlockSpec((MOE_TILE, D_MODEL), lambda i, c, te, na: (jnp.minimum(i, na[0] - 1), 0)),
                  pl.BlockSpec((1, D_MODEL, MOE_FF_CHUNK), lambda i, c, te, na: (te[i], 0, chunk(i, c, na))),
                  pl.BlockSpec((1, D_MODEL, MOE_FF_CHUNK), lambda i, c, te, na: (te[i], 0, chunk(i, c, na))),
                  pl.BlockSpec((1, MOE_FF_CHUNK, D_MODEL), lambda i, c, te, na: (te[i], chunk(i, c, na), 0))],
        out_specs=pl.BlockSpec((MOE_TILE, D_MODEL), lambda i, c, te, na: (i, 0)),
        scratch_shapes=[pltpu.VMEM((MOE_TILE, D_MODEL), F8), pltpu.VMEM((MOE_TILE, 1), F32)],
    )
    return pl.pallas_call(
        _experts_kernel,
        grid_spec=grid_spec,
        out_shape=jax.ShapeDtypeStruct((n_rows, D_MODEL), F32),
        compiler_params=_cparams(2),
        name="moe_experts",
    )(tile_expert, n_active, winv, xs, w1, w3, w2)


def _combine_kernel(dest_ref, next_dest_ref, x_ref, slab_ref, ys_ref, o_ref, y_ref, sem):
    i = pl.program_id(0)
    slot = i % 2

    def gather(d_ref, slot_):
        def start(r, carry):
            for k in range(2):
                d = d_ref[0, 0, 2 * r + k]
                _row_copy(ys_ref.at[pl.ds(d, 1)], y_ref.at[slot_, k, pl.ds(r, 1)], sem.at[slot_]).start()
            return carry

        lax.fori_loop(0, ROUTE_ROWS, start, 0, unroll=8)

    @pl.when(i == 0)
    def _():
        gather(dest_ref, 0)

    @pl.when(i + 1 < pl.num_programs(0))
    def _():
        gather(next_dest_ref, 1 - slot)

    for _ in range(2 * ROUTE_ROWS):
        _row_copy(ys_ref.at[pl.ds(0, 1)], y_ref.at[slot, 0, pl.ds(0, 1)], sem.at[slot]).wait()
    w0 = slab_ref[:, ROUTE_W_LANE:ROUTE_W_LANE + 1]
    w1 = slab_ref[:, ROUTE_W_LANE + 1:ROUTE_W_LANE + 2]
    o_ref[...] = x_ref[...] + w0 * y_ref[slot, 0] + w1 * y_ref[slot, 1]


def _combine(x, slab, dest, ys):
    n = x.shape[0]
    steps = n // ROUTE_ROWS
    row = pl.BlockSpec((ROUTE_ROWS, D_MODEL), lambda i: (i, 0))
    dest3 = dest.reshape(steps, 1, 2 * ROUTE_ROWS)
    dest_block = (1, 1, 2 * ROUTE_ROWS)
    return pl.pallas_call(
        _combine_kernel,
        grid=(steps,),
        in_specs=[pl.BlockSpec(dest_block, lambda i: (i, 0, 0), memory_space=pltpu.SMEM),
                  pl.BlockSpec(dest_block, lambda i: (jnp.minimum(i + 1, steps - 1), 0, 0),
                               memory_space=pltpu.SMEM),
                  row, pl.BlockSpec((ROUTE_ROWS, LANES), lambda i: (i, 0)),
                  pl.BlockSpec(memory_space=pl.ANY)],
        out_specs=row,
        out_shape=jax.ShapeDtypeStruct((n, D_MODEL), F32),
        scratch_shapes=[pltpu.VMEM((2, 2, ROUTE_ROWS, D_MODEL), F32), pltpu.SemaphoreType.DMA((2,))],
        input_output_aliases={2: 0},
        compiler_params=_cparams(1),
        name="moe_combine",
    )(dest3, dest3, x, slab, ys)


def _routing_plan(slab, n_rows):
    n = slab.shape[0]
    idx = slab[:, ROUTE_IDX_LANE:ROUTE_IDX_LANE + 2].astype(jnp.int32)
    onehot = (idx[:, :, None] == jnp.arange(N_EXPERTS, dtype=jnp.int32)).astype(jnp.int32).sum(axis=1)
    csum = jnp.cumsum(onehot, axis=0)
    rank = csum - onehot
    counts = csum[-1]
    padded = ((counts + MOE_TILE - 1) // MOE_TILE) * MOE_TILE
    ends = jnp.cumsum(padded)
    offs = ends - padded
    dest = offs[idx] + jnp.take_along_axis(rank, idx, axis=1)
    tile_start = jnp.arange(n_rows // MOE_TILE, dtype=jnp.int32) * MOE_TILE
    tile_expert = jnp.minimum(jnp.sum(tile_start[:, None] >= ends[None, :], axis=1), N_EXPERTS - 1)
    n_active = (ends[-1] // MOE_TILE).reshape(1)
    i32 = lambda v: v.astype(jnp.int32)
    pad_base = jnp.concatenate([offs + counts, ends[-1:]])
    pad_cnt = jnp.concatenate([padded - counts, n_rows - ends[-1:]])
    return i32(dest.reshape(-1)), i32(tile_expert), i32(n_active), i32(pad_base), i32(pad_cnt)


def _moe(hf, x, slab, w1, w3, w2):
    n = x.shape[0]
    n_rows = 2 * n + N_EXPERTS * MOE_TILE
    dest, tile_expert, n_active, pad_base, pad_cnt = _routing_plan(slab, n_rows)
    xs = _dispatch(hf, dest, pad_base, pad_cnt, n_rows)
    ys = _experts(xs, tile_expert, n_active, w1, w3, w2)
    return _combine(x, slab, dest, ys)


def _prep_inproj(w_in_l, gate_b_l, qn_b, kn_b, qn_c, kn_c):
    a_in = 2 * CONV_CH
    b_q = WB_HEADS * HEAD_DIM
    b_kv = WB_KV_HEADS * HEAD_DIM
    c_w = NA_HEADS * HEAD_DIM
    o = 0
    wa = w_in_l[:, o:o + a_in]; o += a_in
    wbq = w_in_l[:, o:o + b_q]; o += b_q
    wbk = w_in_l[:, o:o + b_kv]; o += b_kv
    wbv = w_in_l[:, o:o + b_kv]; o += b_kv
    wcq = w_in_l[:, o:o + c_w]; o += c_w
    wck = w_in_l[:, o:o + c_w]; o += c_w
    wcv = w_in_l[:, o:o + c_w]; o += c_w
    wg = w_in_l[:, o:]

    def dup(w):
        return jnp.repeat(w.reshape(D_MODEL, WB_KV_HEADS, 1, HEAD_DIM), 2, axis=2).reshape(D_MODEL, -1)

    w = jnp.concatenate([wg, wa, wbq, dup(wbk), dup(wbv), wcq, wck, wcv], axis=1).astype(BF16)
    scale = HEAD_DIM ** -0.5
    ones = lambda k: jnp.ones((k,), F32)
    cg = jnp.concatenate([
        ones(ZBQ), jnp.tile(qn_b.astype(F32), WB_HEADS) * scale, jnp.tile(kn_b.astype(F32), 2 * WB_KV_HEADS),
        ones(ZCQ - ZBV), jnp.tile(qn_c.astype(F32), NA_HEADS) * scale, jnp.tile(kn_c.astype(F32), NA_HEADS),
        ones(ZC - ZCV)])
    return w, cg.reshape(1, ZC), gate_b_l.astype(F32).reshape(1, -1)


MXU_TILE = 256


def _segment_mean_matrix():
    idx = np.arange(MXU_TILE) // HEAD_DIM
    return jnp.asarray((idx[:, None] == idx[None, :]).astype(np.float32) / HEAD_DIM, dtype=BF16)


def kernel(x, t5_rel_bias, attn_norm_g, w_in, gate_b, conv_dw_w, conv_dw_b, conv_ln_g, conv_ln_b,
           conv_w_out, wb_qn_g, wb_kn_g, wb_sink, wb_w_out, na_qn_g, na_kn_g, na_rpb, na_w_out, w_o,
           ffn_norm_g, ffn_w1, ffn_w3, ffn_w2, moe_w_router, moe_b_router, moe_w1, moe_w3, moe_w2):
    bsz, s, d = x.shape
    depth = w_in.shape[0]
    assert d == D_MODEL and s % BLOCK == 0 and s % (NA_QROWS * GRID_W) == 0
    n = bsz * s
    xf = x.reshape(n, d).astype(F32)
    row = lambda v: v.astype(F32).reshape(1, -1)

    t5_bias = _t5_bias(t5_rel_bias)
    na_bias = _na_bias(na_rpb, s // GRID_W)
    bd = _segment_mean_matrix()

    for layer in range(depth):
        w, cg, gb = _prep_inproj(w_in[layer], gate_b[layer], wb_qn_g[layer], wb_kn_g[layer],
                                 na_qn_g[layer], na_kn_g[layer])
        z = _inproj(xf, row(attn_norm_g[layer]), w, cg, gb, bd)
        u = _conv_mixer(z, bsz, s, conv_dw_w[layer].astype(F32), row(conv_dw_b[layer]),
                        row(conv_ln_g[layer]), row(conv_ln_b[layer]))
        ob = _window_attention(z, bsz, s, row(wb_sink[layer]), t5_bias)
        oc = _neighbourhood_attention(z, bsz, s, na_bias, layer)
        weights = (conv_w_out[layer].astype(BF16), wb_w_out[layer].astype(BF16),
                   na_w_out[layer].astype(BF16), w_o[layer].astype(BF16), row(ffn_norm_g[layer]))
        i = layer // 2
        if layer % 2 == 0:
            xf, hf = _merge(u, ob, oc, z, xf, *weights)
            xf = _dense_ffn(hf, xf, ffn_w1[i].astype(BF16), ffn_w3[i].astype(BF16), ffn_w2[i].astype(BF16))
        else:
            wr = jnp.zeros((D_MODEL, LANES), F32).at[:, :N_EXPERTS].set(moe_w_router[i].astype(F32))
            br = jnp.full((1, LANES), -jnp.inf, F32).at[0, :N_EXPERTS].set(moe_b_router[i].astype(F32))
            wr_hi = wr.astype(BF16)
            wr_lo = (wr - wr_hi.astype(F32)).astype(BF16)
            xf, hf, slab = _merge(u, ob, oc, z, xf, *weights, router=(wr_hi, wr_lo, br))
            xf = _moe(hf, xf, slab, moe_w1[i], moe_w3[i], moe_w2[i])
    return xf.reshape(bsz, s, d).astype(x.dtype)
```

```python
import functools
import math

import numpy as np
import jax
import jax.numpy as jnp
from jax import lax
from jax.experimental import pallas as pl
from jax.experimental.pallas import tpu as pltpu

D_MODEL = 1024
HEAD_DIM = 64
CONV_CH = 512
CONV_WIDTH = 31
WB_HEADS = 8
WB_KV_HEADS = 2
WINDOW = 128
BLOCK = 128
NA_HEADS = 8
GRID_W = 64
NA_KH = 8
NA_KW = 16
REL_BUCKETS = 32
REL_MAX_DIST = 128
N_EXPERTS = 8
EPS = 1e-6
NEG_INF = -1e30

LANES = 128

ZG = 0
ZA = 3072
ZBQ = 4096
ZBK = 4608
ZBV = 4864
ZCQ = 5120
ZCK = 5632
ZCV = 6144
ZC = 6656
ZCHUNK = 512

VMEM_LIMIT = 56 * 1024 * 1024

F32 = jnp.float32
BF16 = jnp.bfloat16


def _cparams(grid_rank, vmem=VMEM_LIMIT):
    return pltpu.CompilerParams(dimension_semantics=("arbitrary",) * grid_rank, vmem_limit_bytes=vmem)


def _resident(shape):
    nd = len(shape)
    return pl.BlockSpec(shape, lambda *_: (0,) * nd, pipeline_mode=pl.Buffered(1))


def _dot(a, b):
    return jnp.dot(a, b, preferred_element_type=F32)


def _dot_nt(a, b):
    return lax.dot_general(a, b, (((1,), (1,)), ((), ())), preferred_element_type=F32)


def _t5_bucket_np(rel):
    half = REL_BUCKETS // 2
    max_exact = half // 2
    ret = np.where(rel > 0, half, 0)
    n = np.abs(rel)
    nf = np.maximum(n, 1).astype(np.float32)
    large = max_exact + (np.log(nf / np.float32(max_exact)) / np.float32(math.log(REL_MAX_DIST / max_exact))
                         * np.float32(half - max_exact)).astype(np.int32)
    large = np.minimum(large, half - 1)
    return ret + np.where(n < max_exact, n, large)


def _window_bucket_table():
    col = np.arange(3 * BLOCK)[None, :]
    rel = (col - BLOCK) - np.arange(BLOCK)[:, None]
    band = np.where(np.abs(rel) <= WINDOW, _t5_bucket_np(rel), -1)
    first = np.where(col >= BLOCK, band, -1)
    last = np.where(col < 2 * BLOCK, band, -1)
    return np.stack([first, band, last]).astype(np.int32)


NA_QROWS = 2
NA_KROWS = 10


def _na_tables(rows):
    kh = min(NA_KH, rows)
    col = np.arange(GRID_W)
    col_start = np.clip(col - NA_KW // 2, 0, GRID_W - NA_KW)
    kc = np.arange(GRID_W)[None, :]
    inwin = (kc >= col_start[:, None]) & (kc < col_start[:, None] + NA_KW)
    dmat = np.where(inwin, kc - col[:, None] + NA_KW - 1, -1).astype(np.int32)
    variants, variant_of_pair, kstart = [], [], []
    for i in range(rows // NA_QROWS):
        ks = int(np.clip(NA_QROWS * i - kh // 2, 0, rows - NA_KROWS))
        tab = np.full((NA_QROWS, NA_KROWS), -1, np.int64)
        for a in range(NA_QROWS):
            r = NA_QROWS * i + a
            r0 = int(np.clip(r - kh // 2, 0, rows - kh))
            assert ks <= r0 and r0 + kh <= ks + NA_KROWS
            for kr in range(NA_KROWS):
                if r0 <= ks + kr < r0 + kh:
                    tab[a, kr] = ks + kr - r + NA_KH - 1
        for v, t in enumerate(variants):
            if np.array_equal(t, tab):
                variant_of_pair.append(v)
                break
        else:
            variant_of_pair.append(len(variants))
            variants.append(tab)
        kstart.append(ks)
    return dmat, variants, variant_of_pair, kstart


def _t5_bias_kernel(tab_ref, bucket_ref, o_ref):
    h = pl.program_id(0)
    for v in range(bucket_ref.shape[0]):
        bucket = bucket_ref[v]
        acc = jnp.full(bucket.shape, NEG_INF, F32)
        for b in range(REL_BUCKETS):
            acc = jnp.where(bucket == b, tab_ref[b, h], acc)
        o_ref[0, v] = acc


def _t5_bias(t5_rel_bias):
    bucket = jnp.asarray(_window_bucket_table())
    nv = bucket.shape[0]
    return pl.pallas_call(
        _t5_bias_kernel,
        grid=(WB_HEADS,),
        in_specs=[pl.BlockSpec(memory_space=pltpu.SMEM),
                  pl.BlockSpec((nv, BLOCK, 3 * BLOCK), lambda h: (0, 0, 0))],
        out_specs=pl.BlockSpec((1, nv, BLOCK, 3 * BLOCK), lambda h: (h, 0, 0, 0)),
        out_shape=jax.ShapeDtypeStruct((WB_HEADS, nv, BLOCK, 3 * BLOCK), F32),
        compiler_params=_cparams(1),
        name="t5_bias",
    )(t5_rel_bias.astype(F32), bucket)


def _na_bias_kernel(variants, rpb_ref, dmat_ref, o_ref):
    lh = pl.program_id(0) * NA_HEADS + pl.program_id(1)
    dmat = dmat_ref[...]
    ncol = 2 * NA_KW - 1
    base = []
    for dr in range(2 * NA_KH - 1):
        acc = jnp.full(dmat.shape, NEG_INF, F32)
        for d in range(ncol):
            acc = jnp.where(dmat == d, rpb_ref[lh, dr * ncol + d], acc)
        base.append(acc)
    neg = jnp.full(dmat.shape, NEG_INF, F32)
    for v, tab in enumerate(variants):
        for a in range(NA_QROWS):
            for kr in range(NA_KROWS):
                dr = int(tab[a, kr])
                tile = neg if dr < 0 else base[dr]
                o_ref[0, v, 0, a * GRID_W:(a + 1) * GRID_W, kr * GRID_W:(kr + 1) * GRID_W] = tile


def _na_bias(na_rpb, rows):
    depth = na_rpb.shape[0]
    dmat, variants, _, _ = _na_tables(rows)
    nv = len(variants)
    rpb = jnp.transpose(na_rpb.astype(F32), (0, 3, 1, 2)).reshape(depth * NA_HEADS, -1)
    qn, kn = NA_QROWS * GRID_W, NA_KROWS * GRID_W
    return pl.pallas_call(
        functools.partial(_na_bias_kernel, variants),
        grid=(depth, NA_HEADS),
        in_specs=[pl.BlockSpec(memory_space=pltpu.SMEM),
                  pl.BlockSpec((GRID_W, GRID_W), lambda l, h: (0, 0))],
        out_specs=pl.BlockSpec((1, nv, 1, qn, kn), lambda l, h: (l, 0, h, 0, 0)),
        out_shape=jax.ShapeDtypeStruct((depth, nv, NA_HEADS, qn, kn), F32),
        compiler_params=_cparams(2),
        name="na_bias",
    )(rpb, jnp.asarray(dmat))


def _inproj_kernel(x_ref, ng_ref, w_ref, cg_ref, gb_ref, bd_ref, z_ref, h_ref):
    x = x_ref[...]
    inv = lax.rsqrt(jnp.mean(x * x, axis=-1, keepdims=True) + EPS)
    h_ref[...] = (x * inv * ng_ref[...]).astype(BF16)

    def qknorm(zc, c0):
        w = bd_ref.shape[0]
        sq = (zc * zc).astype(BF16)
        ms = jnp.concatenate([_dot(sq[:, j:j + w], bd_ref[...]) for j in range(0, zc.shape[1], w)], axis=1)
        return zc * lax.rsqrt(ms + EPS) * cg_ref[:, c0:c0 + zc.shape[1]]

    for c0 in range(0, ZC, ZCHUNK):
        zc = _dot(h_ref[...], w_ref[:, c0:c0 + ZCHUNK])
        if c0 < ZA:
            out = jax.nn.sigmoid(zc + gb_ref[:, c0:c0 + ZCHUNK])
        elif c0 in (ZBQ, ZCQ, ZCK):
            out = qknorm(zc, c0)
        elif c0 == ZBK:
            half = ZBV - ZBK
            out = jnp.concatenate([qknorm(zc[:, :half], c0), zc[:, half:]], axis=1)
        else:
            out = zc
        z_ref[:, c0:c0 + ZCHUNK] = out.astype(BF16)


def _inproj(x, ng, w, cg, gb, bd, tm=512):
    n = x.shape[0]
    return pl.pallas_call(
        _inproj_kernel,
        grid=(n // tm,),
        in_specs=[pl.BlockSpec((tm, D_MODEL), lambda i: (i, 0)),
                  _resident((1, D_MODEL)), _resident((D_MODEL, ZC)), _resident((1, ZC)),
                  _resident((1, ZA)), _resident(bd.shape)],
        out_specs=pl.BlockSpec((tm, ZC), lambda i: (i, 0)),
        out_shape=jax.ShapeDtypeStruct((n, ZC), BF16),
        scratch_shapes=[pltpu.VMEM((tm, D_MODEL), BF16)],
        compiler_params=_cparams(1),
        name="inproj",
    )(x, ng, w, cg, gb, bd)


SUBLANES = 8
CONV_PAD = 16
CONV_HALF = 256
CONV_ROWS = 64
CONV_SHIFT_ROWS = 104
CONV_NORM_TILES = 4


def _conv_kernel(z_ref, dw_ref, db_ref, lg_ref, lb_ref, u_ref, vs_ref, c_ref):
    s = z_ref.shape[0]
    span = s + 2 * CONV_PAD
    first = CONV_PAD - CONV_WIDTH // 2
    glu_rows = 256

    for half in range(CONV_CH // CONV_HALF):
        l0 = half * CONV_HALF
        vs_ref[0, 0:CONV_PAD, :] = jnp.zeros((CONV_PAD, CONV_HALF), F32)
        vs_ref[0, CONV_PAD + s:span + SUBLANES, :] = jnp.zeros((CONV_PAD + SUBLANES, CONV_HALF), F32)

        def glu(i, carry):
            r = pl.multiple_of(i * glu_rows, glu_rows)
            a = z_ref[pl.ds(r, glu_rows), l0:l0 + CONV_HALF].astype(F32)
            g = z_ref[pl.ds(r, glu_rows), CONV_CH + l0:CONV_CH + l0 + CONV_HALF].astype(F32)
            vs_ref[0, pl.ds(CONV_PAD + r, glu_rows), :] = a * jax.nn.sigmoid(g)
            return carry

        lax.fori_loop(0, s // glu_rows, glu, 0)

        def shift(i, carry):
            r = pl.multiple_of(i * CONV_SHIFT_ROWS, SUBLANES)
            win = vs_ref[0, pl.ds(r, CONV_SHIFT_ROWS + SUBLANES), :]
            for k in range(1, SUBLANES):
                vs_ref[k, pl.ds(r, CONV_SHIFT_ROWS), :] = win[k:k + CONV_SHIFT_ROWS]
            return carry

        assert span % CONV_SHIFT_ROWS == 0
        lax.fori_loop(0, span // CONV_SHIFT_ROWS, shift, 0)

        def conv(i, carry):
            r = pl.multiple_of(i * CONV_ROWS, CONV_ROWS)
            acc = jnp.zeros((CONV_ROWS, CONV_HALF), F32) + db_ref[:, l0:l0 + CONV_HALF]
            for w in range(CONV_WIDTH):
                k, a = (first + w) % SUBLANES, (first + w) // SUBLANES
                tap = vs_ref[k, pl.ds(r + a * SUBLANES, CONV_ROWS), :]
                acc = acc + tap * dw_ref[w:w + 1, l0:l0 + CONV_HALF]
            c_ref[pl.ds(r, CONV_ROWS), l0:l0 + CONV_HALF] = acc
            return carry

        lax.fori_loop(0, s // CONV_ROWS, conv, 0)

    def norm(i, carry):
        for t in range(CONV_NORM_TILES):
            r = pl.multiple_of((i * CONV_NORM_TILES + t) * CONV_ROWS, CONV_ROWS)
            c = c_ref[pl.ds(r, CONV_ROWS), :]
            xc = c - jnp.mean(c, axis=-1, keepdims=True)
            var = jnp.mean(xc * xc, axis=-1, keepdims=True)
            y = xc * lax.rsqrt(var + EPS) * lg_ref[...] + lb_ref[...]
            u_ref[pl.ds(r, CONV_ROWS), :] = (y * jax.nn.sigmoid(y)).astype(BF16)
        return carry

    assert s % (CONV_ROWS * CONV_NORM_TILES) == 0
    lax.fori_loop(0, s // (CONV_ROWS * CONV_NORM_TILES), norm, 0)


def _conv_mixer(z, bsz, s, dw, db, lg, lb):
    return pl.pallas_call(
        _conv_kernel,
        grid=(bsz,),
        in_specs=[pl.BlockSpec((s, 2 * CONV_CH), lambda b: (b, ZA // (2 * CONV_CH))),
                  _resident((CONV_WIDTH, CONV_CH)), _resident((1, CONV_CH)),
                  _resident((1, CONV_CH)), _resident((1, CONV_CH))],
        out_specs=pl.BlockSpec((s, CONV_CH), lambda b: (b, 0)),
        out_shape=jax.ShapeDtypeStruct((bsz * s, CONV_CH), BF16),
        scratch_shapes=[pltpu.VMEM((SUBLANES, s + 2 * CONV_PAD + SUBLANES, CONV_HALF), F32),
                        pltpu.VMEM((s, CONV_CH), F32)],
        compiler_params=_cparams(1),
        name="conv_mixer",
    )(z, dw, db, lg, lb)


def _head_masks():
    lane = lax.broadcasted_iota(jnp.int32, (1, LANES), 1)
    return lane < HEAD_DIM


ATTN_UNROLL = 8


def _softmax_pv(sc, vw, sink=None):
    m = jnp.max(sc, axis=-1, keepdims=True)
    if sink is not None:
        m = jnp.maximum(m, sink)
    p = jnp.exp(sc - m)
    l = jnp.sum(p, axis=-1, keepdims=True)
    if sink is not None:
        l = l + jnp.exp(sink - m)
    return _dot(p.astype(BF16), vw) * (1.0 / l)


def _wattn_kernel(pairs_per_kv, sink_ref, q_ref, k_ref, v_ref, bias_ref, o_ref, kp_ref, vp_ref):
    hp = pl.program_id(1)
    s = q_ref.shape[0]
    nb = s // BLOCK
    lo = _head_masks()
    zero = jnp.zeros((), BF16)

    @pl.when(hp % pairs_per_kv == 0)
    def _():
        pad = jnp.zeros((BLOCK, LANES), BF16)
        for src, dst in ((k_ref, kp_ref), (v_ref, vp_ref)):
            dst[0:BLOCK, :] = pad
            dst[BLOCK:BLOCK + s, :] = src[...]
            dst[BLOCK + s:2 * BLOCK + s, :] = pad

    def group(g, carry):
        rows, scores = [], []
        for u in range(ATTN_UNROLL):
            n = g * ATTN_UNROLL + u
            r = pl.multiple_of(n * BLOCK, BLOCK)
            var = jnp.where(n == 0, 0, jnp.where(n == nb - 1, 2, 1))
            q = q_ref[pl.ds(r, BLOCK), :]
            kw = kp_ref[pl.ds(r, 3 * BLOCK), :]
            rows.append(r)
            for hh in range(2):
                qm = jnp.where(lo if hh == 0 else jnp.logical_not(lo), q, zero)
                scores.append(_dot_nt(qm, kw) + bias_ref[hh, var])
        for u, r in enumerate(rows):
            vw = vp_ref[pl.ds(r, 3 * BLOCK), :]
            outs = [_softmax_pv(scores[2 * u + hh], vw, sink_ref[0, 2 * hp + hh]) for hh in range(2)]
            o_ref[pl.ds(r, BLOCK), :] = jnp.where(lo, outs[0], outs[1]).astype(BF16)
        return carry

    lax.fori_loop(0, nb // ATTN_UNROLL, group, 0)


def _window_attention(z, bsz, s, sink, bias):
    npair = WB_HEADS // 2
    pairs_per_kv = npair // WB_KV_HEADS
    nv = bias.shape[1]
    assert (s // BLOCK) % ATTN_UNROLL == 0 and s // BLOCK >= 2
    return pl.pallas_call(
        functools.partial(_wattn_kernel, pairs_per_kv),
        grid=(bsz, npair),
        in_specs=[pl.BlockSpec(memory_space=pltpu.SMEM),
                  pl.BlockSpec((s, LANES), lambda b, hp: (b, ZBQ // LANES + hp)),
                  pl.BlockSpec((s, LANES), lambda b, hp: (b, ZBK // LANES + hp // pairs_per_kv)),
                  pl.BlockSpec((s, LANES), lambda b, hp: (b, ZBV // LANES + hp // pairs_per_kv)),
                  pl.BlockSpec((2, nv, BLOCK, 3 * BLOCK), lambda b, hp: (hp, 0, 0, 0))],
        out_specs=pl.BlockSpec((s, LANES), lambda b, hp: (b, hp)),
        out_shape=jax.ShapeDtypeStruct((bsz * s, WB_HEADS * HEAD_DIM), BF16),
        scratch_shapes=[pltpu.VMEM((s + 2 * BLOCK, LANES), BF16), pltpu.VMEM((s + 2 * BLOCK, LANES), BF16)],
        compiler_params=_cparams(2),
        name="window_attention",
    )(sink, z, z, z, bias)


def _nattn_kernel(n_edge, n_pairs, q_ref, k_ref, v_ref, bias_ref, o_ref):
    lo = _head_masks()
    zero = jnp.zeros((), BF16)
    qn, kn = NA_QROWS * GRID_W, NA_KROWS * GRID_W
    max_ks = k_ref.shape[0] // GRID_W - NA_KROWS

    def group(g, carry):
        rows, scores = [], []
        for u in range(ATTN_UNROLL):
            i = g * ATTN_UNROLL + u
            r = pl.multiple_of(i * qn, qn)
            ks = jnp.clip(NA_QROWS * i - NA_KH // 2, 0, max_ks)
            rk = pl.multiple_of(ks * GRID_W, LANES)
            var = jnp.where(i < n_edge, i,
                            jnp.where(i >= n_pairs - n_edge, i - (n_pairs - 2 * n_edge - 1), n_edge))
            q = q_ref[pl.ds(r, qn), :]
            kw = k_ref[pl.ds(rk, kn), :]
            rows.append((r, rk))
            for hh in range(2):
                qm = jnp.where(lo if hh == 0 else jnp.logical_not(lo), q, zero)
                scores.append(_dot_nt(qm, kw) + bias_ref[0, var, hh])
        for u, (r, rk) in enumerate(rows):
            vw = v_ref[pl.ds(rk, kn), :]
            outs = [_softmax_pv(scores[2 * u + hh], vw) for hh in range(2)]
            o_ref[pl.ds(r, qn), :] = jnp.where(lo, outs[0], outs[1]).astype(BF16)
        return carry

    lax.fori_loop(0, n_pairs // ATTN_UNROLL, group, 0)


def _neighbourhood_attention(z, bsz, s, bias, layer):
    rows = s // GRID_W
    _, variants, variant_of_pair, kstart = _na_tables(rows)
    n_pairs = rows // NA_QROWS
    nv = len(variants)
    n_edge = (nv - 1) // 2
    assert n_pairs % ATTN_UNROLL == 0
    for i in range(n_pairs):
        want = i if i < n_edge else (i - (n_pairs - 2 * n_edge - 1) if i >= n_pairs - n_edge else n_edge)
        assert variant_of_pair[i] == want
        assert kstart[i] == int(np.clip(NA_QROWS * i - NA_KH // 2, 0, rows - NA_KROWS))
        assert (kstart[i] * GRID_W) % LANES == 0
    npair = NA_HEADS // 2
    qn, kn = NA_QROWS * GRID_W, NA_KROWS * GRID_W
    return pl.pallas_call(
        functools.partial(_nattn_kernel, n_edge, n_pairs),
        grid=(npair, bsz),
        in_specs=[pl.BlockSpec((s, LANES), lambda hp, b: (b, ZCQ // LANES + hp)),
                  pl.BlockSpec((s, LANES), lambda hp, b: (b, ZCK // LANES + hp)),
                  pl.BlockSpec((s, LANES), lambda hp, b: (b, ZCV // LANES + hp)),
                  pl.BlockSpec((1, nv, 2, qn, kn), lambda hp, b: (layer, 0, hp, 0, 0))],
        out_specs=pl.BlockSpec((s, LANES), lambda hp, b: (b, hp)),
        out_shape=jax.ShapeDtypeStruct((bsz * s, NA_HEADS * HEAD_DIM), BF16),
        compiler_params=_cparams(2),
        name="neighbourhood_attention",
    )(z, z, z, bias)


ROUTE_IDX_LANE = N_EXPERTS
ROUTE_W_LANE = N_EXPERTS + 2


def _merge_kernel(with_router, u_ref, ob_ref, oc_ref, g_ref, x_ref, wa_ref, wb_ref, wc_ref, wo_ref,
                  fg_ref, *rest):
    if with_router:
        wrh_ref, wrl_ref, br_ref, xo_ref, hf_ref, slab_ref = rest
    else:
        xo_ref, hf_ref = rest
    d = D_MODEL
    m = g_ref[:, 0:d].astype(F32) * _dot(u_ref[...], wa_ref[...])
    m = m + g_ref[:, d:2 * d].astype(F32) * _dot(ob_ref[...], wb_ref[...])
    m = m + g_ref[:, 2 * d:3 * d].astype(F32) * _dot(oc_ref[...], wc_ref[...])
    xn = x_ref[...] + _dot(m.astype(BF16), wo_ref[...])
    xo_ref[...] = xn
    hf = xn * lax.rsqrt(jnp.mean(xn * xn, axis=-1, keepdims=True) + EPS) * fg_ref[...]
    hf_ref[...] = hf.astype(hf_ref.dtype)
    if with_router:
        hf_hi = hf.astype(BF16)
        hf_lo = (hf - hf_hi.astype(F32)).astype(BF16)
        logits = (_dot(hf_hi, wrh_ref[...]) + _dot(hf_lo, wrh_ref[...]) + _dot(hf_hi, wrl_ref[...])
                  + br_ref[...])
        lane = lax.broadcasted_iota(jnp.int32, logits.shape, 1)
        m1 = jnp.max(logits, axis=-1, keepdims=True)
        i1 = jnp.min(jnp.where(logits == m1, lane, LANES), axis=-1, keepdims=True)
        rest_l = jnp.where(lane == i1, -jnp.inf, logits)
        m2 = jnp.max(rest_l, axis=-1, keepdims=True)
        i2 = jnp.min(jnp.where(rest_l == m2, lane, LANES), axis=-1, keepdims=True)
        e2 = jnp.exp(m2 - m1)
        w1 = 1.0 / (1.0 + e2)
        w2 = e2 / (1.0 + e2)
        slab = jnp.where(lane == ROUTE_IDX_LANE, i1.astype(F32), 0.0)
        slab = jnp.where(lane == ROUTE_IDX_LANE + 1, i2.astype(F32), slab)
        slab = jnp.where(lane == ROUTE_W_LANE, w1, slab)
        slab = jnp.where(lane == ROUTE_W_LANE + 1, w2, slab)
        slab_ref[...] = slab


def _merge(u, ob, oc, z, x, wa, wb, wc, wo, fg, router=None, tm=512):
    n = x.shape[0]
    with_router = router is not None
    row = lambda w: pl.BlockSpec((tm, w), lambda i: (i, 0))
    in_specs = [row(CONV_CH), row(WB_HEADS * HEAD_DIM), row(NA_HEADS * HEAD_DIM),
                pl.BlockSpec((tm, ZA), lambda i: (i, 0)), row(D_MODEL),
                _resident(wa.shape), _resident(wb.shape), _resident(wc.shape), _resident(wo.shape),
                _resident((1, D_MODEL))]
    args = [u, ob, oc, z, x, wa, wb, wc, wo, fg]
    out_specs = [row(D_MODEL), row(D_MODEL)]
    out_shape = [jax.ShapeDtypeStruct((n, D_MODEL), F32),
                 jax.ShapeDtypeStruct((n, D_MODEL), F32 if with_router else BF16)]
    if with_router:
        in_specs += [_resident((D_MODEL, LANES)), _resident((D_MODEL, LANES)), _resident((1, LANES))]
        args += list(router)
        out_specs.append(row(LANES))
        out_shape.append(jax.ShapeDtypeStruct((n, LANES), F32))
    return pl.pallas_call(
        functools.partial(_merge_kernel, with_router),
        grid=(n // tm,),
        in_specs=in_specs, out_specs=out_specs, out_shape=out_shape,
        compiler_params=_cparams(1),
        name="merge_router" if with_router else "merge",
    )(*args)


FFN_CHUNK = 1024


def _ffn_kernel(hf_ref, x_ref, w1_ref, w3_ref, w2_ref, o_ref):
    dff = w1_ref.shape[1]
    hf = hf_ref[...]
    acc = x_ref[...]
    for c0 in range(0, dff, FFN_CHUNK):
        c1 = min(c0 + FFN_CHUNK, dff)
        a = _dot(hf, w1_ref[:, c0:c1])
        b = _dot(hf, w3_ref[:, c0:c1])
        hm = (a * jax.nn.sigmoid(a) * b).astype(BF16)
        acc = acc + _dot(hm, w2_ref[c0:c1, :])
    o_ref[...] = acc


def _dense_ffn(hf, x, w1, w3, w2, tm=512):
    n = x.shape[0]
    row = pl.BlockSpec((tm, D_MODEL), lambda i: (i, 0))
    return pl.pallas_call(
        _ffn_kernel,
        grid=(n // tm,),
        in_specs=[row, row, _resident(w1.shape), _resident(w3.shape), _resident(w2.shape)],
        out_specs=row,
        out_shape=jax.ShapeDtypeStruct((n, D_MODEL), F32),
        input_output_aliases={1: 0},
        compiler_params=_cparams(1),
        name="dense_ffn",
    )(hf, x, w1, w3, w2)


MOE_TILE = 1024
MOE_FF_CHUNK = 1792
ROUTE_ROWS = 512


def _row_copy(src, dst, sem):
    return pltpu.make_async_copy(src, dst, sem)


def _dispatch_kernel(dest_ref, pad_base_ref, pad_cnt_ref, hf_ref, xs_ref, buf_ref, zrow_ref, load_sem,
                     row_sem, zero_sem, zero_tile_sem):
    i = pl.program_id(0)
    last = pl.num_programs(0) - 1
    slot = i % 2

    def load(step, slot_):
        src = hf_ref.at[pl.ds(pl.multiple_of(step * ROUTE_ROWS, ROUTE_ROWS), ROUTE_ROWS)]
        return pltpu.make_async_copy(src, buf_ref.at[slot_], load_sem.at[slot_])

    def drain(slot_):
        for _ in range(2 * ROUTE_ROWS):
            _row_copy(buf_ref.at[slot_, pl.ds(0, 1)], xs_ref.at[pl.ds(0, 1)], row_sem.at[slot_]).wait()

    def zero_row(dst_row):
        return _row_copy(zrow_ref.at[pl.ds(0, 1)], xs_ref.at[pl.ds(dst_row, 1)], zero_sem)

    def zero_rows(dst_row):
        dst = xs_ref.at[pl.ds(pl.multiple_of(dst_row, SUBLANES), SUBLANES)]
        return _row_copy(zrow_ref, dst, zero_tile_sem)

    unused = N_EXPERTS

    @pl.when(i == 0)
    def _():
        load(0, 0).start()
        zrow_ref[...] = jnp.zeros_like(zrow_ref)
        for e in range(N_EXPERTS):
            def zstart(j, carry):
                zero_row(pad_base_ref[e] + j).start()
                return carry

            lax.fori_loop(0, pad_cnt_ref[e], zstart, 0)

        def ztile(j, carry):
            zero_rows(pad_base_ref[unused] + j * SUBLANES).start()
            return carry

        lax.fori_loop(0, pad_cnt_ref[unused] // SUBLANES, ztile, 0)

    load(i, slot).wait()

    @pl.when(i > 0)
    def _():
        drain(1 - slot)

    @pl.when(i < last)
    def _():
        load(i + 1, 1 - slot).start()

    def start(r, carry):
        for k in range(2):
            d = dest_ref[0, 0, 2 * r + k]
            _row_copy(buf_ref.at[slot, pl.ds(r, 1)], xs_ref.at[pl.ds(d, 1)], row_sem.at[slot]).start(priority=k)
        return carry

    lax.fori_loop(0, ROUTE_ROWS, start, 0, unroll=8)

    @pl.when(i == last)
    def _():
        drain(slot)
        for e in range(N_EXPERTS):
            def zwait(j, carry):
                zero_row(0).wait()
                return carry

            lax.fori_loop(0, pad_cnt_ref[e], zwait, 0)

        def ztile_wait(j, carry):
            zero_rows(0).wait()
            return carry

        lax.fori_loop(0, pad_cnt_ref[unused] // SUBLANES, ztile_wait, 0)


def _dispatch(hf, dest, pad_base, pad_cnt, n_rows):
    n = hf.shape[0]
    steps = n // ROUTE_ROWS
    smem = pl.BlockSpec(memory_space=pltpu.SMEM)
    return pl.pallas_call(
        _dispatch_kernel,
        grid=(steps,),
        in_specs=[pl.BlockSpec((1, 1, 2 * ROUTE_ROWS), lambda i: (i, 0, 0), memory_space=pltpu.SMEM),
                  smem, smem, pl.BlockSpec(memory_space=pl.ANY)],
        out_specs=pl.BlockSpec(memory_space=pl.ANY),
        out_shape=jax.ShapeDtypeStruct((n_rows, D_MODEL), F32),
        scratch_shapes=[pltpu.VMEM((2, ROUTE_ROWS, D_MODEL), F32), pltpu.VMEM((SUBLANES, D_MODEL), F32),
                        pltpu.SemaphoreType.DMA((2,)), pltpu.SemaphoreType.DMA((2,)),
                        pltpu.SemaphoreType.DMA(()), pltpu.SemaphoreType.DMA(())],
        compiler_params=_cparams(1),
        name="moe_dispatch",
    )(dest.reshape(steps, 1, 2 * ROUTE_ROWS), pad_base, pad_cnt, hf)


F8 = jnp.float8_e4m3fn
F32_EXP_BIAS = 127
F32_MANT_BITS = 23
F8_TOP_EXP = 7


def _pow2_scale(amax):
    e = jnp.right_shift(lax.bitcast_convert_type(amax, jnp.int32), F32_MANT_BITS) & 0xFF
    se = jnp.clip(2 * F32_EXP_BIAS + F8_TOP_EXP - e, 1, 2 * F32_EXP_BIAS)
    scale = lax.bitcast_convert_type(jnp.left_shift(se, F32_MANT_BITS), F32)
    inv = lax.bitcast_convert_type(jnp.left_shift(2 * F32_EXP_BIAS - se, F32_MANT_BITS), F32)
    return scale, inv


def _quantize_rows(v):
    scale, inv = _pow2_scale(jnp.max(jnp.abs(v), axis=-1, keepdims=True))
    return (v * scale).astype(F8), inv


def _experts_kernel(te_ref, na_ref, winv_ref, xs_ref, w1_ref, w3_ref, w2_ref, ys_ref, xb_ref, xinv_ref):
    i = pl.program_id(0)
    c = pl.program_id(1)
    active = i < na_ref[0]
    e = te_ref[i]

    @pl.when(c == 0)
    def _():
        ys_ref[...] = jnp.zeros_like(ys_ref)

    @pl.when(jnp.logical_and(active, c == 0))
    def _():
        xb_ref[...], xinv_ref[...] = _quantize_rows(xs_ref[...])

    @pl.when(active)
    def _():
        half = MOE_TILE // 2
        gates = []
        for h in range(2):
            rows = slice(h * half, (h + 1) * half)
            xb = xb_ref[rows, :]
            xinv = xinv_ref[rows, :]
            gates.append((_dot(xb, w1_ref[0]) * (xinv * winv_ref[e, 0]),
                          _dot(xb, w3_ref[0]) * (xinv * winv_ref[e, 1])))
        for h, (a, b) in enumerate(gates):
            hm, hinv = _quantize_rows(a * jax.nn.sigmoid(a) * b)
            ys_ref[h * half:(h + 1) * half, :] += _dot(hm, w2_ref[0]) * (hinv * winv_ref[e, 2])


def _quantize_expert_weights(w):
    scale, inv = _pow2_scale(jnp.max(jnp.abs(w.astype(F32)), axis=(1, 2)))
    return (w * scale[:, None, None]).astype(F8), inv


def _experts(xs, tile_expert, n_active, w1, w3, w2):
    n_rows = xs.shape[0]
    (w1, i1), (w3, i3), (w2, i2) = (_quantize_expert_weights(w) for w in (w1, w3, w2))
    winv = jnp.stack([i1, i3, i2], axis=1)
    dff = w1.shape[2]
    n_chunks = dff // MOE_FF_CHUNK

    def chunk(i, c, na):
        return jnp.where(i < na[0], c, n_chunks - 1)

    grid_spec = pltpu.PrefetchScalarGridSpec(
        num_scalar_prefetch=2,
        grid=(n_rows // MOE_TILE, dff // MOE_FF_CHUNK),
        in_specs=[pl.BlockSpec(memory_space=pltpu.SMEM),
                  pl.BlockSpec((MOE_TILE, D_MODEL), lambda i, c, te, na: (jnp.minimum(i, na[0] - 1), 0)),
                  pl.BlockSpec((1, D_MODEL, MOE_FF_CHUNK), lambda i, c, te, na: (te[i], 0, chunk(i, c, na))),
                  pl.BlockSpec((1, D_MODEL, MOE_FF_CHUNK), lambda i, c, te, na: (te[i], 0, chunk(i, c, na))),
                  pl.BlockSpec((1, MOE_FF_CHUNK, D_MODEL), lambda i, c, te, na: (te[i], chunk(i, c, na), 0))],
        out_specs=pl.BlockSpec((MOE_TILE, D_MODEL), lambda i, c, te, na: (i, 0)),
        scratch_shapes=[pltpu.VMEM((MOE_TILE, D_MODEL), F8), pltpu.VMEM((MOE_TILE, 1), F32)],
    )
    return pl.pallas_call(
        _experts_kernel,
        grid_spec=grid_spec,
        out_shape=jax.ShapeDtypeStruct((n_rows, D_MODEL), F32),
        compiler_params=_cparams(2),
        name="moe_experts",
    )(tile_expert, n_active, winv, xs, w1, w3, w2)


def _combine_kernel(dest_ref, next_dest_ref, x_ref, slab_ref, ys_ref, o_ref, y_ref, sem):
    i = pl.program_id(0)
    slot = i % 2

    def gather(d_ref, slot_):
        def start(r, carry):
            for k in range(2):
                d = d_ref[0, 0, 2 * r + k]
                _row_copy(ys_ref.at[pl.ds(d, 1)], y_ref.at[slot_, k, pl.ds(r, 1)], sem.at[slot_]).start()
            return carry

        lax.fori_loop(0, ROUTE_ROWS, start, 0, unroll=8)

    @pl.when(i == 0)
    def _():
        gather(dest_ref, 0)

    @pl.when(i + 1 < pl.num_programs(0))
    def _():
        gather(next_dest_ref, 1 - slot)

    for _ in range(2 * ROUTE_ROWS):
        _row_copy(ys_ref.at[pl.ds(0, 1)], y_ref.at[slot, 0, pl.ds(0, 1)], sem.at[slot]).wait()
    w0 = slab_ref[:, ROUTE_W_LANE:ROUTE_W_LANE + 1]
    w1 = slab_ref[:, ROUTE_W_LANE + 1:ROUTE_W_LANE + 2]
    o_ref[...] = x_ref[...] + w0 * y_ref[slot, 0] + w1 * y_ref[slot, 1]


def _combine(x, slab, dest, ys):
    n = x.shape[0]
    steps = n // ROUTE_ROWS
    row = pl.BlockSpec((ROUTE_ROWS, D_MODEL), lambda i: (i, 0))
    dest3 = dest.reshape(steps, 1, 2 * ROUTE_ROWS)
    dest_block = (1, 1, 2 * ROUTE_ROWS)
    return pl.pallas_call(
        _combine_kernel,
        grid=(steps,),
        in_specs=[pl.BlockSpec(dest_block, lambda i: (i, 0, 0), memory_space=pltpu.SMEM),
                  pl.BlockSpec(dest_block, lambda i: (jnp.minimum(i + 1, steps - 1), 0, 0),
                               memory_space=pltpu.SMEM),
                  row, pl.BlockSpec((ROUTE_ROWS, LANES), lambda i: (i, 0)),
                  pl.BlockSpec(memory_space=pl.ANY)],
        out_specs=row,
        out_shape=jax.ShapeDtypeStruct((n, D_MODEL), F32),
        scratch_shapes=[pltpu.VMEM((2, 2, ROUTE_ROWS, D_MODEL), F32), pltpu.SemaphoreType.DMA((2,))],
        input_output_aliases={2: 0},
        compiler_params=_cparams(1),
        name="moe_combine",
    )(dest3, dest3, x, slab, ys)


def _routing_plan(slab, n_rows):
    n = slab.shape[0]
    idx = slab[:, ROUTE_IDX_LANE:ROUTE_IDX_LANE + 2].astype(jnp.int32)
    onehot = (idx[:, :, None] == jnp.arange(N_EXPERTS, dtype=jnp.int32)).astype(jnp.int32).sum(axis=1)
    csum = jnp.cumsum(onehot, axis=0)
    rank = csum - onehot
    counts = csum[-1]
    padded = ((counts + MOE_TILE - 1) // MOE_TILE) * MOE_TILE
    ends = jnp.cumsum(padded)
    offs = ends - padded
    dest = offs[idx] + jnp.take_along_axis(rank, idx, axis=1)
    tile_start = jnp.arange(n_rows // MOE_TILE, dtype=jnp.int32) * MOE_TILE
    tile_expert = jnp.minimum(jnp.sum(tile_start[:, None] >= ends[None, :], axis=1), N_EXPERTS - 1)
    n_active = (ends[-1] // MOE_TILE).reshape(1)
    i32 = lambda v: v.astype(jnp.int32)
    pad_base = jnp.concatenate([offs + counts, ends[-1:]])
    pad_cnt = jnp.concatenate([padded - counts, n_rows - ends[-1:]])
    return i32(dest.reshape(-1)), i32(tile_expert), i32(n_active), i32(pad_base), i32(pad_cnt)


def _moe(hf, x, slab, w1, w3, w2):
    n = x.shape[0]
    n_rows = 2 * n + N_EXPERTS * MOE_TILE
    dest, tile_expert, n_active, pad_base, pad_cnt = _routing_plan(slab, n_rows)
    xs = _dispatch(hf, dest, pad_base, pad_cnt, n_rows)
    ys = _experts(xs, tile_expert, n_active, w1, w3, w2)
    return _combine(x, slab, dest, ys)


def _prep_inproj(w_in_l, gate_b_l, qn_b, kn_b, qn_c, kn_c):
    a_in = 2 * CONV_CH
    b_q = WB_HEADS * HEAD_DIM
    b_kv = WB_KV_HEADS * HEAD_DIM
    c_w = NA_HEADS * HEAD_DIM
    o = 0
    wa = w_in_l[:, o:o + a_in]; o += a_in
    wbq = w_in_l[:, o:o + b_q]; o += b_q
    wbk = w_in_l[:, o:o + b_kv]; o += b_kv
    wbv = w_in_l[:, o:o + b_kv]; o += b_kv
    wcq = w_in_l[:, o:o + c_w]; o += c_w
    wck = w_in_l[:, o:o + c_w]; o += c_w
    wcv = w_in_l[:, o:o + c_w]; o += c_w
    wg = w_in_l[:, o:]

    def dup(w):
        return jnp.repeat(w.reshape(D_MODEL, WB_KV_HEADS, 1, HEAD_DIM), 2, axis=2).reshape(D_MODEL, -1)

    w = jnp.concatenate([wg, wa, wbq, dup(wbk), dup(wbv), wcq, wck, wcv], axis=1).astype(BF16)
    scale = HEAD_DIM ** -0.5
    ones = lambda k: jnp.ones((k,), F32)
    cg = jnp.concatenate([
        ones(ZBQ), jnp.tile(qn_b.astype(F32), WB_HEADS) * scale, jnp.tile(kn_b.astype(F32), 2 * WB_KV_HEADS),
        ones(ZCQ - ZBV), jnp.tile(qn_c.astype(F32), NA_HEADS) * scale, jnp.tile(kn_c.astype(F32), NA_HEADS),
        ones(ZC - ZCV)])
    return w, cg.reshape(1, ZC), gate_b_l.astype(F32).reshape(1, -1)


MXU_TILE = 256


def _segment_mean_matrix():
    idx = np.arange(MXU_TILE) // HEAD_DIM
    return jnp.asarray((idx[:, None] == idx[None, :]).astype(np.float32) / HEAD_DIM, dtype=BF16)


def kernel(x, t5_rel_bias, attn_norm_g, w_in, gate_b, conv_dw_w, conv_dw_b, conv_ln_g, conv_ln_b,
           conv_w_out, wb_qn_g, wb_kn_g, wb_sink, wb_w_out, na_qn_g, na_kn_g, na_rpb, na_w_out, w_o,
           ffn_norm_g, ffn_w1, ffn_w3, ffn_w2, moe_w_router, moe_b_router, moe_w1, moe_w3, moe_w2):
    bsz, s, d = x.shape
    depth = w_in.shape[0]
    assert d == D_MODEL and s % BLOCK == 0 and s % (NA_QROWS * GRID_W) == 0
    n = bsz * s
    xf = x.reshape(n, d).astype(F32)
    row = lambda v: v.astype(F32).reshape(1, -1)

    t5_bias = _t5_bias(t5_rel_bias)
    na_bias = _na_bias(na_rpb, s // GRID_W)
    bd = _segment_mean_matrix()

    for layer in range(depth):
        w, cg, gb = _prep_inproj(w_in[layer], gate_b[layer], wb_qn_g[layer], wb_kn_g[layer],
                                 na_qn_g[layer], na_kn_g[layer])
        z = _inproj(xf, row(attn_norm_g[layer]), w, cg, gb, bd)
        u = _conv_mixer(z, bsz, s, conv_dw_w[layer].astype(F32), row(conv_dw_b[layer]),
                        row(conv_ln_g[layer]), row(conv_ln_b[layer]))
        ob = _window_attention(z, bsz, s, row(wb_sink[layer]), t5_bias)
        oc = _neighbourhood_attention(z, bsz, s, na_bias, layer)
        weights = (conv_w_out[layer].astype(BF16), wb_w_out[layer].astype(BF16),
                   na_w_out[layer].astype(BF16), w_o[layer].astype(BF16), row(ffn_norm_g[layer]))
        i = layer // 2
        if layer % 2 == 0:
            xf, hf = _merge(u, ob, oc, z, xf, *weights)
            xf = _dense_ffn(hf, xf, ffn_w1[i].astype(BF16), ffn_w3[i].astype(BF16), ffn_w2[i].astype(BF16))
        else:
            wr = jnp.zeros((D_MODEL, LANES), F32).at[:, :N_EXPERTS].set(moe_w_router[i].astype(F32))
            br = jnp.full((1, LANES), -jnp.inf, F32).at[0, :N_EXPERTS].set(moe_b_router[i].astype(F32))
            wr_hi = wr.astype(BF16)
            wr_lo = (wr - wr_hi.astype(F32)).astype(BF16)
            xf, hf, slab = _merge(u, ob, oc, z, xf, *weights, router=(wr_hi, wr_lo, br))
            xf = _moe(hf, xf, slab, moe_w1[i], moe_w3[i], moe_w2[i])
    return xf.reshape(bsz, s, d).astype(x.dtype)
```

```python
import functools
import math

import numpy as np
import jax
import jax.numpy as jnp
from jax import lax
from jax.experimental import pallas as pl
from jax.experimental.pallas import tpu as pltpu

D_MODEL = 1024
HEAD_DIM = 64
CONV_CH = 512
CONV_WIDTH = 31
WB_HEADS = 8
WB_KV_HEADS = 2
WINDOW = 128
BLOCK = 128
NA_HEADS = 8
GRID_W = 64
NA_KH = 8
NA_KW = 16
REL_BUCKETS = 32
REL_MAX_DIST = 128
N_EXPERTS = 8
EPS = 1e-6
NEG_INF = -1e30

LANES = 128

ZG = 0
ZA = 3072
ZBQ = 4096
ZBK = 4608
ZBV = 4864
ZCQ = 5120
ZCK = 5632
ZCV = 6144
ZC = 6656
ZCHUNK = 512

VMEM_LIMIT = 56 * 1024 * 1024

F32 = jnp.float32
BF16 = jnp.bfloat16


def _cparams(grid_rank, vmem=VMEM_LIMIT):
    return pltpu.CompilerParams(dimension_semantics=("arbitrary",) * grid_rank, vmem_limit_bytes=vmem)


def _resident(shape):
    nd = len(shape)
    return pl.BlockSpec(shape, lambda *_: (0,) * nd, pipeline_mode=pl.Buffered(1))


def _dot(a, b):
    return jnp.dot(a, b, preferred_element_type=F32)


def _dot_nt(a, b):
    return lax.dot_general(a, b, (((1,), (1,)), ((), ())), preferred_element_type=F32)


def _t5_bucket_np(rel):
    half = REL_BUCKETS // 2
    max_exact = half // 2
    ret = np.where(rel > 0, half, 0)
    n = np.abs(rel)
    nf = np.maximum(n, 1).astype(np.float32)
    large = max_exact + (np.log(nf / np.float32(max_exact)) / np.float32(math.log(REL_MAX_DIST / max_exact))
                         * np.float32(half - max_exact)).astype(np.int32)
    large = np.minimum(large, half - 1)
    return ret + np.where(n < max_exact, n, large)


def _window_bucket_table():
    col = np.arange(3 * BLOCK)[None, :]
    rel = (col - BLOCK) - np.arange(BLOCK)[:, None]
    band = np.where(np.abs(rel) <= WINDOW, _t5_bucket_np(rel), -1)
    first = np.where(col >= BLOCK, band, -1)
    last = np.where(col < 2 * BLOCK, band, -1)
    return np.stack([first, band, last]).astype(np.int32)


NA_QROWS = 2
NA_KROWS = 10


def _na_tables(rows):
    kh = min(NA_KH, rows)
    col = np.arange(GRID_W)
    col_start = np.clip(col - NA_KW // 2, 0, GRID_W - NA_KW)
    kc = np.arange(GRID_W)[None, :]
    inwin = (kc >= col_start[:, None]) & (kc < col_start[:, None] + NA_KW)
    dmat = np.where(inwin, kc - col[:, None] + NA_KW - 1, -1).astype(np.int32)
    variants, variant_of_pair, kstart = [], [], []
    for i in range(rows // NA_QROWS):
        ks = int(np.clip(NA_QROWS * i - kh // 2, 0, rows - NA_KROWS))
        tab = np.full((NA_QROWS, NA_KROWS), -1, np.int64)
        for a in range(NA_QROWS):
            r = NA_QROWS * i + a
            r0 = int(np.clip(r - kh // 2, 0, rows - kh))
            assert ks <= r0 and r0 + kh <= ks + NA_KROWS
            for kr in range(NA_KROWS):
                if r0 <= ks + kr < r0 + kh:
                    tab[a, kr] = ks + kr - r + NA_KH - 1
        for v, t in enumerate(variants):
            if np.array_equal(t, tab):
                variant_of_pair.append(v)
                break
        else:
            variant_of_pair.append(len(variants))
            variants.append(tab)
        kstart.append(ks)
    return dmat, variants, variant_of_pair, kstart


def _t5_bias_kernel(tab_ref, bucket_ref, o_ref):
    h = pl.program_id(0)
    for v in range(bucket_ref.shape[0]):
        bucket = bucket_ref[v]
        acc = jnp.full(bucket.shape, NEG_INF, F32)
        for b in range(REL_BUCKETS):
            acc = jnp.where(bucket == b, tab_ref[b, h], acc)
        o_ref[0, v] = acc


def _t5_bias(t5_rel_bias):
    bucket = jnp.asarray(_window_bucket_table())
    nv = bucket.shape[0]
    return pl.pallas_call(
        _t5_bias_kernel,
        grid=(WB_HEADS,),
        in_specs=[pl.BlockSpec(memory_space=pltpu.SMEM),
                  pl.BlockSpec((nv, BLOCK, 3 * BLOCK), lambda h: (0, 0, 0))],
        out_specs=pl.BlockSpec((1, nv, BLOCK, 3 * BLOCK), lambda h: (h, 0, 0, 0)),
        out_shape=jax.ShapeDtypeStruct((WB_HEADS, nv, BLOCK, 3 * BLOCK), F32),
        compiler_params=_cparams(1),
        name="t5_bias",
    )(t5_rel_bias.astype(F32), bucket)


def _na_bias_kernel(variants, rpb_ref, dmat_ref, o_ref):
    lh = pl.program_id(0) * NA_HEADS + pl.program_id(1)
    dmat = dmat_ref[...]
    ncol = 2 * NA_KW - 1
    base = []
    for dr in range(2 * NA_KH - 1):
        acc = jnp.full(dmat.shape, NEG_INF, F32)
        for d in range(ncol):
            acc = jnp.where(dmat == d, rpb_ref[lh, dr * ncol + d], acc)
        base.append(acc)
    neg = jnp.full(dmat.shape, NEG_INF, F32)
    for v, tab in enumerate(variants):
        for a in range(NA_QROWS):
            for kr in range(NA_KROWS):
                dr = int(tab[a, kr])
                tile = neg if dr < 0 else base[dr]
                o_ref[0, v, 0, a * GRID_W:(a + 1) * GRID_W, kr * GRID_W:(kr + 1) * GRID_W] = tile


def _na_bias(na_rpb, rows):
    depth = na_rpb.shape[0]
    dmat, variants, _, _ = _na_tables(rows)
    nv = len(variants)
    rpb = jnp.transpose(na_rpb.astype(F32), (0, 3, 1, 2)).reshape(depth * NA_HEADS, -1)
    qn, kn = NA_QROWS * GRID_W, NA_KROWS * GRID_W
    return pl.pallas_call(
        functools.partial(_na_bias_kernel, variants),
        grid=(depth, NA_HEADS),
        in_specs=[pl.BlockSpec(memory_space=pltpu.SMEM),
                  pl.BlockSpec((GRID_W, GRID_W), lambda l, h: (0, 0))],
        out_specs=pl.BlockSpec((1, nv, 1, qn, kn), lambda l, h: (l, 0, h, 0, 0)),
        out_shape=jax.ShapeDtypeStruct((depth, nv, NA_HEADS, qn, kn), F32),
        compiler_params=_cparams(2),
        name="na_bias",
    )(rpb, jnp.asarray(dmat))


def _inproj_kernel(x_ref, ng_ref, w_ref, cg_ref, gb_ref, bd_ref, z_ref, h_ref):
    x = x_ref[...]
    inv = lax.rsqrt(jnp.mean(x * x, axis=-1, keepdims=True) + EPS)
    h_ref[...] = (x * inv * ng_ref[...]).astype(BF16)

    def qknorm(zc, c0):
        w = bd_ref.shape[0]
        sq = (zc * zc).astype(BF16)
        ms = jnp.concatenate([_dot(sq[:, j:j + w], bd_ref[...]) for j in range(0, zc.shape[1], w)], axis=1)
        return zc * lax.rsqrt(ms + EPS) * cg_ref[:, c0:c0 + zc.shape[1]]

    for c0 in range(0, ZC, ZCHUNK):
        zc = _dot(h_ref[...], w_ref[:, c0:c0 + ZCHUNK])
        if c0 < ZA:
            out = jax.nn.sigmoid(zc + gb_ref[:, c0:c0 + ZCHUNK])
        elif c0 in (ZBQ, ZCQ, ZCK):
            out = qknorm(zc, c0)
        elif c0 == ZBK:
            half = ZBV - ZBK
            out = jnp.concatenate([qknorm(zc[:, :half], c0), zc[:, half:]], axis=1)
        else:
            out = zc
        z_ref[:, c0:c0 + ZCHUNK] = out.astype(BF16)


def _inproj(x, ng, w, cg, gb, bd, tm=512):
    n = x.shape[0]
    return pl.pallas_call(
        _inproj_kernel,
        grid=(n // tm,),
        in_specs=[pl.BlockSpec((tm, D_MODEL), lambda i: (i, 0)),
                  _resident((1, D_MODEL)), _resident((D_MODEL, ZC)), _resident((1, ZC)),
                  _resident((1, ZA)), _resident(bd.shape)],
        out_specs=pl.BlockSpec((tm, ZC), lambda i: (i, 0)),
        out_shape=jax.ShapeDtypeStruct((n, ZC), BF16),
        scratch_shapes=[pltpu.VMEM((tm, D_MODEL), BF16)],
        compiler_params=_cparams(1),
        name="inproj",
    )(x, ng, w, cg, gb, bd)


SUBLANES = 8
CONV_PAD = 16
CONV_HALF = 256
CONV_ROWS = 64
CONV_SHIFT_ROWS = 104
CONV_NORM_TILES = 4


def _conv_kernel(z_ref, dw_ref, db_ref, lg_ref, lb_ref, u_ref, vs_ref, c_ref):
    s = z_ref.shape[0]
    span = s + 2 * CONV_PAD
    first = CONV_PAD - CONV_WIDTH // 2
    glu_rows = 256

    for half in range(CONV_CH // CONV_HALF):
        l0 = half * CONV_HALF
        vs_ref[0, 0:CONV_PAD, :] = jnp.zeros((CONV_PAD, CONV_HALF), F32)
        vs_ref[0, CONV_PAD + s:span + SUBLANES, :] = jnp.zeros((CONV_PAD + SUBLANES, CONV_HALF), F32)

        def glu(i, carry):
            r = pl.multiple_of(i * glu_rows, glu_rows)
            a = z_ref[pl.ds(r, glu_rows), l0:l0 + CONV_HALF].astype(F32)
            g = z_ref[pl.ds(r, glu_rows), CONV_CH + l0:CONV_CH + l0 + CONV_HALF].astype(F32)
            vs_ref[0, pl.ds(CONV_PAD + r, glu_rows), :] = a * jax.nn.sigmoid(g)
            return carry

        lax.fori_loop(0, s // glu_rows, glu, 0)

        def shift(i, carry):
            r = pl.multiple_of(i * CONV_SHIFT_ROWS, SUBLANES)
            win = vs_ref[0, pl.ds(r, CONV_SHIFT_ROWS + SUBLANES), :]
            for k in range(1, SUBLANES):
                vs_ref[k, pl.ds(r, CONV_SHIFT_ROWS), :] = win[k:k + CONV_SHIFT_ROWS]
            return carry

        assert span % CONV_SHIFT_ROWS == 0
        lax.fori_loop(0, span // CONV_SHIFT_ROWS, shift, 0)

        def conv(i, carry):
            r = pl.multiple_of(i * CONV_ROWS, CONV_ROWS)
            acc = jnp.zeros((CONV_ROWS, CONV_HALF), F32) + db_ref[:, l0:l0 + CONV_HALF]
            for w in range(CONV_WIDTH):
                k, a = (first + w) % SUBLANES, (first + w) // SUBLANES
                tap = vs_ref[k, pl.ds(r + a * SUBLANES, CONV_ROWS), :]
                acc = acc + tap * dw_ref[w:w + 1, l0:l0 + CONV_HALF]
            c_ref[pl.ds(r, CONV_ROWS), l0:l0 + CONV_HALF] = acc
            return carry

        lax.fori_loop(0, s // CONV_ROWS, conv, 0)

    def norm(i, carry):
        for t in range(CONV_NORM_TILES):
            r = pl.multiple_of((i * CONV_NORM_TILES + t) * CONV_ROWS, CONV_ROWS)
            c = c_ref[pl.ds(r, CONV_ROWS), :]
            xc = c - jnp.mean(c, axis=-1, keepdims=True)
            var = jnp.mean(xc * xc, axis=-1, keepdims=True)
            y = xc * lax.rsqrt(var + EPS) * lg_ref[...] + lb_ref[...]
            u_ref[pl.ds(r, CONV_ROWS), :] = (y * jax.nn.sigmoid(y)).astype(BF16)
        return carry

    assert s % (CONV_ROWS * CONV_NORM_TILES) == 0
    lax.fori_loop(0, s // (CONV_ROWS * CONV_NORM_TILES), norm, 0)


def _conv_mixer(z, bsz, s, dw, db, lg, lb):
    return pl.pallas_call(
        _conv_kernel,
        grid=(bsz,),
        in_specs=[pl.BlockSpec((s, 2 * CONV_CH), lambda b: (b, ZA // (2 * CONV_CH))),
                  _resident((CONV_WIDTH, CONV_CH)), _resident((1, CONV_CH)),
                  _resident((1, CONV_CH)), _resident((1, CONV_CH))],
        out_specs=pl.BlockSpec((s, CONV_CH), lambda b: (b, 0)),
        out_shape=jax.ShapeDtypeStruct((bsz * s, CONV_CH), BF16),
        scratch_shapes=[pltpu.VMEM((SUBLANES, s + 2 * CONV_PAD + SUBLANES, CONV_HALF), F32),
                        pltpu.VMEM((s, CONV_CH), F32)],
        compiler_params=_cparams(1),
        name="conv_mixer",
    )(z, dw, db, lg, lb)


def _head_masks():
    lane = lax.broadcasted_iota(jnp.int32, (1, LANES), 1)
    return lane < HEAD_DIM


ATTN_UNROLL = 8


def _softmax_pv(sc, vw, sink=None):
    m = jnp.max(sc, axis=-1, keepdims=True)
    if sink is not None:
        m = jnp.maximum(m, sink)
    p = jnp.exp(sc - m)
    l = jnp.sum(p, axis=-1, keepdims=True)
    if sink is not None:
        l = l + jnp.exp(sink - m)
    return _dot(p.astype(BF16), vw) * (1.0 / l)


def _wattn_kernel(pairs_per_kv, sink_ref, q_ref, k_ref, v_ref, bias_ref, o_ref, kp_ref, vp_ref):
    hp = pl.program_id(1)
    s = q_ref.shape[0]
    nb = s // BLOCK
    lo = _head_masks()
    zero = jnp.zeros((), BF16)

    @pl.when(hp % pairs_per_kv == 0)
    def _():
        pad = jnp.zeros((BLOCK, LANES), BF16)
        for src, dst in ((k_ref, kp_ref), (v_ref, vp_ref)):
            dst[0:BLOCK, :] = pad
            dst[BLOCK:BLOCK + s, :] = src[...]
            dst[BLOCK + s:2 * BLOCK + s, :] = pad

    def group(g, carry):
        rows, scores = [], []
        for u in range(ATTN_UNROLL):
            n = g * ATTN_UNROLL + u
            r = pl.multiple_of(n * BLOCK, BLOCK)
            var = jnp.where(n == 0, 0, jnp.where(n == nb - 1, 2, 1))
            q = q_ref[pl.ds(r, BLOCK), :]
            kw = kp_ref[pl.ds(r, 3 * BLOCK), :]
            rows.append(r)
            for hh in range(2):
                qm = jnp.where(lo if hh == 0 else jnp.logical_not(lo), q, zero)
                scores.append(_dot_nt(qm, kw) + bias_ref[hh, var])
        for u, r in enumerate(rows):
            vw = vp_ref[pl.ds(r, 3 * BLOCK), :]
            outs = [_softmax_pv(scores[2 * u + hh], vw, sink_ref[0, 2 * hp + hh]) for hh in range(2)]
            o_ref[pl.ds(r, BLOCK), :] = jnp.where(lo, outs[0], outs[1]).astype(BF16)
        return carry

    lax.fori_loop(0, nb // ATTN_UNROLL, group, 0)


def _window_attention(z, bsz, s, sink, bias):
    npair = WB_HEADS // 2
    pairs_per_kv = npair // WB_KV_HEADS
    nv = bias.shape[1]
    assert (s // BLOCK) % ATTN_UNROLL == 0 and s // BLOCK >= 2
    return pl.pallas_call(
        functools.partial(_wattn_kernel, pairs_per_kv),
        grid=(bsz, npair),
        in_specs=[pl.BlockSpec(memory_space=pltpu.SMEM),
                  pl.BlockSpec((s, LANES), lambda b, hp: (b, ZBQ // LANES + hp)),
                  pl.BlockSpec((s, LANES), lambda b, hp: (b, ZBK // LANES + hp // pairs_per_kv)),
                  pl.BlockSpec((s, LANES), lambda b, hp: (b, ZBV // LANES + hp // pairs_per_kv)),
                  pl.BlockSpec((2, nv, BLOCK, 3 * BLOCK), lambda b, hp: (hp, 0, 0, 0))],
        out_specs=pl.BlockSpec((s, LANES), lambda b, hp: (b, hp)),
        out_shape=jax.ShapeDtypeStruct((bsz * s, WB_HEADS * HEAD_DIM), BF16),
        scratch_shapes=[pltpu.VMEM((s + 2 * BLOCK, LANES), BF16), pltpu.VMEM((s + 2 * BLOCK, LANES), BF16)],
        compiler_params=_cparams(2),
        name="window_attention",
    )(sink, z, z, z, bias)


def _nattn_kernel(n_edge, n_pairs, q_ref, k_ref, v_ref, bias_ref, o_ref):
    lo = _head_masks()
    zero = jnp.zeros((), BF16)
    qn, kn = NA_QROWS * GRID_W, NA_KROWS * GRID_W
    max_ks = k_ref.shape[0] // GRID_W - NA_KROWS

    def group(g, carry):
        rows, scores = [], []
        for u in range(ATTN_UNROLL):
            i = g * ATTN_UNROLL + u
            r = pl.multiple_of(i * qn, qn)
            ks = jnp.clip(NA_QROWS * i - NA_KH // 2, 0, max_ks)
            rk = pl.multiple_of(ks * GRID_W, LANES)
            var = jnp.where(i < n_edge, i,
                            jnp.where(i >= n_pairs - n_edge, i - (n_pairs - 2 * n_edge - 1), n_edge))
            q = q_ref[pl.ds(r, qn), :]
            kw = k_ref[pl.ds(rk, kn), :]
            rows.append((r, rk))
            for hh in range(2):
                qm = jnp.where(lo if hh == 0 else jnp.logical_not(lo), q, zero)
                scores.append(_dot_nt(qm, kw) + bias_ref[0, var, hh])
        for u, (r, rk) in enumerate(rows):
            vw = v_ref[pl.ds(rk, kn), :]
            outs = [_softmax_pv(scores[2 * u + hh], vw) for hh in range(2)]
            o_ref[pl.ds(r, qn), :] = jnp.where(lo, outs[0], outs[1]).astype(BF16)
        return carry

    lax.fori_loop(0, n_pairs // ATTN_UNROLL, group, 0)


def _neighbourhood_attention(z, bsz, s, bias, layer):
    rows = s // GRID_W
    _, variants, variant_of_pair, kstart = _na_tables(rows)
    n_pairs = rows // NA_QROWS
    nv = len(variants)
    n_edge = (nv - 1) // 2
    assert n_pairs % ATTN_UNROLL == 0
    for i in range(n_pairs):
        want = i if i < n_edge else (i - (n_pairs - 2 * n_edge - 1) if i >= n_pairs - n_edge else n_edge)
        assert variant_of_pair[i] == want
        assert kstart[i] == int(np.clip(NA_QROWS * i - NA_KH // 2, 0, rows - NA_KROWS))
        assert (kstart[i] * GRID_W) % LANES == 0
    npair = NA_HEADS // 2
    qn, kn = NA_QROWS * GRID_W, NA_KROWS * GRID_W
    return pl.pallas_call(
        functools.partial(_nattn_kernel, n_edge, n_pairs),
        grid=(npair, bsz),
        in_specs=[pl.BlockSpec((s, LANES), lambda hp, b: (b, ZCQ // LANES + hp)),
                  pl.BlockSpec((s, LANES), lambda hp, b: (b, ZCK // LANES + hp)),
                  pl.BlockSpec((s, LANES), lambda hp, b: (b, ZCV // LANES + hp)),
                  pl.BlockSpec((1, nv, 2, qn, kn), lambda hp, b: (layer, 0, hp, 0, 0))],
        out_specs=pl.BlockSpec((s, LANES), lambda hp, b: (b, hp)),
        out_shape=jax.ShapeDtypeStruct((bsz * s, NA_HEADS * HEAD_DIM), BF16),
        compiler_params=_cparams(2),
        name="neighbourhood_attention",
    )(z, z, z, bias)


ROUTE_IDX_LANE = N_EXPERTS
ROUTE_W_LANE = N_EXPERTS + 2


def _merge_kernel(with_router, u_ref, ob_ref, oc_ref, g_ref, x_ref, wa_ref, wb_ref, wc_ref, wo_ref,
                  fg_ref, *rest):
    if with_router:
        wrh_ref, wrl_ref, br_ref, xo_ref, hf_ref, slab_ref = rest
    else:
        xo_ref, hf_ref = rest
    d = D_MODEL
    m = g_ref[:, 0:d].astype(F32) * _dot(u_ref[...], wa_ref[...])
    m = m + g_ref[:, d:2 * d].astype(F32) * _dot(ob_ref[...], wb_ref[...])
    m = m + g_ref[:, 2 * d:3 * d].astype(F32) * _dot(oc_ref[...], wc_ref[...])
    xn = x_ref[...] + _dot(m.astype(BF16), wo_ref[...])
    xo_ref[...] = xn
    hf = xn * lax.rsqrt(jnp.mean(xn * xn, axis=-1, keepdims=True) + EPS) * fg_ref[...]
    hf_ref[...] = hf.astype(hf_ref.dtype)
    if with_router:
        hf_hi = hf.astype(BF16)
        hf_lo = (hf - hf_hi.astype(F32)).astype(BF16)
        logits = (_dot(hf_hi, wrh_ref[...]) + _dot(hf_lo, wrh_ref[...]) + _dot(hf_hi, wrl_ref[...])
                  + br_ref[...])
        lane = lax.broadcasted_iota(jnp.int32, logits.shape, 1)
        m1 = jnp.max(logits, axis=-1, keepdims=True)
        i1 = jnp.min(jnp.where(logits == m1, lane, LANES), axis=-1, keepdims=True)
        rest_l = jnp.where(lane == i1, -jnp.inf, logits)
        m2 = jnp.max(rest_l, axis=-1, keepdims=True)
        i2 = jnp.min(jnp.where(rest_l == m2, lane, LANES), axis=-1, keepdims=True)
        e2 = jnp.exp(m2 - m1)
        w1 = 1.0 / (1.0 + e2)
        w2 = e2 / (1.0 + e2)
        slab = jnp.where(lane == ROUTE_IDX_LANE, i1.astype(F32), 0.0)
        slab = jnp.where(lane == ROUTE_IDX_LANE + 1, i2.astype(F32), slab)
        slab = jnp.where(lane == ROUTE_W_LANE, w1, slab)
        slab = jnp.where(lane == ROUTE_W_LANE + 1, w2, slab)
        slab_ref[...] = slab


def _merge(u, ob, oc, z, x, wa, wb, wc, wo, fg, router=None, tm=512):
    n = x.shape[0]
    with_router = router is not None
    row = lambda w: pl.BlockSpec((tm, w), lambda i: (i, 0))
    in_specs = [row(CONV_CH), row(WB_HEADS * HEAD_DIM), row(NA_HEADS * HEAD_DIM),
                pl.BlockSpec((tm, ZA), lambda i: (i, 0)), row(D_MODEL),
                _resident(wa.shape), _resident(wb.shape), _resident(wc.shape), _resident(wo.shape),
                _resident((1, D_MODEL))]
    args = [u, ob, oc, z, x, wa, wb, wc, wo, fg]
    out_specs = [row(D_MODEL), row(D_MODEL)]
    out_shape = [jax.ShapeDtypeStruct((n, D_MODEL), F32),
                 jax.ShapeDtypeStruct((n, D_MODEL), F32 if with_router else BF16)]
    if with_router:
        in_specs += [_resident((D_MODEL, LANES)), _resident((D_MODEL, LANES)), _resident((1, LANES))]
        args += list(router)
        out_specs.append(row(LANES))
        out_shape.append(jax.ShapeDtypeStruct((n, LANES), F32))
    return pl.pallas_call(
        functools.partial(_merge_kernel, with_router),
        grid=(n // tm,),
        in_specs=in_specs, out_specs=out_specs, out_shape=out_shape,
        compiler_params=_cparams(1),
        name="merge_router" if with_router else "merge",
    )(*args)


FFN_CHUNK = 1024


def _ffn_kernel(hf_ref, x_ref, w1_ref, w3_ref, w2_ref, o_ref):
    dff = w1_ref.shape[1]
    hf = hf_ref[...]
    acc = x_ref[...]
    for c0 in range(0, dff, FFN_CHUNK):
        c1 = min(c0 + FFN_CHUNK, dff)
        a = _dot(hf, w1_ref[:, c0:c1])
        b = _dot(hf, w3_ref[:, c0:c1])
        hm = (a * jax.nn.sigmoid(a) * b).astype(BF16)
        acc = acc + _dot(hm, w2_ref[c0:c1, :])
    o_ref[...] = acc


def _dense_ffn(hf, x, w1, w3, w2, tm=512):
    n = x.shape[0]
    row = pl.BlockSpec((tm, D_MODEL), lambda i: (i, 0))
    return pl.pallas_call(
        _ffn_kernel,
        grid=(n // tm,),
        in_specs=[row, row, _resident(w1.shape), _resident(w3.shape), _resident(w2.shape)],
        out_specs=row,
        out_shape=jax.ShapeDtypeStruct((n, D_MODEL), F32),
        input_output_aliases={1: 0},
        compiler_params=_cparams(1),
        name="dense_ffn",
    )(hf, x, w1, w3, w2)


MOE_TILE = 1024
MOE_FF_CHUNK = 1792
ROUTE_ROWS = 512


def _row_copy(src, dst, sem):
    return pltpu.make_async_copy(src, dst, sem)


def _dispatch_kernel(dest_ref, pad_base_ref, pad_cnt_ref, hf_ref, xs_ref, buf_ref, zrow_ref, load_sem,
                     row_sem, zero_sem, zero_tile_sem):
    i = pl.program_id(0)
    last = pl.num_programs(0) - 1
    slot = i % 2

    def load(step, slot_):
        src = hf_ref.at[pl.ds(pl.multiple_of(step * ROUTE_ROWS, ROUTE_ROWS), ROUTE_ROWS)]
        return pltpu.make_async_copy(src, buf_ref.at[slot_], load_sem.at[slot_])

    def drain(slot_):
        for _ in range(2 * ROUTE_ROWS):
            _row_copy(buf_ref.at[slot_, pl.ds(0, 1)], xs_ref.at[pl.ds(0, 1)], row_sem.at[slot_]).wait()

    def zero_row(dst_row):
        return _row_copy(zrow_ref.at[pl.ds(0, 1)], xs_ref.at[pl.ds(dst_row, 1)], zero_sem)

    def zero_rows(dst_row):
        dst = xs_ref.at[pl.ds(pl.multiple_of(dst_row, SUBLANES), SUBLANES)]
        return _row_copy(zrow_ref, dst, zero_tile_sem)

    unused = N_EXPERTS

    @pl.when(i == 0)
    def _():
        load(0, 0).start()
        zrow_ref[...] = jnp.zeros_like(zrow_ref)
        for e in range(N_EXPERTS):
            def zstart(j, carry):
                zero_row(pad_base_ref[e] + j).start()
                return carry

            lax.fori_loop(0, pad_cnt_ref[e], zstart, 0)

        def ztile(j, carry):
            zero_rows(pad_base_ref[unused] + j * SUBLANES).start()
            return carry

        lax.fori_loop(0, pad_cnt_ref[unused] // SUBLANES, ztile, 0)

    load(i, slot).wait()

    @pl.when(i > 0)
    def _():
        drain(1 - slot)

    @pl.when(i < last)
    def _():
        load(i + 1, 1 - slot).start()

    def start(r, carry):
        for k in range(2):
            d = dest_ref[0, 0, 2 * r + k]
            _row_copy(buf_ref.at[slot, pl.ds(r, 1)], xs_ref.at[pl.ds(d, 1)], row_sem.at[slot]).start(priority=k)
        return carry

    lax.fori_loop(0, ROUTE_ROWS, start, 0, unroll=8)

    @pl.when(i == last)
    def _():
        drain(slot)
        for e in range(N_EXPERTS):
            def zwait(j, carry):
                zero_row(0).wait()
                return carry

            lax.fori_loop(0, pad_cnt_ref[e], zwait, 0)

        def ztile_wait(j, carry):
            zero_rows(0).wait()
            return carry

        lax.fori_loop(0, pad_cnt_ref[unused] // SUBLANES, ztile_wait, 0)


def _dispatch(hf, dest, pad_base, pad_cnt, n_rows):
    n = hf.shape[0]
    steps = n // ROUTE_ROWS
    smem = pl.BlockSpec(memory_space=pltpu.SMEM)
    return pl.pallas_call(
        _dispatch_kernel,
        grid=(steps,),
        in_specs=[pl.BlockSpec((1, 1, 2 * ROUTE_ROWS), lambda i: (i, 0, 0), memory_space=pltpu.SMEM),
                  smem, smem, pl.BlockSpec(memory_space=pl.ANY)],
        out_specs=pl.BlockSpec(memory_space=pl.ANY),
        out_shape=jax.ShapeDtypeStruct((n_rows, D_MODEL), F32),
        scratch_shapes=[pltpu.VMEM((2, ROUTE_ROWS, D_MODEL), F32), pltpu.VMEM((SUBLANES, D_MODEL), F32),
                        pltpu.SemaphoreType.DMA((2,)), pltpu.SemaphoreType.DMA((2,)),
                        pltpu.SemaphoreType.DMA(()), pltpu.SemaphoreType.DMA(())],
        compiler_params=_cparams(1),
        name="moe_dispatch",
    )(dest.reshape(steps, 1, 2 * ROUTE_ROWS), pad_base, pad_cnt, hf)


F8 = jnp.float8_e4m3fn
F32_EXP_BIAS = 127
F32_MANT_BITS = 23
F8_TOP_EXP = 7


def _pow2_scale(amax):
    e = jnp.right_shift(lax.bitcast_convert_type(amax, jnp.int32), F32_MANT_BITS) & 0xFF
    se = jnp.clip(2 * F32_EXP_BIAS + F8_TOP_EXP - e, 1, 2 * F32_EXP_BIAS)
    scale = lax.bitcast_convert_type(jnp.left_shift(se, F32_MANT_BITS), F32)
    inv = lax.bitcast_convert_type(jnp.left_shift(2 * F32_EXP_BIAS - se, F32_MANT_BITS), F32)
    return scale, inv


def _quantize_rows(v):
    scale, inv = _pow2_scale(jnp.max(jnp.abs(v), axis=-1, keepdims=True))
    return (v * scale).astype(F8), inv


def _experts_kernel(te_ref, na_ref, winv_ref, xs_ref, w1_ref, w3_ref, w2_ref, ys_ref):
    i = pl.program_id(0)
    active = i < na_ref[0]
    e = te_ref[i]
    dff = w1_ref.shape[2]
    half = MOE_TILE // 2

    @pl.when(active)
    def _():
        xq = [_quantize_rows(xs_ref[h * half:(h + 1) * half, :]) for h in range(2)]
        for c0 in range(0, dff, MOE_FF_CHUNK):
            cols = slice(c0, c0 + MOE_FF_CHUNK)
            gates = [(_dot(xb, w1_ref[0, :, cols]) * (xinv * winv_ref[e, 0]), _dot(xb, w3_ref[0, :, cols]))
                     for xb, xinv in xq]
            for h, (a, b) in enumerate(gates):
                hm, hinv = _quantize_rows(a * jax.nn.sigmoid(a) * b)
                y = _dot(hm, w2_ref[0, cols, :]) * (hinv * xq[h][1] * (winv_ref[e, 1] * winv_ref[e, 2]))
                rows = slice(h * half, (h + 1) * half)
                if c0 == 0:
                    ys_ref[rows, :] = y
                else:
                    ys_ref[rows, :] += y

    @pl.when(jnp.logical_not(active))
    def _():
        ys_ref[...] = jnp.zeros_like(ys_ref)


def _quantize_expert_weights(w):
    scale, inv = _pow2_scale(jnp.max(jnp.abs(w.astype(F32)), axis=(1, 2)))
    return (w * scale[:, None, None]).astype(F8), inv


def _experts(xs, tile_expert, n_active, w1, w3, w2):
    n_rows = xs.shape[0]
    (w1, i1), (w3, i3), (w2, i2) = (_quantize_expert_weights(w) for w in (w1, w3, w2))
    winv = jnp.stack([i1, i3, i2], axis=1)
    dff = w1.shape[2]
    assert dff % MOE_FF_CHUNK == 0
    grid_spec = pltpu.PrefetchScalarGridSpec(
        num_scalar_prefetch=2,
        grid=(n_rows // MOE_TILE,),
        in_specs=[pl.BlockSpec(memory_space=pltpu.SMEM),
                  pl.BlockSpec((MOE_TILE, D_MODEL), lambda i, te, na: (jnp.minimum(i, na[0] - 1), 0)),
                  pl.BlockSpec((1, D_MODEL, dff), lambda i, te, na: (te[i], 0, 0)),
                  pl.BlockSpec((1, D_MODEL, dff), lambda i, te, na: (te[i], 0, 0)),
                  pl.BlockSpec((1, dff, D_MODEL), lambda i, te, na: (te[i], 0, 0))],
        out_specs=pl.BlockSpec((MOE_TILE, D_MODEL), lambda i, te, na: (i, 0)),
    )
    return pl.pallas_call(
        _experts_kernel,
        grid_spec=grid_spec,
        out_shape=jax.ShapeDtypeStruct((n_rows, D_MODEL), F32),
        compiler_params=_cparams(1),
        name="moe_experts",
    )(tile_expert, n_active, winv, xs, w1, w3, w2)


def _combine_kernel(dest_ref, next_dest_ref, x_ref, slab_ref, ys_ref, o_ref, y_ref, sem):
    i = pl.program_id(0)
    slot = i % 2

    def gather(d_ref, slot_):
        def start(r, carry):
            for k in range(2):
                d = d_ref[0, 0, 2 * r + k]
                _row_copy(ys_ref.at[pl.ds(d, 1)], y_ref.at[slot_, k, pl.ds(r, 1)], sem.at[slot_]).start()
            return carry

        lax.fori_loop(0, ROUTE_ROWS, start, 0, unroll=8)

    @pl.when(i == 0)
    def _():
        gather(dest_ref, 0)

    @pl.when(i + 1 < pl.num_programs(0))
    def _():
        gather(next_dest_ref, 1 - slot)

    for _ in range(2 * ROUTE_ROWS):
        _row_copy(ys_ref.at[pl.ds(0, 1)], y_ref.at[slot, 0, pl.ds(0, 1)], sem.at[slot]).wait()
    w0 = slab_ref[:, ROUTE_W_LANE:ROUTE_W_LANE + 1]
    w1 = slab_ref[:, ROUTE_W_LANE + 1:ROUTE_W_LANE + 2]
    o_ref[...] = x_ref[...] + w0 * y_ref[slot, 0] + w1 * y_ref[slot, 1]


def _combine(x, slab, dest, ys):
    n = x.shape[0]
    steps = n // ROUTE_ROWS
    row = pl.BlockSpec((ROUTE_ROWS, D_MODEL), lambda i: (i, 0))
    dest3 = dest.reshape(steps, 1, 2 * ROUTE_ROWS)
    dest_block = (1, 1, 2 * ROUTE_ROWS)
    return pl.pallas_call(
        _combine_kernel,
        grid=(steps,),
        in_specs=[pl.BlockSpec(dest_block, lambda i: (i, 0, 0), memory_space=pltpu.SMEM),
                  pl.BlockSpec(dest_block, lambda i: (jnp.minimum(i + 1, steps - 1), 0, 0),
                               memory_space=pltpu.SMEM),
                  row, pl.BlockSpec((ROUTE_ROWS, LANES), lambda i: (i, 0)),
                  pl.BlockSpec(memory_space=pl.ANY)],
        out_specs=row,
        out_shape=jax.ShapeDtypeStruct((n, D_MODEL), F32),
        scratch_shapes=[pltpu.VMEM((2, 2, ROUTE_ROWS, D_MODEL), F32), pltpu.SemaphoreType.DMA((2,))],
        input_output_aliases={2: 0},
        compiler_params=_cparams(1),
        name="moe_combine",
    )(dest3, dest3, x, slab, ys)


def _routing_plan(slab, n_rows):
    n = slab.shape[0]
    idx = slab[:, ROUTE_IDX_LANE:ROUTE_IDX_LANE + 2].astype(jnp.int32)
    onehot = (idx[:, :, None] == jnp.arange(N_EXPERTS, dtype=jnp.int32)).astype(jnp.int32).sum(axis=1)
    csum = jnp.cumsum(onehot, axis=0)
    rank = csum - onehot
    counts = csum[-1]
    padded = ((counts + MOE_TILE - 1) // MOE_TILE) * MOE_TILE
    ends = jnp.cumsum(padded)
    offs = ends - padded
    dest = offs[idx] + jnp.take_along_axis(rank, idx, axis=1)
    tile_start = jnp.arange(n_rows // MOE_TILE, dtype=jnp.int32) * MOE_TILE
    tile_expert = jnp.minimum(jnp.sum(tile_start[:, None] >= ends[None, :], axis=1), N_EXPERTS - 1)
    n_active = (ends[-1] // MOE_TILE).reshape(1)
    i32 = lambda v: v.astype(jnp.int32)
    pad_base = jnp.concatenate([offs + counts, ends[-1:]])
    pad_cnt = jnp.concatenate([padded - counts, n_rows - ends[-1:]])
    return i32(dest.reshape(-1)), i32(tile_expert), i32(n_active), i32(pad_base), i32(pad_cnt)


def _moe(hf, x, slab, w1, w3, w2):
    n = x.shape[0]
    n_rows = 2 * n + N_EXPERTS * MOE_TILE
    dest, tile_expert, n_active, pad_base, pad_cnt = _routing_plan(slab, n_rows)
    xs = _dispatch(hf, dest, pad_base, pad_cnt, n_rows)
    ys = _experts(xs, tile_expert, n_active, w1, w3, w2)
    return _combine(x, slab, dest, ys)


def _prep_inproj(w_in_l, gate_b_l, qn_b, kn_b, qn_c, kn_c):
    a_in = 2 * CONV_CH
    b_q = WB_HEADS * HEAD_DIM
    b_kv = WB_KV_HEADS * HEAD_DIM
    c_w = NA_HEADS * HEAD_DIM
    o = 0
    wa = w_in_l[:, o:o + a_in]; o += a_in
    wbq = w_in_l[:, o:o + b_q]; o += b_q
    wbk = w_in_l[:, o:o + b_kv]; o += b_kv
    wbv = w_in_l[:, o:o + b_kv]; o += b_kv
    wcq = w_in_l[:, o:o + c_w]; o += c_w
    wck = w_in_l[:, o:o + c_w]; o += c_w
    wcv = w_in_l[:, o:o + c_w]; o += c_w
    wg = w_in_l[:, o:]

    def dup(w):
        return jnp.repeat(w.reshape(D_MODEL, WB_KV_HEADS, 1, HEAD_DIM), 2, axis=2).reshape(D_MODEL, -1)

    w = jnp.concatenate([wg, wa, wbq, dup(wbk), dup(wbv), wcq, wck, wcv], axis=1).astype(BF16)
    scale = HEAD_DIM ** -0.5
    ones = lambda k: jnp.ones((k,), F32)
    cg = jnp.concatenate([
        ones(ZBQ), jnp.tile(qn_b.astype(F32), WB_HEADS) * scale, jnp.tile(kn_b.astype(F32), 2 * WB_KV_HEADS),
        ones(ZCQ - ZBV), jnp.tile(qn_c.astype(F32), NA_HEADS) * scale, jnp.tile(kn_c.astype(F32), NA_HEADS),
        ones(ZC - ZCV)])
    return w, cg.reshape(1, ZC), gate_b_l.astype(F32).reshape(1, -1)


MXU_TILE = 256


def _segment_mean_matrix():
    idx = np.arange(MXU_TILE) // HEAD_DIM
    return jnp.asarray((idx[:, None] == idx[None, :]).astype(np.float32) / HEAD_DIM, dtype=BF16)


def kernel(x, t5_rel_bias, attn_norm_g, w_in, gate_b, conv_dw_w, conv_dw_b, conv_ln_g, conv_ln_b,
           conv_w_out, wb_qn_g, wb_kn_g, wb_sink, wb_w_out, na_qn_g, na_kn_g, na_rpb, na_w_out, w_o,
           ffn_norm_g, ffn_w1, ffn_w3, ffn_w2, moe_w_router, moe_b_router, moe_w1, moe_w3, moe_w2):
    bsz, s, d = x.shape
    depth = w_in.shape[0]
    assert d == D_MODEL and s % BLOCK == 0 and s % (NA_QROWS * GRID_W) == 0
    n = bsz * s
    xf = x.reshape(n, d).astype(F32)
    row = lambda v: v.astype(F32).reshape(1, -1)

    t5_bias = _t5_bias(t5_rel_bias)
    na_bias = _na_bias(na_rpb, s // GRID_W)
    bd = _segment_mean_matrix()

    for layer in range(depth):
        w, cg, gb = _prep_inproj(w_in[layer], gate_b[layer], wb_qn_g[layer], wb_kn_g[layer],
                                 na_qn_g[layer], na_kn_g[layer])
        z = _inproj(xf, row(attn_norm_g[layer]), w, cg, gb, bd)
        u = _conv_mixer(z, bsz, s, conv_dw_w[layer].astype(F32), row(conv_dw_b[layer]),
                        row(conv_ln_g[layer]), row(conv_ln_b[layer]))
        ob = _window_attention(z, bsz, s, row(wb_sink[layer]), t5_bias)
        oc = _neighbourhood_attention(z, bsz, s, na_bias, layer)
        weights = (conv_w_out[layer].astype(BF16), wb_w_out[layer].astype(BF16),
                   na_w_out[layer].astype(BF16), w_o[layer].astype(BF16), row(ffn_norm_g[layer]))
        i = layer // 2
        if layer % 2 == 0:
            xf, hf = _merge(u, ob, oc, z, xf, *weights)
            xf = _dense_ffn(hf, xf, ffn_w1[i].astype(BF16), ffn_w3[i].astype(BF16), ffn_w2[i].astype(BF16))
        else:
            wr = jnp.zeros((D_MODEL, LANES), F32).at[:, :N_EXPERTS].set(moe_w_router[i].astype(F32))
            br = jnp.full((1, LANES), -jnp.inf, F32).at[0, :N_EXPERTS].set(moe_b_router[i].astype(F32))
            wr_hi = wr.astype(BF16)
            wr_lo = (wr - wr_hi.astype(F32)).astype(BF16)
            xf, hf, slab = _merge(u, ob, oc, z, xf, *weights, router=(wr_hi, wr_lo, br))
            xf = _moe(hf, xf, slab, moe_w1[i], moe_w3[i], moe_w2[i])
    return xf.reshape(bsz, s, d).astype(x.dtype)
```

```python
import functools
import math

import numpy as np
import jax
import jax.numpy as jnp
from jax import lax
from jax.experimental import pallas as pl
from jax.experimental.pallas import tpu as pltpu

D_MODEL = 1024
HEAD_DIM = 64
CONV_CH = 512
CONV_WIDTH = 31
WB_HEADS = 8
WB_KV_HEADS = 2
WINDOW = 128
BLOCK = 128
NA_HEADS = 8
GRID_W = 64
NA_KH = 8
NA_KW = 16
REL_BUCKETS = 32
REL_MAX_DIST = 128
N_EXPERTS = 8
EPS = 1e-6
NEG_INF = -1e30

LANES = 128

ZG = 0
ZA = 3072
ZBQ = 4096
ZBK = 4608
ZBV = 4864
ZCQ = 5120
ZCK = 5632
ZCV = 6144
ZC = 6656
ZCHUNK = 512

VMEM_LIMIT = 56 * 1024 * 1024

F32 = jnp.float32
BF16 = jnp.bfloat16


def _cparams(grid_rank, vmem=VMEM_LIMIT):
    return pltpu.CompilerParams(dimension_semantics=("arbitrary",) * grid_rank, vmem_limit_bytes=vmem)


def _resident(shape):
    nd = len(shape)
    return pl.BlockSpec(shape, lambda *_: (0,) * nd, pipeline_mode=pl.Buffered(1))


def _dot(a, b):
    return jnp.dot(a, b, preferred_element_type=F32)


def _dot_nt(a, b):
    return lax.dot_general(a, b, (((1,), (1,)), ((), ())), preferred_element_type=F32)


def _t5_bucket_np(rel):
    half = REL_BUCKETS // 2
    max_exact = half // 2
    ret = np.where(rel > 0, half, 0)
    n = np.abs(rel)
    nf = np.maximum(n, 1).astype(np.float32)
    large = max_exact + (np.log(nf / np.float32(max_exact)) / np.float32(math.log(REL_MAX_DIST / max_exact))
                         * np.float32(half - max_exact)).astype(np.int32)
    large = np.minimum(large, half - 1)
    return ret + np.where(n < max_exact, n, large)


def _window_bucket_table():
    col = np.arange(3 * BLOCK)[None, :]
    rel = (col - BLOCK) - np.arange(BLOCK)[:, None]
    band = np.where(np.abs(rel) <= WINDOW, _t5_bucket_np(rel), -1)
    first = np.where(col >= BLOCK, band, -1)
    last = np.where(col < 2 * BLOCK, band, -1)
    return np.stack([first, band, last]).astype(np.int32)


NA_QROWS = 2
NA_KROWS = 10


def _na_tables(rows):
    kh = min(NA_KH, rows)
    col = np.arange(GRID_W)
    col_start = np.clip(col - NA_KW // 2, 0, GRID_W - NA_KW)
    kc = np.arange(GRID_W)[None, :]
    inwin = (kc >= col_start[:, None]) & (kc < col_start[:, None] + NA_KW)
    dmat = np.where(inwin, kc - col[:, None] + NA_KW - 1, -1).astype(np.int32)
    variants, variant_of_pair, kstart = [], [], []
    for i in range(rows // NA_QROWS):
        ks = int(np.clip(NA_QROWS * i - kh // 2, 0, rows - NA_KROWS))
        tab = np.full((NA_QROWS, NA_KROWS), -1, np.int64)
        for a in range(NA_QROWS):
            r = NA_QROWS * i + a
            r0 = int(np.clip(r - kh // 2, 0, rows - kh))
            assert ks <= r0 and r0 + kh <= ks + NA_KROWS
            for kr in range(NA_KROWS):
                if r0 <= ks + kr < r0 + kh:
                    tab[a, kr] = ks + kr - r + NA_KH - 1
        for v, t in enumerate(variants):
            if np.array_equal(t, tab):
                variant_of_pair.append(v)
                break
        else:
            variant_of_pair.append(len(variants))
            variants.append(tab)
        kstart.append(ks)
    return dmat, variants, variant_of_pair, kstart


def _t5_bias_kernel(tab_ref, bucket_ref, o_ref):
    h = pl.program_id(0)
    for v in range(bucket_ref.shape[0]):
        bucket = bucket_ref[v]
        acc = jnp.full(bucket.shape, NEG_INF, F32)
        for b in range(REL_BUCKETS):
            acc = jnp.where(bucket == b, tab_ref[b, h], acc)
        o_ref[0, v] = acc


def _t5_bias(t5_rel_bias):
    bucket = jnp.asarray(_window_bucket_table())
    nv = bucket.shape[0]
    return pl.pallas_call(
        _t5_bias_kernel,
        grid=(WB_HEADS,),
        in_specs=[pl.BlockSpec(memory_space=pltpu.SMEM),
                  pl.BlockSpec((nv, BLOCK, 3 * BLOCK), lambda h: (0, 0, 0))],
        out_specs=pl.BlockSpec((1, nv, BLOCK, 3 * BLOCK), lambda h: (h, 0, 0, 0)),
        out_shape=jax.ShapeDtypeStruct((WB_HEADS, nv, BLOCK, 3 * BLOCK), F32),
        compiler_params=_cparams(1),
        name="t5_bias",
    )(t5_rel_bias.astype(F32), bucket)


def _na_bias_kernel(variants, rpb_ref, dmat_ref, o_ref):
    lh = pl.program_id(0) * NA_HEADS + pl.program_id(1)
    dmat = dmat_ref[...]
    ncol = 2 * NA_KW - 1
    base = []
    for dr in range(2 * NA_KH - 1):
        acc = jnp.full(dmat.shape, NEG_INF, F32)
        for d in range(ncol):
            acc = jnp.where(dmat == d, rpb_ref[lh, dr * ncol + d], acc)
        base.append(acc)
    neg = jnp.full(dmat.shape, NEG_INF, F32)
    for v, tab in enumerate(variants):
        for a in range(NA_QROWS):
            for kr in range(NA_KROWS):
                dr = int(tab[a, kr])
                tile = neg if dr < 0 else base[dr]
                o_ref[0, v, 0, a * GRID_W:(a + 1) * GRID_W, kr * GRID_W:(kr + 1) * GRID_W] = tile


def _na_bias(na_rpb, rows):
    depth = na_rpb.shape[0]
    dmat, variants, _, _ = _na_tables(rows)
    nv = len(variants)
    rpb = jnp.transpose(na_rpb.astype(F32), (0, 3, 1, 2)).reshape(depth * NA_HEADS, -1)
    qn, kn = NA_QROWS * GRID_W, NA_KROWS * GRID_W
    return pl.pallas_call(
        functools.partial(_na_bias_kernel, variants),
        grid=(depth, NA_HEADS),
        in_specs=[pl.BlockSpec(memory_space=pltpu.SMEM),
                  pl.BlockSpec((GRID_W, GRID_W), lambda l, h: (0, 0))],
        out_specs=pl.BlockSpec((1, nv, 1, qn, kn), lambda l, h: (l, 0, h, 0, 0)),
        out_shape=jax.ShapeDtypeStruct((depth, nv, NA_HEADS, qn, kn), F32),
        compiler_params=_cparams(2),
        name="na_bias",
    )(rpb, jnp.asarray(dmat))


def _inproj_kernel(x_ref, ng_ref, w_ref, cg_ref, gb_ref, bd_ref, z_ref, h_ref):
    x = x_ref[...]
    inv = lax.rsqrt(jnp.mean(x * x, axis=-1, keepdims=True) + EPS)
    h_ref[...] = (x * inv * ng_ref[...]).astype(BF16)

    def qknorm(zc, c0):
        w = bd_ref.shape[0]
        sq = (zc * zc).astype(BF16)
        ms = jnp.concatenate([_dot(sq[:, j:j + w], bd_ref[...]) for j in range(0, zc.shape[1], w)], axis=1)
        return zc * lax.rsqrt(ms + EPS) * cg_ref[:, c0:c0 + zc.shape[1]]

    for c0 in range(0, ZC, ZCHUNK):
        zc = _dot(h_ref[...], w_ref[:, c0:c0 + ZCHUNK])
        if c0 < ZA:
            out = jax.nn.sigmoid(zc + gb_ref[:, c0:c0 + ZCHUNK])
        elif c0 in (ZBQ, ZCQ, ZCK):
            out = qknorm(zc, c0)
        elif c0 == ZBK:
            half = ZBV - ZBK
            out = jnp.concatenate([qknorm(zc[:, :half], c0), zc[:, half:]], axis=1)
        else:
            out = zc
        z_ref[:, c0:c0 + ZCHUNK] = out.astype(BF16)


def _inproj(x, ng, w, cg, gb, bd, tm=512):
    n = x.shape[0]
    return pl.pallas_call(
        _inproj_kernel,
        grid=(n // tm,),
        in_specs=[pl.BlockSpec((tm, D_MODEL), lambda i: (i, 0)),
                  _resident((1, D_MODEL)), _resident((D_MODEL, ZC)), _resident((1, ZC)),
                  _resident((1, ZA)), _resident(bd.shape)],
        out_specs=pl.BlockSpec((tm, ZC), lambda i: (i, 0)),
        out_shape=jax.ShapeDtypeStruct((n, ZC), BF16),
        scratch_shapes=[pltpu.VMEM((tm, D_MODEL), BF16)],
        compiler_params=_cparams(1),
        name="inproj",
    )(x, ng, w, cg, gb, bd)


SUBLANES = 8
CONV_PAD = 16
CONV_HALF = 256
CONV_ROWS = 64
CONV_SHIFT_ROWS = 104
CONV_NORM_TILES = 4


def _conv_kernel(z_ref, dw_ref, db_ref, lg_ref, lb_ref, u_ref, vs_ref, c_ref):
    s = z_ref.shape[0]
    span = s + 2 * CONV_PAD
    first = CONV_PAD - CONV_WIDTH // 2
    glu_rows = 256

    for half in range(CONV_CH // CONV_HALF):
        l0 = half * CONV_HALF
        vs_ref[0, 0:CONV_PAD, :] = jnp.zeros((CONV_PAD, CONV_HALF), F32)
        vs_ref[0, CONV_PAD + s:span + SUBLANES, :] = jnp.zeros((CONV_PAD + SUBLANES, CONV_HALF), F32)

        def glu(i, carry):
            r = pl.multiple_of(i * glu_rows, glu_rows)
            a = z_ref[pl.ds(r, glu_rows), l0:l0 + CONV_HALF].astype(F32)
            g = z_ref[pl.ds(r, glu_rows), CONV_CH + l0:CONV_CH + l0 + CONV_HALF].astype(F32)
            vs_ref[0, pl.ds(CONV_PAD + r, glu_rows), :] = a * jax.nn.sigmoid(g)
            return carry

        lax.fori_loop(0, s // glu_rows, glu, 0)

        def shift(i, carry):
            r = pl.multiple_of(i * CONV_SHIFT_ROWS, SUBLANES)
            win = vs_ref[0, pl.ds(r, CONV_SHIFT_ROWS + SUBLANES), :]
            for k in range(1, SUBLANES):
                vs_ref[k, pl.ds(r, CONV_SHIFT_ROWS), :] = win[k:k + CONV_SHIFT_ROWS]
            return carry

        assert span % CONV_SHIFT_ROWS == 0
        lax.fori_loop(0, span // CONV_SHIFT_ROWS, shift, 0)

        def conv(i, carry):
            r = pl.multiple_of(i * CONV_ROWS, CONV_ROWS)
            acc = jnp.zeros((CONV_ROWS, CONV_HALF), F32) + db_ref[:, l0:l0 + CONV_HALF]
            for w in range(CONV_WIDTH):
                k, a = (first + w) % SUBLANES, (first + w) // SUBLANES
                tap = vs_ref[k, pl.ds(r + a * SUBLANES, CONV_ROWS), :]
                acc = acc + tap * dw_ref[w:w + 1, l0:l0 + CONV_HALF]
            c_ref[pl.ds(r, CONV_ROWS), l0:l0 + CONV_HALF] = acc
            return carry

        lax.fori_loop(0, s // CONV_ROWS, conv, 0)

    def norm(i, carry):
        for t in range(CONV_NORM_TILES):
            r = pl.multiple_of((i * CONV_NORM_TILES + t) * CONV_ROWS, CONV_ROWS)
            c = c_ref[pl.ds(r, CONV_ROWS), :]
            xc = c - jnp.mean(c, axis=-1, keepdims=True)
            var = jnp.mean(xc * xc, axis=-1, keepdims=True)
            y = xc * lax.rsqrt(var + EPS) * lg_ref[...] + lb_ref[...]
            u_ref[pl.ds(r, CONV_ROWS), :] = (y * jax.nn.sigmoid(y)).astype(BF16)
        return carry

    assert s % (CONV_ROWS * CONV_NORM_TILES) == 0
    lax.fori_loop(0, s // (CONV_ROWS * CONV_NORM_TILES), norm, 0)


def _conv_mixer(z, bsz, s, dw, db, lg, lb):
    return pl.pallas_call(
        _conv_kernel,
        grid=(bsz,),
        in_specs=[pl.BlockSpec((s, 2 * CONV_CH), lambda b: (b, ZA // (2 * CONV_CH))),
                  _resident((CONV_WIDTH, CONV_CH)), _resident((1, CONV_CH)),
                  _resident((1, CONV_CH)), _resident((1, CONV_CH))],
        out_specs=pl.BlockSpec((s, CONV_CH), lambda b: (b, 0)),
        out_shape=jax.ShapeDtypeStruct((bsz * s, CONV_CH), BF16),
        scratch_shapes=[pltpu.VMEM((SUBLANES, s + 2 * CONV_PAD + SUBLANES, CONV_HALF), F32),
                        pltpu.VMEM((s, CONV_CH), F32)],
        compiler_params=_cparams(1),
        name="conv_mixer",
    )(z, dw, db, lg, lb)


def _head_masks():
    lane = lax.broadcasted_iota(jnp.int32, (1, LANES), 1)
    return lane < HEAD_DIM


ATTN_UNROLL = 8


def _softmax_pv(sc, vw, sink=None):
    m = jnp.max(sc, axis=-1, keepdims=True)
    if sink is not None:
        m = jnp.maximum(m, sink)
    p = jnp.exp(sc - m)
    l = jnp.sum(p, axis=-1, keepdims=True)
    if sink is not None:
        l = l + jnp.exp(sink - m)
    return _dot(p.astype(BF16), vw) * (1.0 / l)


def _wattn_kernel(pairs_per_kv, sink_ref, q_ref, k_ref, v_ref, bias_ref, o_ref, kp_ref, vp_ref):
    hp = pl.program_id(1)
    s = q_ref.shape[0]
    nb = s // BLOCK
    lo = _head_masks()
    zero = jnp.zeros((), BF16)

    @pl.when(hp % pairs_per_kv == 0)
    def _():
        pad = jnp.zeros((BLOCK, LANES), BF16)
        for src, dst in ((k_ref, kp_ref), (v_ref, vp_ref)):
            dst[0:BLOCK, :] = pad
            dst[BLOCK:BLOCK + s, :] = src[...]
            dst[BLOCK + s:2 * BLOCK + s, :] = pad

    def group(g, carry):
        rows, scores = [], []
        for u in range(ATTN_UNROLL):
            n = g * ATTN_UNROLL + u
            r = pl.multiple_of(n * BLOCK, BLOCK)
            var = jnp.where(n == 0, 0, jnp.where(n == nb - 1, 2, 1))
            q = q_ref[pl.ds(r, BLOCK), :]
            kw = kp_ref[pl.ds(r, 3 * BLOCK), :]
            rows.append(r)
            for hh in range(2):
                qm = jnp.where(lo if hh == 0 else jnp.logical_not(lo), q, zero)
                scores.append(_dot_nt(qm, kw) + bias_ref[hh, var])
        for u, r in enumerate(rows):
            vw = vp_ref[pl.ds(r, 3 * BLOCK), :]
            outs = [_softmax_pv(scores[2 * u + hh], vw, sink_ref[0, 2 * hp + hh]) for hh in range(2)]
            o_ref[pl.ds(r, BLOCK), :] = jnp.where(lo, outs[0], outs[1]).astype(BF16)
        return carry

    lax.fori_loop(0, nb // ATTN_UNROLL, group, 0)


def _window_attention(z, bsz, s, sink, bias):
    npair = WB_HEADS // 2
    pairs_per_kv = npair // WB_KV_HEADS
    nv = bias.shape[1]
    assert (s // BLOCK) % ATTN_UNROLL == 0 and s // BLOCK >= 2
    return pl.pallas_call(
        functools.partial(_wattn_kernel, pairs_per_kv),
        grid=(bsz, npair),
        in_specs=[pl.BlockSpec(memory_space=pltpu.SMEM),
                  pl.BlockSpec((s, LANES), lambda b, hp: (b, ZBQ // LANES + hp)),
                  pl.BlockSpec((s, LANES), lambda b, hp: (b, ZBK // LANES + hp // pairs_per_kv)),
                  pl.BlockSpec((s, LANES), lambda b, hp: (b, ZBV // LANES + hp // pairs_per_kv)),
                  pl.BlockSpec((2, nv, BLOCK, 3 * BLOCK), lambda b, hp: (hp, 0, 0, 0))],
        out_specs=pl.BlockSpec((s, LANES), lambda b, hp: (b, hp)),
        out_shape=jax.ShapeDtypeStruct((bsz * s, WB_HEADS * HEAD_DIM), BF16),
        scratch_shapes=[pltpu.VMEM((s + 2 * BLOCK, LANES), BF16), pltpu.VMEM((s + 2 * BLOCK, LANES), BF16)],
        compiler_params=_cparams(2),
        name="window_attention",
    )(sink, z, z, z, bias)


def _nattn_kernel(n_edge, n_pairs, q_ref, k_ref, v_ref, bias_ref, o_ref):
    lo = _head_masks()
    zero = jnp.zeros((), BF16)
    qn, kn = NA_QROWS * GRID_W, NA_KROWS * GRID_W
    max_ks = k_ref.shape[0] // GRID_W - NA_KROWS

    def group(g, carry):
        rows, scores = [], []
        for u in range(ATTN_UNROLL):
            i = g * ATTN_UNROLL + u
            r = pl.multiple_of(i * qn, qn)
            ks = jnp.clip(NA_QROWS * i - NA_KH // 2, 0, max_ks)
            rk = pl.multiple_of(ks * GRID_W, LANES)
            var = jnp.where(i < n_edge, i,
                            jnp.where(i >= n_pairs - n_edge, i - (n_pairs - 2 * n_edge - 1), n_edge))
            q = q_ref[pl.ds(r, qn), :]
            kw = k_ref[pl.ds(rk, kn), :]
            rows.append((r, rk))
            for hh in range(2):
                qm = jnp.where(lo if hh == 0 else jnp.logical_not(lo), q, zero)
                scores.append(_dot_nt(qm, kw) + bias_ref[0, var, hh])
        for u, (r, rk) in enumerate(rows):
            vw = v_ref[pl.ds(rk, kn), :]
            outs = [_softmax_pv(scores[2 * u + hh], vw) for hh in range(2)]
            o_ref[pl.ds(r, qn), :] = jnp.where(lo, outs[0], outs[1]).astype(BF16)
        return carry

    lax.fori_loop(0, n_pairs // ATTN_UNROLL, group, 0)


def _neighbourhood_attention(z, bsz, s, bias, layer):
    rows = s // GRID_W
    _, variants, variant_of_pair, kstart = _na_tables(rows)
    n_pairs = rows // NA_QROWS
    nv = len(variants)
    n_edge = (nv - 1) // 2
    assert n_pairs % ATTN_UNROLL == 0
    for i in range(n_pairs):
        want = i if i < n_edge else (i - (n_pairs - 2 * n_edge - 1) if i >= n_pairs - n_edge else n_edge)
        assert variant_of_pair[i] == want
        assert kstart[i] == int(np.clip(NA_QROWS * i - NA_KH // 2, 0, rows - NA_KROWS))
        assert (kstart[i] * GRID_W) % LANES == 0
    npair = NA_HEADS // 2
    qn, kn = NA_QROWS * GRID_W, NA_KROWS * GRID_W
    return pl.pallas_call(
        functools.partial(_nattn_kernel, n_edge, n_pairs),
        grid=(npair, bsz),
        in_specs=[pl.BlockSpec((s, LANES), lambda hp, b: (b, ZCQ // LANES + hp)),
                  pl.BlockSpec((s, LANES), lambda hp, b: (b, ZCK // LANES + hp)),
                  pl.BlockSpec((s, LANES), lambda hp, b: (b, ZCV // LANES + hp)),
                  pl.BlockSpec((1, nv, 2, qn, kn), lambda hp, b: (layer, 0, hp, 0, 0))],
        out_specs=pl.BlockSpec((s, LANES), lambda hp, b: (b, hp)),
        out_shape=jax.ShapeDtypeStruct((bsz * s, NA_HEADS * HEAD_DIM), BF16),
        compiler_params=_cparams(2),
        name="neighbourhood_attention",
    )(z, z, z, bias)


ROUTE_IDX_LANE = N_EXPERTS
ROUTE_W_LANE = N_EXPERTS + 2


def _merge_kernel(with_router, u_ref, ob_ref, oc_ref, g_ref, x_ref, wa_ref, wb_ref, wc_ref, wo_ref,
                  fg_ref, *rest):
    if with_router:
        wrh_ref, wrl_ref, br_ref, xo_ref, hf_ref, slab_ref = rest
    else:
        xo_ref, hf_ref = rest
    d = D_MODEL
    m = g_ref[:, 0:d].astype(F32) * _dot(u_ref[...], wa_ref[...])
    m = m + g_ref[:, d:2 * d].astype(F32) * _dot(ob_ref[...], wb_ref[...])
    m = m + g_ref[:, 2 * d:3 * d].astype(F32) * _dot(oc_ref[...], wc_ref[...])
    xn = x_ref[...] + _dot(m.astype(BF16), wo_ref[...])
    xo_ref[...] = xn
    hf = xn * lax.rsqrt(jnp.mean(xn * xn, axis=-1, keepdims=True) + EPS) * fg_ref[...]
    hf_ref[...] = hf.astype(hf_ref.dtype)
    if with_router:
        hf_hi = hf.astype(BF16)
        hf_lo = (hf - hf_hi.astype(F32)).astype(BF16)
        logits = (_dot(hf_hi, wrh_ref[...]) + _dot(hf_lo, wrh_ref[...]) + _dot(hf_hi, wrl_ref[...])
                  + br_ref[...])
        lane = lax.broadcasted_iota(jnp.int32, logits.shape, 1)
        m1 = jnp.max(logits, axis=-1, keepdims=True)
        i1 = jnp.min(jnp.where(logits == m1, lane, LANES), axis=-1, keepdims=True)
        rest_l = jnp.where(lane == i1, -jnp.inf, logits)
        m2 = jnp.max(rest_l, axis=-1, keepdims=True)
        i2 = jnp.min(jnp.where(rest_l == m2, lane, LANES), axis=-1, keepdims=True)
        e2 = jnp.exp(m2 - m1)
        w1 = 1.0 / (1.0 + e2)
        w2 = e2 / (1.0 + e2)
        slab = jnp.where(lane == ROUTE_IDX_LANE, i1.astype(F32), 0.0)
        slab = jnp.where(lane == ROUTE_IDX_LANE + 1, i2.astype(F32), slab)
        slab = jnp.where(lane == ROUTE_W_LANE, w1, slab)
        slab = jnp.where(lane == ROUTE_W_LANE + 1, w2, slab)
        slab_ref[...] = slab


def _merge(u, ob, oc, z, x, wa, wb, wc, wo, fg, router=None, tm=512):
    n = x.shape[0]
    with_router = router is not None
    row = lambda w: pl.BlockSpec((tm, w), lambda i: (i, 0))
    in_specs = [row(CONV_CH), row(WB_HEADS * HEAD_DIM), row(NA_HEADS * HEAD_DIM),
                pl.BlockSpec((tm, ZA), lambda i: (i, 0)), row(D_MODEL),
                _resident(wa.shape), _resident(wb.shape), _resident(wc.shape), _resident(wo.shape),
                _resident((1, D_MODEL))]
    args = [u, ob, oc, z, x, wa, wb, wc, wo, fg]
    out_specs = [row(D_MODEL), row(D_MODEL)]
    out_shape = [jax.ShapeDtypeStruct((n, D_MODEL), F32),
                 jax.ShapeDtypeStruct((n, D_MODEL), F32 if with_router else BF16)]
    if with_router:
        in_specs += [_resident((D_MODEL, LANES)), _resident((D_MODEL, LANES)), _resident((1, LANES))]
        args += list(router)
        out_specs.append(row(LANES))
        out_shape.append(jax.ShapeDtypeStruct((n, LANES), F32))
    return pl.pallas_call(
        functools.partial(_merge_kernel, with_router),
        grid=(n // tm,),
        in_specs=in_specs, out_specs=out_specs, out_shape=out_shape,
        compiler_params=_cparams(1),
        name="merge_router" if with_router else "merge",
    )(*args)


FFN_CHUNK = 1024


def _ffn_kernel(hf_ref, x_ref, w1_ref, w3_ref, w2_ref, o_ref):
    dff = w1_ref.shape[1]
    hf = hf_ref[...]
    acc = x_ref[...]
    for c0 in range(0, dff, FFN_CHUNK):
        c1 = min(c0 + FFN_CHUNK, dff)
        a = _dot(hf, w1_ref[:, c0:c1])
        b = _dot(hf, w3_ref[:, c0:c1])
        hm = (a * jax.nn.sigmoid(a) * b).astype(BF16)
        acc = acc + _dot(hm, w2_ref[c0:c1, :])
    o_ref[...] = acc


def _dense_ffn(hf, x, w1, w3, w2, tm=512):
    n = x.shape[0]
    row = pl.BlockSpec((tm, D_MODEL), lambda i: (i, 0))
    return pl.pallas_call(
        _ffn_kernel,
        grid=(n // tm,),
        in_specs=[row, row, _resident(w1.shape), _resident(w3.shape), _resident(w2.shape)],
        out_specs=row,
        out_shape=jax.ShapeDtypeStruct((n, D_MODEL), F32),
        input_output_aliases={1: 0},
        compiler_params=_cparams(1),
        name="dense_ffn",
    )(hf, x, w1, w3, w2)


MOE_TILE = 1024
MOE_FF_CHUNK = 1792
ROUTE_ROWS = 512


def _row_copy(src, dst, sem):
    return pltpu.make_async_copy(src, dst, sem)


def _dispatch_kernel(dest_ref, pad_base_ref, pad_cnt_ref, hf_ref, xs_ref, buf_ref, zrow_ref, load_sem,
                     row_sem, zero_sem, zero_tile_sem):
    i = pl.program_id(0)
    last = pl.num_programs(0) - 1
    slot = i % 2

    def load(step, slot_):
        src = hf_ref.at[pl.ds(pl.multiple_of(step * ROUTE_ROWS, ROUTE_ROWS), ROUTE_ROWS)]
        return pltpu.make_async_copy(src, buf_ref.at[slot_], load_sem.at[slot_])

    def drain(slot_):
        for _ in range(2 * ROUTE_ROWS):
            _row_copy(buf_ref.at[slot_, pl.ds(0, 1)], xs_ref.at[pl.ds(0, 1)], row_sem.at[slot_]).wait()

    def zero_row(dst_row):
        return _row_copy(zrow_ref.at[pl.ds(0, 1)], xs_ref.at[pl.ds(dst_row, 1)], zero_sem)

    def zero_rows(dst_row):
        dst = xs_ref.at[pl.ds(pl.multiple_of(dst_row, SUBLANES), SUBLANES)]
        return _row_copy(zrow_ref, dst, zero_tile_sem)

    unused = N_EXPERTS

    @pl.when(i == 0)
    def _():
        load(0, 0).start()
        zrow_ref[...] = jnp.zeros_like(zrow_ref)
        for e in range(N_EXPERTS):
            def zstart(j, carry):
                zero_row(pad_base_ref[e] + j).start()
                return carry

            lax.fori_loop(0, pad_cnt_ref[e], zstart, 0)

        def ztile(j, carry):
            zero_rows(pad_base_ref[unused] + j * SUBLANES).start()
            return carry

        lax.fori_loop(0, pad_cnt_ref[unused] // SUBLANES, ztile, 0)

    load(i, slot).wait()

    @pl.when(i > 0)
    def _():
        drain(1 - slot)

    @pl.when(i < last)
    def _():
        load(i + 1, 1 - slot).start()

    def start(r, carry):
        for k in range(2):
            d = dest_ref[0, 0, k * ROUTE_ROWS + r]
            _row_copy(buf_ref.at[slot, pl.ds(r, 1)], xs_ref.at[pl.ds(d, 1)], row_sem.at[slot]).start(priority=k)
        return carry

    lax.fori_loop(0, ROUTE_ROWS, start, 0, unroll=8)

    @pl.when(i == last)
    def _():
        drain(slot)
        for e in range(N_EXPERTS):
            def zwait(j, carry):
                zero_row(0).wait()
                return carry

            lax.fori_loop(0, pad_cnt_ref[e], zwait, 0)

        def ztile_wait(j, carry):
            zero_rows(0).wait()
            return carry

        lax.fori_loop(0, pad_cnt_ref[unused] // SUBLANES, ztile_wait, 0)


def _dispatch(hf, dest, pad_base, pad_cnt, n_rows):
    n = hf.shape[0]
    steps = n // ROUTE_ROWS
    smem = pl.BlockSpec(memory_space=pltpu.SMEM)
    return pl.pallas_call(
        _dispatch_kernel,
        grid=(steps,),
        in_specs=[pl.BlockSpec((1, 1, 2 * ROUTE_ROWS), lambda i: (i, 0, 0), memory_space=pltpu.SMEM),
                  smem, smem, pl.BlockSpec(memory_space=pl.ANY)],
        out_specs=pl.BlockSpec(memory_space=pl.ANY),
        out_shape=jax.ShapeDtypeStruct((n_rows, D_MODEL), F32),
        scratch_shapes=[pltpu.VMEM((2, ROUTE_ROWS, D_MODEL), F32), pltpu.VMEM((SUBLANES, D_MODEL), F32),
                        pltpu.SemaphoreType.DMA((2,)), pltpu.SemaphoreType.DMA((2,)),
                        pltpu.SemaphoreType.DMA(()), pltpu.SemaphoreType.DMA(())],
        compiler_params=_cparams(1),
        name="moe_dispatch",
    )(dest, pad_base, pad_cnt, hf)


F8 = jnp.float8_e4m3fn
F32_EXP_BIAS = 127
F32_MANT_BITS = 23
F8_TOP_EXP = 7


def _pow2_scale(amax):
    e = jnp.right_shift(lax.bitcast_convert_type(amax, jnp.int32), F32_MANT_BITS) & 0xFF
    se = jnp.clip(2 * F32_EXP_BIAS + F8_TOP_EXP - e, 1, 2 * F32_EXP_BIAS)
    scale = lax.bitcast_convert_type(jnp.left_shift(se, F32_MANT_BITS), F32)
    inv = lax.bitcast_convert_type(jnp.left_shift(2 * F32_EXP_BIAS - se, F32_MANT_BITS), F32)
    return scale, inv


def _quantize_rows(v):
    scale, inv = _pow2_scale(jnp.max(jnp.abs(v), axis=-1, keepdims=True))
    return (v * scale).astype(F8), inv


def _experts_kernel(te_ref, na_ref, winv_ref, xs_ref, w1_ref, w3_ref, w2_ref, ys_ref):
    i = pl.program_id(0)
    active = i < na_ref[0]
    e = te_ref[i]
    dff = w1_ref.shape[2]
    half = MOE_TILE // 2

    @pl.when(active)
    def _():
        xq = [_quantize_rows(xs_ref[h * half:(h + 1) * half, :]) for h in range(2)]
        for c0 in range(0, dff, MOE_FF_CHUNK):
            cols = slice(c0, c0 + MOE_FF_CHUNK)
            gates = [(_dot(xb, w1_ref[0, :, cols]) * (xinv * winv_ref[e, 0]), _dot(xb, w3_ref[0, :, cols]))
                     for xb, xinv in xq]
            for h, (a, b) in enumerate(gates):
                hm, hinv = _quantize_rows(a * jax.nn.sigmoid(a) * b)
                y = _dot(hm, w2_ref[0, cols, :]) * (hinv * xq[h][1] * (winv_ref[e, 1] * winv_ref[e, 2]))
                rows = slice(h * half, (h + 1) * half)
                if c0 == 0:
                    ys_ref[rows, :] = y
                else:
                    ys_ref[rows, :] += y

    @pl.when(jnp.logical_not(active))
    def _():
        ys_ref[...] = jnp.zeros_like(ys_ref)


def _quantize_expert_weights(w):
    scale, inv = _pow2_scale(jnp.max(jnp.abs(w.astype(F32)), axis=(1, 2)))
    return (w * scale[:, None, None]).astype(F8), inv


def _experts(xs, tile_expert, n_active, w1, w3, w2):
    n_rows = xs.shape[0]
    (w1, i1), (w3, i3), (w2, i2) = (_quantize_expert_weights(w) for w in (w1, w3, w2))
    winv = jnp.stack([i1, i3, i2], axis=1)
    dff = w1.shape[2]
    assert dff % MOE_FF_CHUNK == 0
    grid_spec = pltpu.PrefetchScalarGridSpec(
        num_scalar_prefetch=2,
        grid=(n_rows // MOE_TILE,),
        in_specs=[pl.BlockSpec(memory_space=pltpu.SMEM),
                  pl.BlockSpec((MOE_TILE, D_MODEL), lambda i, te, na: (jnp.minimum(i, na[0] - 1), 0)),
                  pl.BlockSpec((1, D_MODEL, dff), lambda i, te, na: (te[i], 0, 0)),
                  pl.BlockSpec((1, D_MODEL, dff), lambda i, te, na: (te[i], 0, 0)),
                  pl.BlockSpec((1, dff, D_MODEL), lambda i, te, na: (te[i], 0, 0))],
        out_specs=pl.BlockSpec((MOE_TILE, D_MODEL), lambda i, te, na: (i, 0)),
    )
    return pl.pallas_call(
        _experts_kernel,
        grid_spec=grid_spec,
        out_shape=jax.ShapeDtypeStruct((n_rows, D_MODEL), F32),
        compiler_params=_cparams(1),
        name="moe_experts",
    )(tile_expert, n_active, winv, xs, w1, w3, w2)


def _combine_kernel(dest_ref, next_dest_ref, x_ref, slab_ref, ys_ref, o_ref, y_ref, sem):
    i = pl.program_id(0)
    slot = i % 2

    def gather(d_ref, slot_):
        def start(r, carry):
            for k in range(2):
                d = d_ref[0, 0, k * ROUTE_ROWS + r]
                _row_copy(ys_ref.at[pl.ds(d, 1)], y_ref.at[slot_, k, pl.ds(r, 1)], sem.at[slot_]).start()
            return carry

        lax.fori_loop(0, ROUTE_ROWS, start, 0, unroll=8)

    @pl.when(i == 0)
    def _():
        gather(dest_ref, 0)

    @pl.when(i + 1 < pl.num_programs(0))
    def _():
        gather(next_dest_ref, 1 - slot)

    for _ in range(2 * ROUTE_ROWS):
        _row_copy(ys_ref.at[pl.ds(0, 1)], y_ref.at[slot, 0, pl.ds(0, 1)], sem.at[slot]).wait()
    w0 = slab_ref[:, ROUTE_W_LANE:ROUTE_W_LANE + 1]
    w1 = slab_ref[:, ROUTE_W_LANE + 1:ROUTE_W_LANE + 2]
    o_ref[...] = x_ref[...] + w0 * y_ref[slot, 0] + w1 * y_ref[slot, 1]


def _combine(x, slab, dest, ys):
    n = x.shape[0]
    steps = n // ROUTE_ROWS
    row = pl.BlockSpec((ROUTE_ROWS, D_MODEL), lambda i: (i, 0))
    dest_block = (1, 1, 2 * ROUTE_ROWS)
    return pl.pallas_call(
        _combine_kernel,
        grid=(steps,),
        in_specs=[pl.BlockSpec(dest_block, lambda i: (i, 0, 0), memory_space=pltpu.SMEM),
                  pl.BlockSpec(dest_block, lambda i: (jnp.minimum(i + 1, steps - 1), 0, 0),
                               memory_space=pltpu.SMEM),
                  row, pl.BlockSpec((ROUTE_ROWS, LANES), lambda i: (i, 0)),
                  pl.BlockSpec(memory_space=pl.ANY)],
        out_specs=row,
        out_shape=jax.ShapeDtypeStruct((n, D_MODEL), F32),
        scratch_shapes=[pltpu.VMEM((2, 2, ROUTE_ROWS, D_MODEL), F32), pltpu.SemaphoreType.DMA((2,))],
        input_output_aliases={2: 0},
        compiler_params=_cparams(1),
        name="moe_combine",
    )(dest, dest, x, slab, ys)


def _routing_plan(slab, n_rows):
    n = slab.shape[0]
    idx = slab[:, ROUTE_IDX_LANE:ROUTE_IDX_LANE + 2].astype(jnp.int32).T
    experts = jnp.arange(N_EXPERTS, dtype=jnp.int32)[:, None]
    chosen = [idx[k][None, :] == experts for k in range(2)]
    onehot = chosen[0].astype(jnp.int32) + chosen[1].astype(jnp.int32)
    csum = jnp.cumsum(onehot, axis=1)
    counts = csum[:, -1]
    padded = ((counts + MOE_TILE - 1) // MOE_TILE) * MOE_TILE
    ends = jnp.cumsum(padded)
    offs = ends - padded
    row = offs[:, None] + csum - onehot
    dest = jnp.stack([jnp.sum(jnp.where(c, row, 0), axis=0) for c in chosen])
    steps = n // ROUTE_ROWS
    dest = dest.reshape(2, steps, ROUTE_ROWS).transpose(1, 0, 2).reshape(steps, 1, 2 * ROUTE_ROWS)
    tile_start = jnp.arange(n_rows // MOE_TILE, dtype=jnp.int32) * MOE_TILE
    tile_expert = jnp.minimum(jnp.sum(tile_start[:, None] >= ends[None, :], axis=1), N_EXPERTS - 1)
    n_active = (ends[-1] // MOE_TILE).reshape(1)
    i32 = lambda v: v.astype(jnp.int32)
    pad_base = jnp.concatenate([offs + counts, ends[-1:]])
    pad_cnt = jnp.concatenate([padded - counts, n_rows - ends[-1:]])
    return i32(dest), i32(tile_expert), i32(n_active), i32(pad_base), i32(pad_cnt)


def _moe(hf, x, slab, w1, w3, w2):
    n = x.shape[0]
    n_rows = 2 * n + N_EXPERTS * MOE_TILE
    dest, tile_expert, n_active, pad_base, pad_cnt = _routing_plan(slab, n_rows)
    xs = _dispatch(hf, dest, pad_base, pad_cnt, n_rows)
    ys = _experts(xs, tile_expert, n_active, w1, w3, w2)
    return _combine(x, slab, dest, ys)


def _prep_inproj(w_in_l, gate_b_l, qn_b, kn_b, qn_c, kn_c):
    a_in = 2 * CONV_CH
    b_q = WB_HEADS * HEAD_DIM
    b_kv = WB_KV_HEADS * HEAD_DIM
    c_w = NA_HEADS * HEAD_DIM
    o = 0
    wa = w_in_l[:, o:o + a_in]; o += a_in
    wbq = w_in_l[:, o:o + b_q]; o += b_q
    wbk = w_in_l[:, o:o + b_kv]; o += b_kv
    wbv = w_in_l[:, o:o + b_kv]; o += b_kv
    wcq = w_in_l[:, o:o + c_w]; o += c_w
    wck = w_in_l[:, o:o + c_w]; o += c_w
    wcv = w_in_l[:, o:o + c_w]; o += c_w
    wg = w_in_l[:, o:]

    def dup(w):
        return jnp.repeat(w.reshape(D_MODEL, WB_KV_HEADS, 1, HEAD_DIM), 2, axis=2).reshape(D_MODEL, -1)

    w = jnp.concatenate([wg, wa, wbq, dup(wbk), dup(wbv), wcq, wck, wcv], axis=1).astype(BF16)
    scale = HEAD_DIM ** -0.5
    ones = lambda k: jnp.ones((k,), F32)
    cg = jnp.concatenate([
        ones(ZBQ), jnp.tile(qn_b.astype(F32), WB_HEADS) * scale, jnp.tile(kn_b.astype(F32), 2 * WB_KV_HEADS),
        ones(ZCQ - ZBV), jnp.tile(qn_c.astype(F32), NA_HEADS) * scale, jnp.tile(kn_c.astype(F32), NA_HEADS),
        ones(ZC - ZCV)])
    return w, cg.reshape(1, ZC), gate_b_l.astype(F32).reshape(1, -1)


MXU_TILE = 256


def _segment_mean_matrix():
    idx = np.arange(MXU_TILE) // HEAD_DIM
    return jnp.asarray((idx[:, None] == idx[None, :]).astype(np.float32) / HEAD_DIM, dtype=BF16)


def kernel(x, t5_rel_bias, attn_norm_g, w_in, gate_b, conv_dw_w, conv_dw_b, conv_ln_g, conv_ln_b,
           conv_w_out, wb_qn_g, wb_kn_g, wb_sink, wb_w_out, na_qn_g, na_kn_g, na_rpb, na_w_out, w_o,
           ffn_norm_g, ffn_w1, ffn_w3, ffn_w2, moe_w_router, moe_b_router, moe_w1, moe_w3, moe_w2):
    bsz, s, d = x.shape
    depth = w_in.shape[0]
    assert d == D_MODEL and s % BLOCK == 0 and s % (NA_QROWS * GRID_W) == 0
    n = bsz * s
    xf = x.reshape(n, d).astype(F32)
    row = lambda v: v.astype(F32).reshape(1, -1)

    t5_bias = _t5_bias(t5_rel_bias)
    na_bias = _na_bias(na_rpb, s // GRID_W)
    bd = _segment_mean_matrix()

    for layer in range(depth):
        w, cg, gb = _prep_inproj(w_in[layer], gate_b[layer], wb_qn_g[layer], wb_kn_g[layer],
                                 na_qn_g[layer], na_kn_g[layer])
        z = _inproj(xf, row(attn_norm_g[layer]), w, cg, gb, bd)
        u = _conv_mixer(z, bsz, s, conv_dw_w[layer].astype(F32), row(conv_dw_b[layer]),
                        row(conv_ln_g[layer]), row(conv_ln_b[layer]))
        ob = _window_attention(z, bsz, s, row(wb_sink[layer]), t5_bias)
        oc = _neighbourhood_attention(z, bsz, s, na_bias, layer)
        weights = (conv_w_out[layer].astype(BF16), wb_w_out[layer].astype(BF16),
                   na_w_out[layer].astype(BF16), w_o[layer].astype(BF16), row(ffn_norm_g[layer]))
        i = layer // 2
        if layer % 2 == 0:
            xf, hf = _merge(u, ob, oc, z, xf, *weights)
            xf = _dense_ffn(hf, xf, ffn_w1[i].astype(BF16), ffn_w3[i].astype(BF16), ffn_w2[i].astype(BF16))
        else:
            wr = jnp.zeros((D_MODEL, LANES), F32).at[:, :N_EXPERTS].set(moe_w_router[i].astype(F32))
            br = jnp.full((1, LANES), -jnp.inf, F32).at[0, :N_EXPERTS].set(moe_b_router[i].astype(F32))
            wr_hi = wr.astype(BF16)
            wr_lo = (wr - wr_hi.astype(F32)).astype(BF16)
            xf, hf, slab = _merge(u, ob, oc, z, xf, *weights, router=(wr_hi, wr_lo, br))
            xf = _moe(hf, xf, slab, moe_w1[i], moe_w3[i], moe_w2[i])
    return xf.reshape(bsz, s, d).astype(x.dtype)
```

```python
import functools
import math

import numpy as np
import jax
import jax.numpy as jnp
from jax import lax
from jax.experimental import pallas as pl
from jax.experimental.pallas import tpu as pltpu

D_MODEL = 1024
HEAD_DIM = 64
CONV_CH = 512
CONV_WIDTH = 31
WB_HEADS = 8
WB_KV_HEADS = 2
WINDOW = 128
BLOCK = 128
NA_HEADS = 8
GRID_W = 64
NA_KH = 8
NA_KW = 16
REL_BUCKETS = 32
REL_MAX_DIST = 128
N_EXPERTS = 8
EPS = 1e-6
NEG_INF = -1e30

LANES = 128

ZG = 0
ZA = 3072
ZBQ = 4096
ZCQ = 4608
ZCK = 5120
ZCV = 5632
ZBK = 6144
ZBV = 6272
ZC = 6400
ZCHUNK = 512
WB_HEAD_ORDER = tuple(h for p in range(WB_HEADS // 2) for h in (p, p + WB_HEADS // 2))

VMEM_LIMIT = 56 * 1024 * 1024

F32 = jnp.float32
BF16 = jnp.bfloat16


def _cparams(grid_rank, vmem=VMEM_LIMIT):
    return pltpu.CompilerParams(dimension_semantics=("arbitrary",) * grid_rank, vmem_limit_bytes=vmem)


def _resident(shape):
    nd = len(shape)
    return pl.BlockSpec(shape, lambda *_: (0,) * nd, pipeline_mode=pl.Buffered(1))


def _dot(a, b):
    return jnp.dot(a, b, preferred_element_type=F32)


def _dot_nt(a, b):
    return lax.dot_general(a, b, (((1,), (1,)), ((), ())), preferred_element_type=F32)


def _t5_bucket_np(rel):
    half = REL_BUCKETS // 2
    max_exact = half // 2
    ret = np.where(rel > 0, half, 0)
    n = np.abs(rel)
    nf = np.maximum(n, 1).astype(np.float32)
    large = max_exact + (np.log(nf / np.float32(max_exact)) / np.float32(math.log(REL_MAX_DIST / max_exact))
                         * np.float32(half - max_exact)).astype(np.int32)
    large = np.minimum(large, half - 1)
    return ret + np.where(n < max_exact, n, large)


def _window_bucket_table():
    col = np.arange(3 * BLOCK)[None, :]
    rel = (col - BLOCK) - np.arange(BLOCK)[:, None]
    band = np.where(np.abs(rel) <= WINDOW, _t5_bucket_np(rel), -1)
    first = np.where(col >= BLOCK, band, -1)
    last = np.where(col < 2 * BLOCK, band, -1)
    return np.stack([first, band, last]).astype(np.int32)


NA_QROWS = 2
NA_KROWS = 10


def _na_tables(rows):
    kh = min(NA_KH, rows)
    col = np.arange(GRID_W)
    col_start = np.clip(col - NA_KW // 2, 0, GRID_W - NA_KW)
    kc = np.arange(GRID_W)[None, :]
    inwin = (kc >= col_start[:, None]) & (kc < col_start[:, None] + NA_KW)
    dmat = np.where(inwin, kc - col[:, None] + NA_KW - 1, -1).astype(np.int32)
    variants, variant_of_pair, kstart = [], [], []
    for i in range(rows // NA_QROWS):
        ks = int(np.clip(NA_QROWS * i - kh // 2, 0, rows - NA_KROWS))
        tab = np.full((NA_QROWS, NA_KROWS), -1, np.int64)
        for a in range(NA_QROWS):
            r = NA_QROWS * i + a
            r0 = int(np.clip(r - kh // 2, 0, rows - kh))
            assert ks <= r0 and r0 + kh <= ks + NA_KROWS
            for kr in range(NA_KROWS):
                if r0 <= ks + kr < r0 + kh:
                    tab[a, kr] = ks + kr - r + NA_KH - 1
        for v, t in enumerate(variants):
            if np.array_equal(t, tab):
                variant_of_pair.append(v)
                break
        else:
            variant_of_pair.append(len(variants))
            variants.append(tab)
        kstart.append(ks)
    return dmat, variants, variant_of_pair, kstart


def _t5_bias_kernel(tab_ref, bucket_ref, o_ref):
    h = pl.program_id(0)
    for v in range(bucket_ref.shape[0]):
        bucket = bucket_ref[v]
        acc = jnp.full(bucket.shape, NEG_INF, F32)
        for b in range(REL_BUCKETS):
            acc = jnp.where(bucket == b, tab_ref[b, h], acc)
        o_ref[0, v] = acc


def _t5_bias(t5_rel_bias):
    bucket = jnp.asarray(_window_bucket_table())
    nv = bucket.shape[0]
    return pl.pallas_call(
        _t5_bias_kernel,
        grid=(WB_HEADS,),
        in_specs=[pl.BlockSpec(memory_space=pltpu.SMEM),
                  pl.BlockSpec((nv, BLOCK, 3 * BLOCK), lambda h: (0, 0, 0))],
        out_specs=pl.BlockSpec((1, nv, BLOCK, 3 * BLOCK), lambda h: (h, 0, 0, 0)),
        out_shape=jax.ShapeDtypeStruct((WB_HEADS, nv, BLOCK, 3 * BLOCK), F32),
        compiler_params=_cparams(1),
        name="t5_bias",
    )(t5_rel_bias.astype(F32), bucket)


def _na_bias_kernel(variants, rpb_ref, dmat_ref, o_ref):
    lh = pl.program_id(0) * NA_HEADS + pl.program_id(1)
    dmat = dmat_ref[...]
    ncol = 2 * NA_KW - 1
    base = []
    for dr in range(2 * NA_KH - 1):
        acc = jnp.full(dmat.shape, NEG_INF, F32)
        for d in range(ncol):
            acc = jnp.where(dmat == d, rpb_ref[lh, dr * ncol + d], acc)
        base.append(acc)
    neg = jnp.full(dmat.shape, NEG_INF, F32)
    for v, tab in enumerate(variants):
        for a in range(NA_QROWS):
            for kr in range(NA_KROWS):
                dr = int(tab[a, kr])
                tile = neg if dr < 0 else base[dr]
                o_ref[0, v, 0, a * GRID_W:(a + 1) * GRID_W, kr * GRID_W:(kr + 1) * GRID_W] = tile


def _na_bias(na_rpb, rows):
    depth = na_rpb.shape[0]
    dmat, variants, _, _ = _na_tables(rows)
    nv = len(variants)
    rpb = jnp.transpose(na_rpb.astype(F32), (0, 3, 1, 2)).reshape(depth * NA_HEADS, -1)
    qn, kn = NA_QROWS * GRID_W, NA_KROWS * GRID_W
    return pl.pallas_call(
        functools.partial(_na_bias_kernel, variants),
        grid=(depth, NA_HEADS),
        in_specs=[pl.BlockSpec(memory_space=pltpu.SMEM),
                  pl.BlockSpec((GRID_W, GRID_W), lambda l, h: (0, 0))],
        out_specs=pl.BlockSpec((1, nv, 1, qn, kn), lambda l, h: (l, 0, h, 0, 0)),
        out_shape=jax.ShapeDtypeStruct((depth, nv, NA_HEADS, qn, kn), F32),
        compiler_params=_cparams(2),
        name="na_bias",
    )(rpb, jnp.asarray(dmat))


def _inproj_kernel(x_ref, ng_ref, w_ref, cg_ref, gb_ref, bd_ref, z_ref, h_ref):
    x = x_ref[...]
    inv = lax.rsqrt(jnp.mean(x * x, axis=-1, keepdims=True) + EPS)
    h_ref[...] = (x * inv * ng_ref[...]).astype(BF16)

    def qknorm(zc, c0):
        w = min(bd_ref.shape[0], zc.shape[1])
        sq = (zc * zc).astype(BF16)
        ms = jnp.concatenate([_dot(sq[:, j:j + w], bd_ref[:w, :w]) for j in range(0, zc.shape[1], w)], axis=1)
        return zc * lax.rsqrt(ms + EPS) * cg_ref[:, c0:c0 + zc.shape[1]]

    for c0 in range(0, ZC, ZCHUNK):
        c1 = min(c0 + ZCHUNK, ZC)
        zc = _dot(h_ref[...], w_ref[:, c0:c1])
        if c0 < ZA:
            out = jax.nn.sigmoid(zc + gb_ref[:, c0:c1])
        elif c0 in (ZBQ, ZCQ, ZCK):
            out = qknorm(zc, c0)
        elif c0 == ZBK:
            nk = ZBV - ZBK
            out = jnp.concatenate([qknorm(zc[:, :nk], c0), zc[:, nk:]], axis=1)
        else:
            out = zc
        z_ref[:, c0:c1] = out.astype(BF16)


def _inproj(x, ng, w, cg, gb, bd, tm=512):
    n = x.shape[0]
    return pl.pallas_call(
        _inproj_kernel,
        grid=(n // tm,),
        in_specs=[pl.BlockSpec((tm, D_MODEL), lambda i: (i, 0)),
                  _resident((1, D_MODEL)), _resident((D_MODEL, ZC)), _resident((1, ZC)),
                  _resident((1, ZA)), _resident(bd.shape)],
        out_specs=pl.BlockSpec((tm, ZC), lambda i: (i, 0)),
        out_shape=jax.ShapeDtypeStruct((n, ZC), BF16),
        scratch_shapes=[pltpu.VMEM((tm, D_MODEL), BF16)],
        compiler_params=_cparams(1),
        name="inproj",
    )(x, ng, w, cg, gb, bd)


SUBLANES = 8
CONV_PAD = 16
CONV_HALF = 256
CONV_ROWS = 64
CONV_SHIFT_ROWS = 104
CONV_NORM_TILES = 4


def _conv_kernel(z_ref, dw_ref, db_ref, lg_ref, lb_ref, u_ref, vs_ref, c_ref):
    s = z_ref.shape[0]
    span = s + 2 * CONV_PAD
    first = CONV_PAD - CONV_WIDTH // 2
    glu_rows = 256

    for half in range(CONV_CH // CONV_HALF):
        l0 = half * CONV_HALF
        vs_ref[0, 0:CONV_PAD, :] = jnp.zeros((CONV_PAD, CONV_HALF), F32)
        vs_ref[0, CONV_PAD + s:span + SUBLANES, :] = jnp.zeros((CONV_PAD + SUBLANES, CONV_HALF), F32)

        def glu(i, carry):
            r = pl.multiple_of(i * glu_rows, glu_rows)
            a = z_ref[pl.ds(r, glu_rows), l0:l0 + CONV_HALF].astype(F32)
            g = z_ref[pl.ds(r, glu_rows), CONV_CH + l0:CONV_CH + l0 + CONV_HALF].astype(F32)
            vs_ref[0, pl.ds(CONV_PAD + r, glu_rows), :] = a * jax.nn.sigmoid(g)
            return carry

        lax.fori_loop(0, s // glu_rows, glu, 0)

        def shift(i, carry):
            r = pl.multiple_of(i * CONV_SHIFT_ROWS, SUBLANES)
            win = vs_ref[0, pl.ds(r, CONV_SHIFT_ROWS + SUBLANES), :]
            for k in range(1, SUBLANES):
                vs_ref[k, pl.ds(r, CONV_SHIFT_ROWS), :] = win[k:k + CONV_SHIFT_ROWS]
            return carry

        assert span % CONV_SHIFT_ROWS == 0
        lax.fori_loop(0, span // CONV_SHIFT_ROWS, shift, 0)

        def conv(i, carry):
            r = pl.multiple_of(i * CONV_ROWS, CONV_ROWS)
            acc = jnp.zeros((CONV_ROWS, CONV_HALF), F32) + db_ref[:, l0:l0 + CONV_HALF]
            for w in range(CONV_WIDTH):
                k, a = (first + w) % SUBLANES, (first + w) // SUBLANES
                tap = vs_ref[k, pl.ds(r + a * SUBLANES, CONV_ROWS), :]
                acc = acc + tap * dw_ref[w:w + 1, l0:l0 + CONV_HALF]
            c_ref[pl.ds(r, CONV_ROWS), l0:l0 + CONV_HALF] = acc
            return carry

        lax.fori_loop(0, s // CONV_ROWS, conv, 0)

    def norm(i, carry):
        for t in range(CONV_NORM_TILES):
            r = pl.multiple_of((i * CONV_NORM_TILES + t) * CONV_ROWS, CONV_ROWS)
            c = c_ref[pl.ds(r, CONV_ROWS), :]
            xc = c - jnp.mean(c, axis=-1, keepdims=True)
            var = jnp.mean(xc * xc, axis=-1, keepdims=True)
            y = xc * lax.rsqrt(var + EPS) * lg_ref[...] + lb_ref[...]
            u_ref[pl.ds(r, CONV_ROWS), :] = (y * jax.nn.sigmoid(y)).astype(BF16)
        return carry

    assert s % (CONV_ROWS * CONV_NORM_TILES) == 0
    lax.fori_loop(0, s // (CONV_ROWS * CONV_NORM_TILES), norm, 0)


def _conv_mixer(z, bsz, s, dw, db, lg, lb):
    return pl.pallas_call(
        _conv_kernel,
        grid=(bsz,),
        in_specs=[pl.BlockSpec((s, 2 * CONV_CH), lambda b: (b, ZA // (2 * CONV_CH))),
                  _resident((CONV_WIDTH, CONV_CH)), _resident((1, CONV_CH)),
                  _resident((1, CONV_CH)), _resident((1, CONV_CH))],
        out_specs=pl.BlockSpec((s, CONV_CH), lambda b: (b, 0)),
        out_shape=jax.ShapeDtypeStruct((bsz * s, CONV_CH), BF16),
        scratch_shapes=[pltpu.VMEM((SUBLANES, s + 2 * CONV_PAD + SUBLANES, CONV_HALF), F32),
                        pltpu.VMEM((s, CONV_CH), F32)],
        compiler_params=_cparams(1),
        name="conv_mixer",
    )(z, dw, db, lg, lb)


def _head_masks():
    lane = lax.broadcasted_iota(jnp.int32, (1, LANES), 1)
    return lane < HEAD_DIM


ATTN_UNROLL = 8


def _softmax_pv(sc, vw, sink=None):
    m = jnp.max(sc, axis=-1, keepdims=True)
    if sink is not None:
        m = jnp.maximum(m, sink)
    p = jnp.exp(sc - m)
    l = jnp.sum(p, axis=-1, keepdims=True)
    if sink is not None:
        l = l + jnp.exp(sink - m)
    return _dot(p.astype(BF16), vw) * (1.0 / l)


def _wattn_kernel(sink_ref, q_ref, k_ref, v_ref, bias_ref, o_ref, kp_ref, vp_ref):
    hp = pl.program_id(1)
    s = q_ref.shape[0]
    nb = s // BLOCK
    lo = _head_masks()
    zero = jnp.zeros((), BF16)

    @pl.when(hp == 0)
    def _():
        pad = jnp.zeros((BLOCK, LANES), BF16)
        for src, dst in ((k_ref, kp_ref), (v_ref, vp_ref)):
            dst[0:BLOCK, :] = pad
            dst[BLOCK:BLOCK + s, :] = src[...]
            dst[BLOCK + s:2 * BLOCK + s, :] = pad

    def group(g, carry):
        rows, scores = [], []
        for u in range(ATTN_UNROLL):
            n = g * ATTN_UNROLL + u
            r = pl.multiple_of(n * BLOCK, BLOCK)
            var = jnp.where(n == 0, 0, jnp.where(n == nb - 1, 2, 1))
            q = q_ref[pl.ds(r, BLOCK), :]
            kw = kp_ref[pl.ds(r, 3 * BLOCK), :]
            rows.append(r)
            for hh in range(2):
                qm = jnp.where(lo if hh == 0 else jnp.logical_not(lo), q, zero)
                scores.append(_dot_nt(qm, kw) + bias_ref[hh, var])
        for u, r in enumerate(rows):
            vw = vp_ref[pl.ds(r, 3 * BLOCK), :]
            outs = [_softmax_pv(scores[2 * u + hh], vw, sink_ref[0, hp + hh * (WB_HEADS // 2)]) for hh in range(2)]
            o_ref[pl.ds(r, BLOCK), :] = jnp.where(lo, outs[0], outs[1]).astype(BF16)
        return carry

    lax.fori_loop(0, nb // ATTN_UNROLL, group, 0)


def _window_attention(z, bsz, s, sink, bias):
    npair = WB_HEADS // 2
    nv = bias.shape[1]
    assert WB_KV_HEADS * HEAD_DIM == LANES and (s // BLOCK) % ATTN_UNROLL == 0 and s // BLOCK >= 2
    return pl.pallas_call(
        _wattn_kernel,
        grid=(bsz, npair),
        in_specs=[pl.BlockSpec(memory_space=pltpu.SMEM),
                  pl.BlockSpec((s, LANES), lambda b, hp: (b, ZBQ // LANES + hp)),
                  pl.BlockSpec((s, LANES), lambda b, hp: (b, ZBK // LANES)),
                  pl.BlockSpec((s, LANES), lambda b, hp: (b, ZBV // LANES)),
                  pl.BlockSpec((2, nv, BLOCK, 3 * BLOCK), lambda b, hp: (hp, 0, 0, 0))],
        out_specs=pl.BlockSpec((s, LANES), lambda b, hp: (b, hp)),
        out_shape=jax.ShapeDtypeStruct((bsz * s, WB_HEADS * HEAD_DIM), BF16),
        scratch_shapes=[pltpu.VMEM((s + 2 * BLOCK, LANES), BF16), pltpu.VMEM((s + 2 * BLOCK, LANES), BF16)],
        compiler_params=_cparams(2),
        name="window_attention",
    )(sink, z, z, z, bias)


def _nattn_kernel(n_edge, n_pairs, q_ref, k_ref, v_ref, bias_ref, o_ref):
    lo = _head_masks()
    zero = jnp.zeros((), BF16)
    qn, kn = NA_QROWS * GRID_W, NA_KROWS * GRID_W
    max_ks = k_ref.shape[0] // GRID_W - NA_KROWS

    def group(g, carry):
        rows, scores = [], []
        for u in range(ATTN_UNROLL):
            i = g * ATTN_UNROLL + u
            r = pl.multiple_of(i * qn, qn)
            ks = jnp.clip(NA_QROWS * i - NA_KH // 2, 0, max_ks)
            rk = pl.multiple_of(ks * GRID_W, LANES)
            var = jnp.where(i < n_edge, i,
                            jnp.where(i >= n_pairs - n_edge, i - (n_pairs - 2 * n_edge - 1), n_edge))
            q = q_ref[pl.ds(r, qn), :]
            kw = k_ref[pl.ds(rk, kn), :]
            rows.append((r, rk))
            for hh in range(2):
                qm = jnp.where(lo if hh == 0 else jnp.logical_not(lo), q, zero)
                scores.append(_dot_nt(qm, kw) + bias_ref[0, var, hh])
        for u, (r, rk) in enumerate(rows):
            vw = v_ref[pl.ds(rk, kn), :]
            outs = [_softmax_pv(scores[2 * u + hh], vw) for hh in range(2)]
            o_ref[pl.ds(r, qn), :] = jnp.where(lo, outs[0], outs[1]).astype(BF16)
        return carry

    lax.fori_loop(0, n_pairs // ATTN_UNROLL, group, 0)


def _neighbourhood_attention(z, bsz, s, bias, layer):
    rows = s // GRID_W
    _, variants, variant_of_pair, kstart = _na_tables(rows)
    n_pairs = rows // NA_QROWS
    nv = len(variants)
    n_edge = (nv - 1) // 2
    assert n_pairs % ATTN_UNROLL == 0
    for i in range(n_pairs):
        want = i if i < n_edge else (i - (n_pairs - 2 * n_edge - 1) if i >= n_pairs - n_edge else n_edge)
        assert variant_of_pair[i] == want
        assert kstart[i] == int(np.clip(NA_QROWS * i - NA_KH // 2, 0, rows - NA_KROWS))
        assert (kstart[i] * GRID_W) % LANES == 0
    npair = NA_HEADS // 2
    qn, kn = NA_QROWS * GRID_W, NA_KROWS * GRID_W
    return pl.pallas_call(
        functools.partial(_nattn_kernel, n_edge, n_pairs),
        grid=(npair, bsz),
        in_specs=[pl.BlockSpec((s, LANES), lambda hp, b: (b, ZCQ // LANES + hp)),
                  pl.BlockSpec((s, LANES), lambda hp, b: (b, ZCK // LANES + hp)),
                  pl.BlockSpec((s, LANES), lambda hp, b: (b, ZCV // LANES + hp)),
                  pl.BlockSpec((1, nv, 2, qn, kn), lambda hp, b: (layer, 0, hp, 0, 0))],
        out_specs=pl.BlockSpec((s, LANES), lambda hp, b: (b, hp)),
        out_shape=jax.ShapeDtypeStruct((bsz * s, NA_HEADS * HEAD_DIM), BF16),
        compiler_params=_cparams(2),
        name="neighbourhood_attention",
    )(z, z, z, bias)


ROUTE_IDX_LANE = N_EXPERTS
ROUTE_W_LANE = N_EXPERTS + 2


def _merge_kernel(with_router, u_ref, ob_ref, oc_ref, g_ref, x_ref, wa_ref, wb_ref, wc_ref, wo_ref,
                  fg_ref, *rest):
    if with_router:
        wrh_ref, wrl_ref, br_ref, xo_ref, hf_ref, slab_ref = rest
    else:
        xo_ref, hf_ref = rest
    d = D_MODEL
    m = g_ref[:, 0:d].astype(F32) * _dot(u_ref[...], wa_ref[...])
    m = m + g_ref[:, d:2 * d].astype(F32) * _dot(ob_ref[...], wb_ref[...])
    m = m + g_ref[:, 2 * d:3 * d].astype(F32) * _dot(oc_ref[...], wc_ref[...])
    xn = x_ref[...] + _dot(m.astype(BF16), wo_ref[...])
    xo_ref[...] = xn
    hf = xn * lax.rsqrt(jnp.mean(xn * xn, axis=-1, keepdims=True) + EPS) * fg_ref[...]
    hf_ref[...] = hf.astype(hf_ref.dtype)
    if with_router:
        hf_hi = hf.astype(BF16)
        hf_lo = (hf - hf_hi.astype(F32)).astype(BF16)
        logits = (_dot(hf_hi, wrh_ref[...]) + _dot(hf_lo, wrh_ref[...]) + _dot(hf_hi, wrl_ref[...])
                  + br_ref[...])
        lane = lax.broadcasted_iota(jnp.int32, logits.shape, 1)
        m1 = jnp.max(logits, axis=-1, keepdims=True)
        i1 = jnp.min(jnp.where(logits == m1, lane, LANES), axis=-1, keepdims=True)
        rest_l = jnp.where(lane == i1, -jnp.inf, logits)
        m2 = jnp.max(rest_l, axis=-1, keepdims=True)
        i2 = jnp.min(jnp.where(rest_l == m2, lane, LANES), axis=-1, keepdims=True)
        e2 = jnp.exp(m2 - m1)
        w1 = 1.0 / (1.0 + e2)
        w2 = e2 / (1.0 + e2)
        slab = jnp.where(lane == ROUTE_IDX_LANE, i1.astype(F32), 0.0)
        slab = jnp.where(lane == ROUTE_IDX_LANE + 1, i2.astype(F32), slab)
        slab = jnp.where(lane == ROUTE_W_LANE, w1, slab)
        slab = jnp.where(lane == ROUTE_W_LANE + 1, w2, slab)
        slab_ref[...] = slab


def _merge(u, ob, oc, z, x, wa, wb, wc, wo, fg, router=None, tm=512):
    n = x.shape[0]
    with_router = router is not None
    row = lambda w: pl.BlockSpec((tm, w), lambda i: (i, 0))
    in_specs = [row(CONV_CH), row(WB_HEADS * HEAD_DIM), row(NA_HEADS * HEAD_DIM),
                pl.BlockSpec((tm, ZA), lambda i: (i, 0)), row(D_MODEL),
                _resident(wa.shape), _resident(wb.shape), _resident(wc.shape), _resident(wo.shape),
                _resident((1, D_MODEL))]
    args = [u, ob, oc, z, x, wa, wb, wc, wo, fg]
    out_specs = [row(D_MODEL), row(D_MODEL)]
    out_shape = [jax.ShapeDtypeStruct((n, D_MODEL), F32),
                 jax.ShapeDtypeStruct((n, D_MODEL), F32 if with_router else BF16)]
    if with_router:
        in_specs += [_resident((D_MODEL, LANES)), _resident((D_MODEL, LANES)), _resident((1, LANES))]
        args += list(router)
        out_specs.append(row(LANES))
        out_shape.append(jax.ShapeDtypeStruct((n, LANES), F32))
    return pl.pallas_call(
        functools.partial(_merge_kernel, with_router),
        grid=(n // tm,),
        in_specs=in_specs, out_specs=out_specs, out_shape=out_shape,
        compiler_params=_cparams(1),
        name="merge_router" if with_router else "merge",
    )(*args)


FFN_CHUNK = 1024


def _ffn_kernel(hf_ref, x_ref, w1_ref, w3_ref, w2_ref, o_ref):
    dff = w1_ref.shape[1]
    hf = hf_ref[...]
    acc = x_ref[...]
    for c0 in range(0, dff, FFN_CHUNK):
        c1 = min(c0 + FFN_CHUNK, dff)
        a = _dot(hf, w1_ref[:, c0:c1])
        b = _dot(hf, w3_ref[:, c0:c1])
        hm = (a * jax.nn.sigmoid(a) * b).astype(BF16)
        acc = acc + _dot(hm, w2_ref[c0:c1, :])
    o_ref[...] = acc


def _dense_ffn(hf, x, w1, w3, w2, tm=512):
    n = x.shape[0]
    row = pl.BlockSpec((tm, D_MODEL), lambda i: (i, 0))
    return pl.pallas_call(
        _ffn_kernel,
        grid=(n // tm,),
        in_specs=[row, row, _resident(w1.shape), _resident(w3.shape), _resident(w2.shape)],
        out_specs=row,
        out_shape=jax.ShapeDtypeStruct((n, D_MODEL), F32),
        input_output_aliases={1: 0},
        compiler_params=_cparams(1),
        name="dense_ffn",
    )(hf, x, w1, w3, w2)


MOE_TILE = 1024
MOE_FF_CHUNK = 1792
ROUTE_ROWS = 512


def _row_copy(src, dst, sem):
    return pltpu.make_async_copy(src, dst, sem)


def _dispatch_kernel(dest_ref, pad_base_ref, pad_cnt_ref, hf_ref, xs_ref, buf_ref, zrow_ref, load_sem,
                     row_sem, zero_sem, zero_tile_sem):
    i = pl.program_id(0)
    last = pl.num_programs(0) - 1
    slot = i % 2

    def load(step, slot_):
        src = hf_ref.at[pl.ds(pl.multiple_of(step * ROUTE_ROWS, ROUTE_ROWS), ROUTE_ROWS)]
        return pltpu.make_async_copy(src, buf_ref.at[slot_], load_sem.at[slot_])

    def drain(slot_):
        for _ in range(2 * ROUTE_ROWS):
            _row_copy(buf_ref.at[slot_, pl.ds(0, 1)], xs_ref.at[pl.ds(0, 1)], row_sem.at[slot_]).wait()

    def zero_row(dst_row):
        return _row_copy(zrow_ref.at[pl.ds(0, 1)], xs_ref.at[pl.ds(dst_row, 1)], zero_sem)

    def zero_rows(dst_row):
        dst = xs_ref.at[pl.ds(pl.multiple_of(dst_row, SUBLANES), SUBLANES)]
        return _row_copy(zrow_ref, dst, zero_tile_sem)

    unused = N_EXPERTS

    @pl.when(i == 0)
    def _():
        load(0, 0).start()
        zrow_ref[...] = jnp.zeros_like(zrow_ref)
        for e in range(N_EXPERTS):
            def zstart(j, carry):
                zero_row(pad_base_ref[e] + j).start()
                return carry

            lax.fori_loop(0, pad_cnt_ref[e], zstart, 0)

        def ztile(j, carry):
            zero_rows(pad_base_ref[unused] + j * SUBLANES).start()
            return carry

        lax.fori_loop(0, pad_cnt_ref[unused] // SUBLANES, ztile, 0)

    load(i, slot).wait()

    @pl.when(i > 0)
    def _():
        drain(1 - slot)

    @pl.when(i < last)
    def _():
        load(i + 1, 1 - slot).start()

    def start(r, carry):
        for k in range(2):
            d = dest_ref[0, 0, k * ROUTE_ROWS + r]
            _row_copy(buf_ref.at[slot, pl.ds(r, 1)], xs_ref.at[pl.ds(d, 1)], row_sem.at[slot]).start(priority=k)
        return carry

    lax.fori_loop(0, ROUTE_ROWS, start, 0, unroll=8)

    @pl.when(i == last)
    def _():
        drain(slot)
        for e in range(N_EXPERTS):
            def zwait(j, carry):
                zero_row(0).wait()
                return carry

            lax.fori_loop(0, pad_cnt_ref[e], zwait, 0)

        def ztile_wait(j, carry):
            zero_rows(0).wait()
            return carry

        lax.fori_loop(0, pad_cnt_ref[unused] // SUBLANES, ztile_wait, 0)


def _dispatch(hf, dest, pad_base, pad_cnt, n_rows):
    n = hf.shape[0]
    steps = n // ROUTE_ROWS
    smem = pl.BlockSpec(memory_space=pltpu.SMEM)
    return pl.pallas_call(
        _dispatch_kernel,
        grid=(steps,),
        in_specs=[pl.BlockSpec((1, 1, 2 * ROUTE_ROWS), lambda i: (i, 0, 0), memory_space=pltpu.SMEM),
                  smem, smem, pl.BlockSpec(memory_space=pl.ANY)],
        out_specs=pl.BlockSpec(memory_space=pl.ANY),
        out_shape=jax.ShapeDtypeStruct((n_rows, D_MODEL), F32),
        scratch_shapes=[pltpu.VMEM((2, ROUTE_ROWS, D_MODEL), F32), pltpu.VMEM((SUBLANES, D_MODEL), F32),
                        pltpu.SemaphoreType.DMA((2,)), pltpu.SemaphoreType.DMA((2,)),
                        pltpu.SemaphoreType.DMA(()), pltpu.SemaphoreType.DMA(())],
        compiler_params=_cparams(1),
        name="moe_dispatch",
    )(dest, pad_base, pad_cnt, hf)


F8 = jnp.float8_e4m3fn
F32_EXP_BIAS = 127
F32_MANT_BITS = 23
F8_TOP_EXP = 7


def _pow2_scale(amax):
    e = jnp.right_shift(lax.bitcast_convert_type(amax, jnp.int32), F32_MANT_BITS) & 0xFF
    se = jnp.clip(2 * F32_EXP_BIAS + F8_TOP_EXP - e, 1, 2 * F32_EXP_BIAS)
    scale = lax.bitcast_convert_type(jnp.left_shift(se, F32_MANT_BITS), F32)
    inv = lax.bitcast_convert_type(jnp.left_shift(2 * F32_EXP_BIAS - se, F32_MANT_BITS), F32)
    return scale, inv


def _quantize_rows(v):
    scale, inv = _pow2_scale(jnp.max(jnp.abs(v), axis=-1, keepdims=True))
    return (v * scale).astype(F8), inv


def _experts_kernel(te_ref, na_ref, winv_ref, xs_ref, w1_ref, w3_ref, w2_ref, ys_ref):
    i = pl.program_id(0)
    active = i < na_ref[0]
    e = te_ref[i]
    dff = w1_ref.shape[2]
    half = MOE_TILE // 2

    @pl.when(active)
    def _():
        xq = [_quantize_rows(xs_ref[h * half:(h + 1) * half, :]) for h in range(2)]
        for c0 in range(0, dff, MOE_FF_CHUNK):
            cols = slice(c0, c0 + MOE_FF_CHUNK)
            gates = [(_dot(xb, w1_ref[0, :, cols]) * (xinv * winv_ref[e, 0]), _dot(xb, w3_ref[0, :, cols]))
                     for xb, xinv in xq]
            for h, (a, b) in enumerate(gates):
                hm, hinv = _quantize_rows(a * jax.nn.sigmoid(a) * b)
                y = _dot(hm, w2_ref[0, cols, :]) * (hinv * xq[h][1] * (winv_ref[e, 1] * winv_ref[e, 2]))
                rows = slice(h * half, (h + 1) * half)
                if c0 == 0:
                    ys_ref[rows, :] = y
                else:
                    ys_ref[rows, :] += y

    @pl.when(jnp.logical_not(active))
    def _():
        ys_ref[...] = jnp.zeros_like(ys_ref)


def _quantize_expert_weights(w):
    scale, inv = _pow2_scale(jnp.max(jnp.abs(w.astype(F32)), axis=(1, 2)))
    return (w * scale[:, None, None]).astype(F8), inv


def _experts(xs, tile_expert, n_active, w1, w3, w2):
    n_rows = xs.shape[0]
    (w1, i1), (w3, i3), (w2, i2) = (_quantize_expert_weights(w) for w in (w1, w3, w2))
    winv = jnp.stack([i1, i3, i2], axis=1)
    dff = w1.shape[2]
    assert dff % MOE_FF_CHUNK == 0
    grid_spec = pltpu.PrefetchScalarGridSpec(
        num_scalar_prefetch=2,
        grid=(n_rows // MOE_TILE,),
        in_specs=[pl.BlockSpec(memory_space=pltpu.SMEM),
                  pl.BlockSpec((MOE_TILE, D_MODEL), lambda i, te, na: (jnp.minimum(i, na[0] - 1), 0)),
                  pl.BlockSpec((1, D_MODEL, dff), lambda i, te, na: (te[i], 0, 0)),
                  pl.BlockSpec((1, D_MODEL, dff), lambda i, te, na: (te[i], 0, 0)),
                  pl.BlockSpec((1, dff, D_MODEL), lambda i, te, na: (te[i], 0, 0))],
        out_specs=pl.BlockSpec((MOE_TILE, D_MODEL), lambda i, te, na: (i, 0)),
    )
    return pl.pallas_call(
        _experts_kernel,
        grid_spec=grid_spec,
        out_shape=jax.ShapeDtypeStruct((n_rows, D_MODEL), F32),
        compiler_params=_cparams(1),
        name="moe_experts",
    )(tile_expert, n_active, winv, xs, w1, w3, w2)


def _combine_kernel(dest_ref, next_dest_ref, x_ref, slab_ref, ys_ref, o_ref, y_ref, sem):
    i = pl.program_id(0)
    slot = i % 2

    def gather(d_ref, slot_):
        def start(r, carry):
            for k in range(2):
                d = d_ref[0, 0, k * ROUTE_ROWS + r]
                _row_copy(ys_ref.at[pl.ds(d, 1)], y_ref.at[slot_, k, pl.ds(r, 1)], sem.at[slot_]).start()
            return carry

        lax.fori_loop(0, ROUTE_ROWS, start, 0, unroll=8)

    @pl.when(i == 0)
    def _():
        gather(dest_ref, 0)

    @pl.when(i + 1 < pl.num_programs(0))
    def _():
        gather(next_dest_ref, 1 - slot)

    for _ in range(2 * ROUTE_ROWS):
        _row_copy(ys_ref.at[pl.ds(0, 1)], y_ref.at[slot, 0, pl.ds(0, 1)], sem.at[slot]).wait()
    w0 = slab_ref[:, ROUTE_W_LANE:ROUTE_W_LANE + 1]
    w1 = slab_ref[:, ROUTE_W_LANE + 1:ROUTE_W_LANE + 2]
    o_ref[...] = x_ref[...] + w0 * y_ref[slot, 0] + w1 * y_ref[slot, 1]


def _combine(x, slab, dest, ys):
    n = x.shape[0]
    steps = n // ROUTE_ROWS
    row = pl.BlockSpec((ROUTE_ROWS, D_MODEL), lambda i: (i, 0))
    dest_block = (1, 1, 2 * ROUTE_ROWS)
    return pl.pallas_call(
        _combine_kernel,
        grid=(steps,),
        in_specs=[pl.BlockSpec(dest_block, lambda i: (i, 0, 0), memory_space=pltpu.SMEM),
                  pl.BlockSpec(dest_block, lambda i: (jnp.minimum(i + 1, steps - 1), 0, 0),
                               memory_space=pltpu.SMEM),
                  row, pl.BlockSpec((ROUTE_ROWS, LANES), lambda i: (i, 0)),
                  pl.BlockSpec(memory_space=pl.ANY)],
        out_specs=row,
        out_shape=jax.ShapeDtypeStruct((n, D_MODEL), F32),
        scratch_shapes=[pltpu.VMEM((2, 2, ROUTE_ROWS, D_MODEL), F32), pltpu.SemaphoreType.DMA((2,))],
        input_output_aliases={2: 0},
        compiler_params=_cparams(1),
        name="moe_combine",
    )(dest, dest, x, slab, ys)


def _routing_plan(slab, n_rows):
    n = slab.shape[0]
    idx = slab[:, ROUTE_IDX_LANE:ROUTE_IDX_LANE + 2].astype(jnp.int32).T
    experts = jnp.arange(N_EXPERTS, dtype=jnp.int32)[:, None]
    chosen = [idx[k][None, :] == experts for k in range(2)]
    onehot = chosen[0].astype(jnp.int32) + chosen[1].astype(jnp.int32)
    csum = jnp.cumsum(onehot, axis=1)
    counts = csum[:, -1]
    padded = ((counts + MOE_TILE - 1) // MOE_TILE) * MOE_TILE
    ends = jnp.cumsum(padded)
    offs = ends - padded
    row = offs[:, None] + csum - onehot
    dest = jnp.stack([jnp.sum(jnp.where(c, row, 0), axis=0) for c in chosen])
    steps = n // ROUTE_ROWS
    dest = dest.reshape(2, steps, ROUTE_ROWS).transpose(1, 0, 2).reshape(steps, 1, 2 * ROUTE_ROWS)
    tile_start = jnp.arange(n_rows // MOE_TILE, dtype=jnp.int32) * MOE_TILE
    tile_expert = jnp.minimum(jnp.sum(tile_start[:, None] >= ends[None, :], axis=1), N_EXPERTS - 1)
    n_active = (ends[-1] // MOE_TILE).reshape(1)
    i32 = lambda v: v.astype(jnp.int32)
    pad_base = jnp.concatenate([offs + counts, ends[-1:]])
    pad_cnt = jnp.concatenate([padded - counts, n_rows - ends[-1:]])
    return i32(dest), i32(tile_expert), i32(n_active), i32(pad_base), i32(pad_cnt)


def _moe(hf, x, slab, w1, w3, w2):
    n = x.shape[0]
    n_rows = 2 * n + N_EXPERTS * MOE_TILE
    dest, tile_expert, n_active, pad_base, pad_cnt = _routing_plan(slab, n_rows)
    xs = _dispatch(hf, dest, pad_base, pad_cnt, n_rows)
    ys = _experts(xs, tile_expert, n_active, w1, w3, w2)
    return _combine(x, slab, dest, ys)


def _prep_inproj(w_in_l, gate_b_l, qn_b, kn_b, qn_c, kn_c):
    a_in = 2 * CONV_CH
    b_q = WB_HEADS * HEAD_DIM
    b_kv = WB_KV_HEADS * HEAD_DIM
    c_w = NA_HEADS * HEAD_DIM
    o = 0
    wa = w_in_l[:, o:o + a_in]; o += a_in
    wbq = w_in_l[:, o:o + b_q]; o += b_q
    wbk = w_in_l[:, o:o + b_kv]; o += b_kv
    wbv = w_in_l[:, o:o + b_kv]; o += b_kv
    wcq = w_in_l[:, o:o + c_w]; o += c_w
    wck = w_in_l[:, o:o + c_w]; o += c_w
    wcv = w_in_l[:, o:o + c_w]; o += c_w
    wg = w_in_l[:, o:]

    wbq = wbq.reshape(D_MODEL, WB_HEADS, HEAD_DIM)[:, WB_HEAD_ORDER, :].reshape(D_MODEL, b_q)
    w = jnp.concatenate([wg, wa, wbq, wcq, wck, wcv, wbk, wbv], axis=1).astype(BF16)
    scale = HEAD_DIM ** -0.5
    ones = lambda k: jnp.ones((k,), F32)
    cg = jnp.concatenate([
        ones(ZBQ), jnp.tile(qn_b.astype(F32), WB_HEADS) * scale,
        jnp.tile(qn_c.astype(F32), NA_HEADS) * scale, jnp.tile(kn_c.astype(F32), NA_HEADS), ones(ZBK - ZCV),
        jnp.tile(kn_b.astype(F32), WB_KV_HEADS), ones(ZC - ZBV)])
    return w, cg.reshape(1, ZC), gate_b_l.astype(F32).reshape(1, -1)


MXU_TILE = 256


def _segment_mean_matrix():
    idx = np.arange(MXU_TILE) // HEAD_DIM
    return jnp.asarray((idx[:, None] == idx[None, :]).astype(np.float32) / HEAD_DIM, dtype=BF16)


def kernel(x, t5_rel_bias, attn_norm_g, w_in, gate_b, conv_dw_w, conv_dw_b, conv_ln_g, conv_ln_b,
           conv_w_out, wb_qn_g, wb_kn_g, wb_sink, wb_w_out, na_qn_g, na_kn_g, na_rpb, na_w_out, w_o,
           ffn_norm_g, ffn_w1, ffn_w3, ffn_w2, moe_w_router, moe_b_router, moe_w1, moe_w3, moe_w2):
    bsz, s, d = x.shape
    depth = w_in.shape[0]
    assert d == D_MODEL and s % BLOCK == 0 and s % (NA_QROWS * GRID_W) == 0
    n = bsz * s
    xf = x.reshape(n, d).astype(F32)
    row = lambda v: v.astype(F32).reshape(1, -1)

    t5_bias = _t5_bias(t5_rel_bias)[np.asarray(WB_HEAD_ORDER)]
    head_rows = lambda w: w.reshape(WB_HEADS, HEAD_DIM, -1)[np.asarray(WB_HEAD_ORDER)].reshape(w.shape)
    na_bias = _na_bias(na_rpb, s // GRID_W)
    bd = _segment_mean_matrix()

    for layer in range(depth):
        w, cg, gb = _prep_inproj(w_in[layer], gate_b[layer], wb_qn_g[layer], wb_kn_g[layer],
                                 na_qn_g[layer], na_kn_g[layer])
        z = _inproj(xf, row(attn_norm_g[layer]), w, cg, gb, bd)
        u = _conv_mixer(z, bsz, s, conv_dw_w[layer].astype(F32), row(conv_dw_b[layer]),
                        row(conv_ln_g[layer]), row(conv_ln_b[layer]))
        ob = _window_attention(z, bsz, s, row(wb_sink[layer]), t5_bias)
        oc = _neighbourhood_attention(z, bsz, s, na_bias, layer)
        weights = (conv_w_out[layer].astype(BF16), head_rows(wb_w_out[layer]).astype(BF16),
                   na_w_out[layer].astype(BF16), w_o[layer].astype(BF16), row(ffn_norm_g[layer]))
        i = layer // 2
        if layer % 2 == 0:
            xf, hf = _merge(u, ob, oc, z, xf, *weights)
            xf = _dense_ffn(hf, xf, ffn_w1[i].astype(BF16), ffn_w3[i].astype(BF16), ffn_w2[i].astype(BF16))
        else:
            wr = jnp.zeros((D_MODEL, LANES), F32).at[:, :N_EXPERTS].set(moe_w_router[i].astype(F32))
            br = jnp.full((1, LANES), -jnp.inf, F32).at[0, :N_EXPERTS].set(moe_b_router[i].astype(F32))
            wr_hi = wr.astype(BF16)
            wr_lo = (wr - wr_hi.astype(F32)).astype(BF16)
            xf, hf, slab = _merge(u, ob, oc, z, xf, *weights, router=(wr_hi, wr_lo, br))
            xf = _moe(hf, xf, slab, moe_w1[i], moe_w3[i], moe_w2[i])
    return xf.reshape(bsz, s, d).astype(x.dtype)
```

```python
import functools
import math

import numpy as np
import jax
import jax.numpy as jnp
from jax import lax
from jax.experimental import pallas as pl
from jax.experimental.pallas import tpu as pltpu

D_MODEL = 1024
HEAD_DIM = 64
CONV_CH = 512
CONV_WIDTH = 31
WB_HEADS = 8
WB_KV_HEADS = 2
WINDOW = 128
BLOCK = 128
NA_HEADS = 8
GRID_W = 64
NA_KH = 8
NA_KW = 16
REL_BUCKETS = 32
REL_MAX_DIST = 128
N_EXPERTS = 8
EPS = 1e-6
NEG_INF = -1e30

LANES = 128

ZG = 0
ZA = 3072
ZBQ = 4096
ZCQ = 4608
ZCK = 5120
ZCV = 5632
ZBK = 6144
ZBV = 6272
ZC = 6400
ZCHUNK = 512
WB_HEAD_ORDER = tuple(h for p in range(WB_HEADS // 2) for h in (p, p + WB_HEADS // 2))

VMEM_LIMIT = 56 * 1024 * 1024

F32 = jnp.float32
BF16 = jnp.bfloat16


def _cparams(grid_rank, vmem=VMEM_LIMIT):
    return pltpu.CompilerParams(dimension_semantics=("arbitrary",) * grid_rank, vmem_limit_bytes=vmem)


def _resident(shape):
    nd = len(shape)
    return pl.BlockSpec(shape, lambda *_: (0,) * nd, pipeline_mode=pl.Buffered(1))


def _dot(a, b):
    return jnp.dot(a, b, preferred_element_type=F32)


def _dot_nt(a, b):
    return lax.dot_general(a, b, (((1,), (1,)), ((), ())), preferred_element_type=F32)


def _t5_bucket_np(rel):
    half = REL_BUCKETS // 2
    max_exact = half // 2
    ret = np.where(rel > 0, half, 0)
    n = np.abs(rel)
    nf = np.maximum(n, 1).astype(np.float32)
    large = max_exact + (np.log(nf / np.float32(max_exact)) / np.float32(math.log(REL_MAX_DIST / max_exact))
                         * np.float32(half - max_exact)).astype(np.int32)
    large = np.minimum(large, half - 1)
    return ret + np.where(n < max_exact, n, large)


def _window_bucket_table():
    col = np.arange(3 * BLOCK)[None, :]
    rel = (col - BLOCK) - np.arange(BLOCK)[:, None]
    band = np.where(np.abs(rel) <= WINDOW, _t5_bucket_np(rel), -1)
    first = np.where(col >= BLOCK, band, -1)
    last = np.where(col < 2 * BLOCK, band, -1)
    return np.stack([first, band, last]).astype(np.int32)


NA_QROWS = 2
NA_KROWS = 10


def _na_tables(rows):
    kh = min(NA_KH, rows)
    col = np.arange(GRID_W)
    col_start = np.clip(col - NA_KW // 2, 0, GRID_W - NA_KW)
    kc = np.arange(GRID_W)[None, :]
    inwin = (kc >= col_start[:, None]) & (kc < col_start[:, None] + NA_KW)
    dmat = np.where(inwin, kc - col[:, None] + NA_KW - 1, -1).astype(np.int32)
    variants, variant_of_pair, kstart = [], [], []
    for i in range(rows // NA_QROWS):
        ks = int(np.clip(NA_QROWS * i - kh // 2, 0, rows - NA_KROWS))
        tab = np.full((NA_QROWS, NA_KROWS), -1, np.int64)
        for a in range(NA_QROWS):
            r = NA_QROWS * i + a
            r0 = int(np.clip(r - kh // 2, 0, rows - kh))
            assert ks <= r0 and r0 + kh <= ks + NA_KROWS
            for kr in range(NA_KROWS):
                if r0 <= ks + kr < r0 + kh:
                    tab[a, kr] = ks + kr - r + NA_KH - 1
        for v, t in enumerate(variants):
            if np.array_equal(t, tab):
                variant_of_pair.append(v)
                break
        else:
            variant_of_pair.append(len(variants))
            variants.append(tab)
        kstart.append(ks)
    return dmat, variants, variant_of_pair, kstart


def _t5_bias_kernel(tab_ref, bucket_ref, o_ref):
    h = pl.program_id(0)
    for v in range(bucket_ref.shape[0]):
        bucket = bucket_ref[v]
        acc = jnp.full(bucket.shape, NEG_INF, F32)
        for b in range(REL_BUCKETS):
            acc = jnp.where(bucket == b, tab_ref[b, h], acc)
        o_ref[0, v] = acc


def _t5_bias(t5_rel_bias):
    bucket = jnp.asarray(_window_bucket_table())
    nv = bucket.shape[0]
    return pl.pallas_call(
        _t5_bias_kernel,
        grid=(WB_HEADS,),
        in_specs=[pl.BlockSpec(memory_space=pltpu.SMEM),
                  pl.BlockSpec((nv, BLOCK, 3 * BLOCK), lambda h: (0, 0, 0))],
        out_specs=pl.BlockSpec((1, nv, BLOCK, 3 * BLOCK), lambda h: (h, 0, 0, 0)),
        out_shape=jax.ShapeDtypeStruct((WB_HEADS, nv, BLOCK, 3 * BLOCK), F32),
        compiler_params=_cparams(1),
        name="t5_bias",
    )(t5_rel_bias.astype(F32), bucket)


def _na_bias_kernel(variants, rpb_ref, dmat_ref, o_ref):
    lh = pl.program_id(0) * NA_HEADS + pl.program_id(1)
    dmat = dmat_ref[...]
    ncol = 2 * NA_KW - 1
    base = []
    for dr in range(2 * NA_KH - 1):
        acc = jnp.full(dmat.shape, NEG_INF, F32)
        for d in range(ncol):
            acc = jnp.where(dmat == d, rpb_ref[lh, dr * ncol + d], acc)
        base.append(acc)
    neg = jnp.full(dmat.shape, NEG_INF, F32)
    for v, tab in enumerate(variants):
        for a in range(NA_QROWS):
            for kr in range(NA_KROWS):
                dr = int(tab[a, kr])
                tile = neg if dr < 0 else base[dr]
                o_ref[0, v, 0, a * GRID_W:(a + 1) * GRID_W, kr * GRID_W:(kr + 1) * GRID_W] = tile


def _na_bias(na_rpb, rows):
    depth = na_rpb.shape[0]
    dmat, variants, _, _ = _na_tables(rows)
    nv = len(variants)
    rpb = jnp.transpose(na_rpb.astype(F32), (0, 3, 1, 2)).reshape(depth * NA_HEADS, -1)
    qn, kn = NA_QROWS * GRID_W, NA_KROWS * GRID_W
    return pl.pallas_call(
        functools.partial(_na_bias_kernel, variants),
        grid=(depth, NA_HEADS),
        in_specs=[pl.BlockSpec(memory_space=pltpu.SMEM),
                  pl.BlockSpec((GRID_W, GRID_W), lambda l, h: (0, 0))],
        out_specs=pl.BlockSpec((1, nv, 1, qn, kn), lambda l, h: (l, 0, h, 0, 0)),
        out_shape=jax.ShapeDtypeStruct((depth, nv, NA_HEADS, qn, kn), F32),
        compiler_params=_cparams(2),
        name="na_bias",
    )(rpb, jnp.asarray(dmat))


def _inproj_kernel(x_ref, ng_ref, w_ref, cg_ref, gb_ref, bd_ref, z_ref, h_ref):
    x = x_ref[...]
    inv = lax.rsqrt(jnp.mean(x * x, axis=-1, keepdims=True) + EPS)
    h_ref[...] = (x * inv * ng_ref[...]).astype(BF16)

    def qknorm(zc, c0):
        w = min(bd_ref.shape[0], zc.shape[1])
        sq = (zc * zc).astype(BF16)
        ms = jnp.concatenate([_dot(sq[:, j:j + w], bd_ref[:w, :w]) for j in range(0, zc.shape[1], w)], axis=1)
        return zc * lax.rsqrt(ms + EPS) * cg_ref[:, c0:c0 + zc.shape[1]]

    for c0 in range(0, ZC, ZCHUNK):
        c1 = min(c0 + ZCHUNK, ZC)
        zc = _dot(h_ref[...], w_ref[:, c0:c1])
        if c0 < ZA:
            out = jax.nn.sigmoid(zc + gb_ref[:, c0:c1])
        elif c0 in (ZBQ, ZCQ, ZCK):
            out = qknorm(zc, c0)
        elif c0 == ZBK:
            nk = ZBV - ZBK
            out = jnp.concatenate([qknorm(zc[:, :nk], c0), zc[:, nk:]], axis=1)
        else:
            out = zc
        z_ref[:, c0:c1] = out.astype(BF16)


def _inproj(x, ng, w, cg, gb, bd, tm=512):
    n = x.shape[0]
    return pl.pallas_call(
        _inproj_kernel,
        grid=(n // tm,),
        in_specs=[pl.BlockSpec((tm, D_MODEL), lambda i: (i, 0)),
                  _resident((1, D_MODEL)), _resident((D_MODEL, ZC)), _resident((1, ZC)),
                  _resident((1, ZA)), _resident(bd.shape)],
        out_specs=pl.BlockSpec((tm, ZC), lambda i: (i, 0)),
        out_shape=jax.ShapeDtypeStruct((n, ZC), BF16),
        scratch_shapes=[pltpu.VMEM((tm, D_MODEL), BF16)],
        compiler_params=_cparams(1),
        name="inproj",
    )(x, ng, w, cg, gb, bd)


SUBLANES = 8
CONV_PAD = 16
CONV_HALF = 256
CONV_ROWS = 64
CONV_SHIFT_ROWS = 104
CONV_NORM_TILES = 4


def _conv_kernel(z_ref, dw_ref, db_ref, lg_ref, lb_ref, u_ref, vs_ref, c_ref):
    s = z_ref.shape[0]
    span = s + 2 * CONV_PAD
    first = CONV_PAD - CONV_WIDTH // 2
    glu_rows = 256

    for half in range(CONV_CH // CONV_HALF):
        l0 = half * CONV_HALF
        vs_ref[0, 0:CONV_PAD, :] = jnp.zeros((CONV_PAD, CONV_HALF), F32)
        vs_ref[0, CONV_PAD + s:span + SUBLANES, :] = jnp.zeros((CONV_PAD + SUBLANES, CONV_HALF), F32)

        def glu(i, carry):
            r = pl.multiple_of(i * glu_rows, glu_rows)
            a = z_ref[pl.ds(r, glu_rows), l0:l0 + CONV_HALF].astype(F32)
            g = z_ref[pl.ds(r, glu_rows), CONV_CH + l0:CONV_CH + l0 + CONV_HALF].astype(F32)
            vs_ref[0, pl.ds(CONV_PAD + r, glu_rows), :] = a * jax.nn.sigmoid(g)
            return carry

        lax.fori_loop(0, s // glu_rows, glu, 0)

        def shift(i, carry):
            r = pl.multiple_of(i * CONV_SHIFT_ROWS, SUBLANES)
            win = vs_ref[0, pl.ds(r, CONV_SHIFT_ROWS + SUBLANES), :]
            for k in range(1, SUBLANES):
                vs_ref[k, pl.ds(r, CONV_SHIFT_ROWS), :] = win[k:k + CONV_SHIFT_ROWS]
            return carry

        assert span % CONV_SHIFT_ROWS == 0
        lax.fori_loop(0, span // CONV_SHIFT_ROWS, shift, 0)

        def conv(i, carry):
            r = pl.multiple_of(i * CONV_ROWS, CONV_ROWS)
            acc = jnp.zeros((CONV_ROWS, CONV_HALF), F32) + db_ref[:, l0:l0 + CONV_HALF]
            for w in range(CONV_WIDTH):
                k, a = (first + w) % SUBLANES, (first + w) // SUBLANES
                tap = vs_ref[k, pl.ds(r + a * SUBLANES, CONV_ROWS), :]
                acc = acc + tap * dw_ref[w:w + 1, l0:l0 + CONV_HALF]
            c_ref[pl.ds(r, CONV_ROWS), l0:l0 + CONV_HALF] = acc
            return carry

        lax.fori_loop(0, s // CONV_ROWS, conv, 0)

    def norm(i, carry):
        for t in range(CONV_NORM_TILES):
            r = pl.multiple_of((i * CONV_NORM_TILES + t) * CONV_ROWS, CONV_ROWS)
            c = c_ref[pl.ds(r, CONV_ROWS), :]
            xc = c - jnp.mean(c, axis=-1, keepdims=True)
            var = jnp.mean(xc * xc, axis=-1, keepdims=True)
            y = xc * lax.rsqrt(var + EPS) * lg_ref[...] + lb_ref[...]
            u_ref[pl.ds(r, CONV_ROWS), :] = (y * jax.nn.sigmoid(y)).astype(BF16)
        return carry

    assert s % (CONV_ROWS * CONV_NORM_TILES) == 0
    lax.fori_loop(0, s // (CONV_ROWS * CONV_NORM_TILES), norm, 0)


def _conv_mixer(z, bsz, s, dw, db, lg, lb):
    return pl.pallas_call(
        _conv_kernel,
        grid=(bsz,),
        in_specs=[pl.BlockSpec((s, 2 * CONV_CH), lambda b: (b, ZA // (2 * CONV_CH))),
                  _resident((CONV_WIDTH, CONV_CH)), _resident((1, CONV_CH)),
                  _resident((1, CONV_CH)), _resident((1, CONV_CH))],
        out_specs=pl.BlockSpec((s, CONV_CH), lambda b: (b, 0)),
        out_shape=jax.ShapeDtypeStruct((bsz * s, CONV_CH), BF16),
        scratch_shapes=[pltpu.VMEM((SUBLANES, s + 2 * CONV_PAD + SUBLANES, CONV_HALF), F32),
                        pltpu.VMEM((s, CONV_CH), F32)],
        compiler_params=_cparams(1),
        name="conv_mixer",
    )(z, dw, db, lg, lb)


def _head_masks():
    lane = lax.broadcasted_iota(jnp.int32, (1, LANES), 1)
    return lane < HEAD_DIM


ATTN_UNROLL = 16


def _softmax_pv(sc, vw, sink=None):
    m = jnp.max(sc, axis=-1, keepdims=True)
    if sink is not None:
        m = jnp.maximum(m, sink)
    p = jnp.exp(sc - m)
    l = jnp.sum(p, axis=-1, keepdims=True)
    if sink is not None:
        l = l + jnp.exp(sink - m)
    return _dot(p.astype(BF16), vw) * (1.0 / l)


def _wattn_kernel(sink_ref, q_ref, k_ref, v_ref, bias_ref, o_ref, kp_ref, vp_ref):
    hp = pl.program_id(1)
    s = q_ref.shape[0]
    nb = s // BLOCK
    lo = _head_masks()
    zero = jnp.zeros((), BF16)

    @pl.when(hp == 0)
    def _():
        pad = jnp.zeros((BLOCK, LANES), BF16)
        for src, dst in ((k_ref, kp_ref), (v_ref, vp_ref)):
            dst[0:BLOCK, :] = pad
            dst[BLOCK:BLOCK + s, :] = src[...]
            dst[BLOCK + s:2 * BLOCK + s, :] = pad

    def group(g, carry):
        rows, scores = [], []
        for u in range(ATTN_UNROLL):
            n = g * ATTN_UNROLL + u
            r = pl.multiple_of(n * BLOCK, BLOCK)
            var = jnp.where(n == 0, 0, jnp.where(n == nb - 1, 2, 1))
            q = q_ref[pl.ds(r, BLOCK), :]
            kw = kp_ref[pl.ds(r, 3 * BLOCK), :]
            rows.append(r)
            for hh in range(2):
                qm = jnp.where(lo if hh == 0 else jnp.logical_not(lo), q, zero)
                scores.append(_dot_nt(qm, kw) + bias_ref[hh, var])
        for u, r in enumerate(rows):
            vw = vp_ref[pl.ds(r, 3 * BLOCK), :]
            outs = [_softmax_pv(scores[2 * u + hh], vw, sink_ref[0, hp + hh * (WB_HEADS // 2)]) for hh in range(2)]
            o_ref[pl.ds(r, BLOCK), :] = jnp.where(lo, outs[0], outs[1]).astype(BF16)
        return carry

    lax.fori_loop(0, nb // ATTN_UNROLL, group, 0)


def _window_attention(z, bsz, s, sink, bias):
    npair = WB_HEADS // 2
    nv = bias.shape[1]
    assert WB_KV_HEADS * HEAD_DIM == LANES and (s // BLOCK) % ATTN_UNROLL == 0 and s // BLOCK >= 2
    return pl.pallas_call(
        _wattn_kernel,
        grid=(bsz, npair),
        in_specs=[pl.BlockSpec(memory_space=pltpu.SMEM),
                  pl.BlockSpec((s, LANES), lambda b, hp: (b, ZBQ // LANES + hp)),
                  pl.BlockSpec((s, LANES), lambda b, hp: (b, ZBK // LANES)),
                  pl.BlockSpec((s, LANES), lambda b, hp: (b, ZBV // LANES)),
                  pl.BlockSpec((2, nv, BLOCK, 3 * BLOCK), lambda b, hp: (hp, 0, 0, 0))],
        out_specs=pl.BlockSpec((s, LANES), lambda b, hp: (b, hp)),
        out_shape=jax.ShapeDtypeStruct((bsz * s, WB_HEADS * HEAD_DIM), BF16),
        scratch_shapes=[pltpu.VMEM((s + 2 * BLOCK, LANES), BF16), pltpu.VMEM((s + 2 * BLOCK, LANES), BF16)],
        compiler_params=_cparams(2),
        name="window_attention",
    )(sink, z, z, z, bias)


def _nattn_kernel(n_edge, n_pairs, q_ref, k_ref, v_ref, bias_ref, o_ref):
    lo = _head_masks()
    zero = jnp.zeros((), BF16)
    qn, kn = NA_QROWS * GRID_W, NA_KROWS * GRID_W
    max_ks = k_ref.shape[0] // GRID_W - NA_KROWS

    def group(g, carry):
        rows, scores = [], []
        for u in range(ATTN_UNROLL):
            i = g * ATTN_UNROLL + u
            r = pl.multiple_of(i * qn, qn)
            ks = jnp.clip(NA_QROWS * i - NA_KH // 2, 0, max_ks)
            rk = pl.multiple_of(ks * GRID_W, LANES)
            var = jnp.where(i < n_edge, i,
                            jnp.where(i >= n_pairs - n_edge, i - (n_pairs - 2 * n_edge - 1), n_edge))
            q = q_ref[pl.ds(r, qn), :]
            kw = k_ref[pl.ds(rk, kn), :]
            rows.append((r, rk))
            for hh in range(2):
                qm = jnp.where(lo if hh == 0 else jnp.logical_not(lo), q, zero)
                scores.append(_dot_nt(qm, kw) + bias_ref[0, var, hh])
        for u, (r, rk) in enumerate(rows):
            vw = v_ref[pl.ds(rk, kn), :]
            outs = [_softmax_pv(scores[2 * u + hh], vw) for hh in range(2)]
            o_ref[pl.ds(r, qn), :] = jnp.where(lo, outs[0], outs[1]).astype(BF16)
        return carry

    lax.fori_loop(0, n_pairs // ATTN_UNROLL, group, 0)


def _neighbourhood_attention(z, bsz, s, bias, layer):
    rows = s // GRID_W
    _, variants, variant_of_pair, kstart = _na_tables(rows)
    n_pairs = rows // NA_QROWS
    nv = len(variants)
    n_edge = (nv - 1) // 2
    assert n_pairs % ATTN_UNROLL == 0
    for i in range(n_pairs):
        want = i if i < n_edge else (i - (n_pairs - 2 * n_edge - 1) if i >= n_pairs - n_edge else n_edge)
        assert variant_of_pair[i] == want
        assert kstart[i] == int(np.clip(NA_QROWS * i - NA_KH // 2, 0, rows - NA_KROWS))
        assert (kstart[i] * GRID_W) % LANES == 0
    npair = NA_HEADS // 2
    qn, kn = NA_QROWS * GRID_W, NA_KROWS * GRID_W
    return pl.pallas_call(
        functools.partial(_nattn_kernel, n_edge, n_pairs),
        grid=(npair, bsz),
        in_specs=[pl.BlockSpec((s, LANES), lambda hp, b: (b, ZCQ // LANES + hp)),
                  pl.BlockSpec((s, LANES), lambda hp, b: (b, ZCK // LANES + hp)),
                  pl.BlockSpec((s, LANES), lambda hp, b: (b, ZCV // LANES + hp)),
                  pl.BlockSpec((1, nv, 2, qn, kn), lambda hp, b: (layer, 0, hp, 0, 0))],
        out_specs=pl.BlockSpec((s, LANES), lambda hp, b: (b, hp)),
        out_shape=jax.ShapeDtypeStruct((bsz * s, NA_HEADS * HEAD_DIM), BF16),
        compiler_params=_cparams(2),
        name="neighbourhood_attention",
    )(z, z, z, bias)


ROUTE_IDX_LANE = N_EXPERTS
ROUTE_W_LANE = N_EXPERTS + 2


def _merge_kernel(with_router, u_ref, ob_ref, oc_ref, g_ref, x_ref, wa_ref, wb_ref, wc_ref, wo_ref,
                  fg_ref, *rest):
    if with_router:
        wrh_ref, wrl_ref, br_ref, xo_ref, hf_ref, slab_ref = rest
    else:
        xo_ref, hf_ref = rest
    d = D_MODEL
    m = g_ref[:, 0:d].astype(F32) * _dot(u_ref[...], wa_ref[...])
    m = m + g_ref[:, d:2 * d].astype(F32) * _dot(ob_ref[...], wb_ref[...])
    m = m + g_ref[:, 2 * d:3 * d].astype(F32) * _dot(oc_ref[...], wc_ref[...])
    xn = x_ref[...] + _dot(m.astype(BF16), wo_ref[...])
    xo_ref[...] = xn
    hf = xn * lax.rsqrt(jnp.mean(xn * xn, axis=-1, keepdims=True) + EPS) * fg_ref[...]
    hf_ref[...] = hf.astype(hf_ref.dtype)
    if with_router:
        hf_hi = hf.astype(BF16)
        hf_lo = (hf - hf_hi.astype(F32)).astype(BF16)
        logits = (_dot(hf_hi, wrh_ref[...]) + _dot(hf_lo, wrh_ref[...]) + _dot(hf_hi, wrl_ref[...])
                  + br_ref[...])
        lane = lax.broadcasted_iota(jnp.int32, logits.shape, 1)
        m1 = jnp.max(logits, axis=-1, keepdims=True)
        i1 = jnp.min(jnp.where(logits == m1, lane, LANES), axis=-1, keepdims=True)
        rest_l = jnp.where(lane == i1, -jnp.inf, logits)
        m2 = jnp.max(rest_l, axis=-1, keepdims=True)
        i2 = jnp.min(jnp.where(rest_l == m2, lane, LANES), axis=-1, keepdims=True)
        e2 = jnp.exp(m2 - m1)
        w1 = 1.0 / (1.0 + e2)
        w2 = e2 / (1.0 + e2)
        slab = jnp.where(lane == ROUTE_IDX_LANE, i1.astype(F32), 0.0)
        slab = jnp.where(lane == ROUTE_IDX_LANE + 1, i2.astype(F32), slab)
        slab = jnp.where(lane == ROUTE_W_LANE, w1, slab)
        slab = jnp.where(lane == ROUTE_W_LANE + 1, w2, slab)
        slab_ref[...] = slab


def _merge(u, ob, oc, z, x, wa, wb, wc, wo, fg, router=None, tm=512):
    n = x.shape[0]
    with_router = router is not None
    row = lambda w: pl.BlockSpec((tm, w), lambda i: (i, 0))
    in_specs = [row(CONV_CH), row(WB_HEADS * HEAD_DIM), row(NA_HEADS * HEAD_DIM),
                pl.BlockSpec((tm, ZA), lambda i: (i, 0)), row(D_MODEL),
                _resident(wa.shape), _resident(wb.shape), _resident(wc.shape), _resident(wo.shape),
                _resident((1, D_MODEL))]
    args = [u, ob, oc, z, x, wa, wb, wc, wo, fg]
    out_specs = [row(D_MODEL), row(D_MODEL)]
    out_shape = [jax.ShapeDtypeStruct((n, D_MODEL), F32),
                 jax.ShapeDtypeStruct((n, D_MODEL), F32 if with_router else BF16)]
    if with_router:
        in_specs += [_resident((D_MODEL, LANES)), _resident((D_MODEL, LANES)), _resident((1, LANES))]
        args += list(router)
        out_specs.append(row(LANES))
        out_shape.append(jax.ShapeDtypeStruct((n, LANES), F32))
    return pl.pallas_call(
        functools.partial(_merge_kernel, with_router),
        grid=(n // tm,),
        in_specs=in_specs, out_specs=out_specs, out_shape=out_shape,
        compiler_params=_cparams(1),
        name="merge_router" if with_router else "merge",
    )(*args)


FFN_CHUNK = 1024


def _ffn_kernel(hf_ref, x_ref, w1_ref, w3_ref, w2_ref, o_ref):
    dff = w1_ref.shape[1]
    hf = hf_ref[...]
    acc = x_ref[...]
    for c0 in range(0, dff, FFN_CHUNK):
        c1 = min(c0 + FFN_CHUNK, dff)
        a = _dot(hf, w1_ref[:, c0:c1])
        b = _dot(hf, w3_ref[:, c0:c1])
        hm = (a * jax.nn.sigmoid(a) * b).astype(BF16)
        acc = acc + _dot(hm, w2_ref[c0:c1, :])
    o_ref[...] = acc


def _dense_ffn(hf, x, w1, w3, w2, tm=512):
    n = x.shape[0]
    row = pl.BlockSpec((tm, D_MODEL), lambda i: (i, 0))
    return pl.pallas_call(
        _ffn_kernel,
        grid=(n // tm,),
        in_specs=[row, row, _resident(w1.shape), _resident(w3.shape), _resident(w2.shape)],
        out_specs=row,
        out_shape=jax.ShapeDtypeStruct((n, D_MODEL), F32),
        input_output_aliases={1: 0},
        compiler_params=_cparams(1),
        name="dense_ffn",
    )(hf, x, w1, w3, w2)


MOE_TILE = 1024
MOE_FF_CHUNK = 1792
ROUTE_ROWS = 512


def _row_copy(src, dst, sem):
    return pltpu.make_async_copy(src, dst, sem)


def _dispatch_kernel(dest_ref, pad_base_ref, pad_cnt_ref, hf_ref, xs_ref, buf_ref, zrow_ref, load_sem,
                     row_sem, zero_sem, zero_tile_sem):
    i = pl.program_id(0)
    last = pl.num_programs(0) - 1
    slot = i % 2

    def load(step, slot_):
        src = hf_ref.at[pl.ds(pl.multiple_of(step * ROUTE_ROWS, ROUTE_ROWS), ROUTE_ROWS)]
        return pltpu.make_async_copy(src, buf_ref.at[slot_], load_sem.at[slot_])

    def drain(slot_):
        for _ in range(2 * ROUTE_ROWS):
            _row_copy(buf_ref.at[slot_, pl.ds(0, 1)], xs_ref.at[pl.ds(0, 1)], row_sem.at[slot_]).wait()

    def zero_row(dst_row):
        return _row_copy(zrow_ref.at[pl.ds(0, 1)], xs_ref.at[pl.ds(dst_row, 1)], zero_sem)

    def zero_rows(dst_row):
        dst = xs_ref.at[pl.ds(pl.multiple_of(dst_row, SUBLANES), SUBLANES)]
        return _row_copy(zrow_ref, dst, zero_tile_sem)

    unused = N_EXPERTS

    @pl.when(i == 0)
    def _():
        load(0, 0).start()
        zrow_ref[...] = jnp.zeros_like(zrow_ref)
        for e in range(N_EXPERTS):
            def zstart(j, carry):
                zero_row(pad_base_ref[e] + j).start()
                return carry

            lax.fori_loop(0, pad_cnt_ref[e], zstart, 0)

        def ztile(j, carry):
            zero_rows(pad_base_ref[unused] + j * SUBLANES).start()
            return carry

        lax.fori_loop(0, pad_cnt_ref[unused] // SUBLANES, ztile, 0)

    load(i, slot).wait()

    @pl.when(i > 0)
    def _():
        drain(1 - slot)

    @pl.when(i < last)
    def _():
        load(i + 1, 1 - slot).start()

    def start(r, carry):
        for k in range(2):
            d = dest_ref[0, 0, k * ROUTE_ROWS + r]
            _row_copy(buf_ref.at[slot, pl.ds(r, 1)], xs_ref.at[pl.ds(d, 1)], row_sem.at[slot]).start(priority=k)
        return carry

    lax.fori_loop(0, ROUTE_ROWS, start, 0, unroll=8)

    @pl.when(i == last)
    def _():
        drain(slot)
        for e in range(N_EXPERTS):
            def zwait(j, carry):
                zero_row(0).wait()
                return carry

            lax.fori_loop(0, pad_cnt_ref[e], zwait, 0)

        def ztile_wait(j, carry):
            zero_rows(0).wait()
            return carry

        lax.fori_loop(0, pad_cnt_ref[unused] // SUBLANES, ztile_wait, 0)


def _dispatch(hf, dest, pad_base, pad_cnt, n_rows):
    n = hf.shape[0]
    steps = n // ROUTE_ROWS
    smem = pl.BlockSpec(memory_space=pltpu.SMEM)
    return pl.pallas_call(
        _dispatch_kernel,
        grid=(steps,),
        in_specs=[pl.BlockSpec((1, 1, 2 * ROUTE_ROWS), lambda i: (i, 0, 0), memory_space=pltpu.SMEM),
                  smem, smem, pl.BlockSpec(memory_space=pl.ANY)],
        out_specs=pl.BlockSpec(memory_space=pl.ANY),
        out_shape=jax.ShapeDtypeStruct((n_rows, D_MODEL), F32),
        scratch_shapes=[pltpu.VMEM((2, ROUTE_ROWS, D_MODEL), F32), pltpu.VMEM((SUBLANES, D_MODEL), F32),
                        pltpu.SemaphoreType.DMA((2,)), pltpu.SemaphoreType.DMA((2,)),
                        pltpu.SemaphoreType.DMA(()), pltpu.SemaphoreType.DMA(())],
        compiler_params=_cparams(1),
        name="moe_dispatch",
    )(dest, pad_base, pad_cnt, hf)


F8 = jnp.float8_e4m3fn
F32_EXP_BIAS = 127
F32_MANT_BITS = 23
F8_TOP_EXP = 7


def _pow2_scale(amax):
    e = jnp.right_shift(lax.bitcast_convert_type(amax, jnp.int32), F32_MANT_BITS) & 0xFF
    se = jnp.clip(2 * F32_EXP_BIAS + F8_TOP_EXP - e, 1, 2 * F32_EXP_BIAS)
    scale = lax.bitcast_convert_type(jnp.left_shift(se, F32_MANT_BITS), F32)
    inv = lax.bitcast_convert_type(jnp.left_shift(2 * F32_EXP_BIAS - se, F32_MANT_BITS), F32)
    return scale, inv


def _quantize_rows(v):
    scale, inv = _pow2_scale(jnp.max(jnp.abs(v), axis=-1, keepdims=True))
    return (v * scale).astype(F8), inv


def _experts_kernel(te_ref, na_ref, winv_ref, xs_ref, w1_ref, w3_ref, w2_ref, ys_ref):
    i = pl.program_id(0)
    active = i < na_ref[0]
    e = te_ref[i]
    dff = w1_ref.shape[2]
    half = MOE_TILE // 2

    @pl.when(active)
    def _():
        xq = [_quantize_rows(xs_ref[h * half:(h + 1) * half, :]) for h in range(2)]
        for c0 in range(0, dff, MOE_FF_CHUNK):
            cols = slice(c0, c0 + MOE_FF_CHUNK)
            gates = [(_dot(xb, w1_ref[0, :, cols]) * (xinv * winv_ref[e, 0]), _dot(xb, w3_ref[0, :, cols]))
                     for xb, xinv in xq]
            for h, (a, b) in enumerate(gates):
                hm, hinv = _quantize_rows(a * jax.nn.sigmoid(a) * b)
                y = _dot(hm, w2_ref[0, cols, :]) * (hinv * xq[h][1] * (winv_ref[e, 1] * winv_ref[e, 2]))
                rows = slice(h * half, (h + 1) * half)
                if c0 == 0:
                    ys_ref[rows, :] = y
                else:
                    ys_ref[rows, :] += y

    @pl.when(jnp.logical_not(active))
    def _():
        ys_ref[...] = jnp.zeros_like(ys_ref)


def _quantize_expert_weights(w):
    scale, inv = _pow2_scale(jnp.max(jnp.abs(w.astype(F32)), axis=(1, 2)))
    return (w * scale[:, None, None]).astype(F8), inv


def _experts(xs, tile_expert, n_active, w1, w3, w2):
    n_rows = xs.shape[0]
    (w1, i1), (w3, i3), (w2, i2) = (_quantize_expert_weights(w) for w in (w1, w3, w2))
    winv = jnp.stack([i1, i3, i2], axis=1)
    dff = w1.shape[2]
    assert dff % MOE_FF_CHUNK == 0
    grid_spec = pltpu.PrefetchScalarGridSpec(
        num_scalar_prefetch=2,
        grid=(n_rows // MOE_TILE,),
        in_specs=[pl.BlockSpec(memory_space=pltpu.SMEM),
                  pl.BlockSpec((MOE_TILE, D_MODEL), lambda i, te, na: (jnp.minimum(i, na[0] - 1), 0)),
                  pl.BlockSpec((1, D_MODEL, dff), lambda i, te, na: (te[i], 0, 0)),
                  pl.BlockSpec((1, D_MODEL, dff), lambda i, te, na: (te[i], 0, 0)),
                  pl.BlockSpec((1, dff, D_MODEL), lambda i, te, na: (te[i], 0, 0))],
        out_specs=pl.BlockSpec((MOE_TILE, D_MODEL), lambda i, te, na: (i, 0)),
    )
    return pl.pallas_call(
        _experts_kernel,
        grid_spec=grid_spec,
        out_shape=jax.ShapeDtypeStruct((n_rows, D_MODEL), F32),
        compiler_params=_cparams(1),
        name="moe_experts",
    )(tile_expert, n_active, winv, xs, w1, w3, w2)


def _combine_kernel(dest_ref, next_dest_ref, x_ref, slab_ref, ys_ref, o_ref, y_ref, sem):
    i = pl.program_id(0)
    slot = i % 2

    def gather(d_ref, slot_):
        def start(r, carry):
            for k in range(2):
                d = d_ref[0, 0, k * ROUTE_ROWS + r]
                _row_copy(ys_ref.at[pl.ds(d, 1)], y_ref.at[slot_, k, pl.ds(r, 1)], sem.at[slot_]).start()
            return carry

        lax.fori_loop(0, ROUTE_ROWS, start, 0, unroll=8)

    @pl.when(i == 0)
    def _():
        gather(dest_ref, 0)

    @pl.when(i + 1 < pl.num_programs(0))
    def _():
        gather(next_dest_ref, 1 - slot)

    for _ in range(2 * ROUTE_ROWS):
        _row_copy(ys_ref.at[pl.ds(0, 1)], y_ref.at[slot, 0, pl.ds(0, 1)], sem.at[slot]).wait()
    w0 = slab_ref[:, ROUTE_W_LANE:ROUTE_W_LANE + 1]
    w1 = slab_ref[:, ROUTE_W_LANE + 1:ROUTE_W_LANE + 2]
    o_ref[...] = x_ref[...] + w0 * y_ref[slot, 0] + w1 * y_ref[slot, 1]


def _combine(x, slab, dest, ys):
    n = x.shape[0]
    steps = n // ROUTE_ROWS
    row = pl.BlockSpec((ROUTE_ROWS, D_MODEL), lambda i: (i, 0))
    dest_block = (1, 1, 2 * ROUTE_ROWS)
    return pl.pallas_call(
        _combine_kernel,
        grid=(steps,),
        in_specs=[pl.BlockSpec(dest_block, lambda i: (i, 0, 0), memory_space=pltpu.SMEM),
                  pl.BlockSpec(dest_block, lambda i: (jnp.minimum(i + 1, steps - 1), 0, 0),
                               memory_space=pltpu.SMEM),
                  row, pl.BlockSpec((ROUTE_ROWS, LANES), lambda i: (i, 0)),
                  pl.BlockSpec(memory_space=pl.ANY)],
        out_specs=row,
        out_shape=jax.ShapeDtypeStruct((n, D_MODEL), F32),
        scratch_shapes=[pltpu.VMEM((2, 2, ROUTE_ROWS, D_MODEL), F32), pltpu.SemaphoreType.DMA((2,))],
        input_output_aliases={2: 0},
        compiler_params=_cparams(1),
        name="moe_combine",
    )(dest, dest, x, slab, ys)


def _routing_plan(slab, n_rows):
    n = slab.shape[0]
    idx = slab[:, ROUTE_IDX_LANE:ROUTE_IDX_LANE + 2].astype(jnp.int32).T
    experts = jnp.arange(N_EXPERTS, dtype=jnp.int32)[:, None]
    chosen = [idx[k][None, :] == experts for k in range(2)]
    onehot = chosen[0].astype(jnp.int32) + chosen[1].astype(jnp.int32)
    csum = jnp.cumsum(onehot, axis=1)
    counts = csum[:, -1]
    padded = ((counts + MOE_TILE - 1) // MOE_TILE) * MOE_TILE
    ends = jnp.cumsum(padded)
    offs = ends - padded
    row = offs[:, None] + csum - onehot
    dest = jnp.stack([jnp.sum(jnp.where(c, row, 0), axis=0) for c in chosen])
    steps = n // ROUTE_ROWS
    dest = dest.reshape(2, steps, ROUTE_ROWS).transpose(1, 0, 2).reshape(steps, 1, 2 * ROUTE_ROWS)
    tile_start = jnp.arange(n_rows // MOE_TILE, dtype=jnp.int32) * MOE_TILE
    tile_expert = jnp.minimum(jnp.sum(tile_start[:, None] >= ends[None, :], axis=1), N_EXPERTS - 1)
    n_active = (ends[-1] // MOE_TILE).reshape(1)
    i32 = lambda v: v.astype(jnp.int32)
    pad_base = jnp.concatenate([offs + counts, ends[-1:]])
    pad_cnt = jnp.concatenate([padded - counts, n_rows - ends[-1:]])
    return i32(dest), i32(tile_expert), i32(n_active), i32(pad_base), i32(pad_cnt)


def _moe(hf, x, slab, w1, w3, w2):
    n = x.shape[0]
    n_rows = 2 * n + N_EXPERTS * MOE_TILE
    dest, tile_expert, n_active, pad_base, pad_cnt = _routing_plan(slab, n_rows)
    xs = _dispatch(hf, dest, pad_base, pad_cnt, n_rows)
    ys = _experts(xs, tile_expert, n_active, w1, w3, w2)
    return _combine(x, slab, dest, ys)


def _prep_inproj(w_in_l, gate_b_l, qn_b, kn_b, qn_c, kn_c):
    a_in = 2 * CONV_CH
    b_q = WB_HEADS * HEAD_DIM
    b_kv = WB_KV_HEADS * HEAD_DIM
    c_w = NA_HEADS * HEAD_DIM
    o = 0
    wa = w_in_l[:, o:o + a_in]; o += a_in
    wbq = w_in_l[:, o:o + b_q]; o += b_q
    wbk = w_in_l[:, o:o + b_kv]; o += b_kv
    wbv = w_in_l[:, o:o + b_kv]; o += b_kv
    wcq = w_in_l[:, o:o + c_w]; o += c_w
    wck = w_in_l[:, o:o + c_w]; o += c_w
    wcv = w_in_l[:, o:o + c_w]; o += c_w
    wg = w_in_l[:, o:]

    wbq = wbq.reshape(D_MODEL, WB_HEADS, HEAD_DIM)[:, WB_HEAD_ORDER, :].reshape(D_MODEL, b_q)
    w = jnp.concatenate([wg, wa, wbq, wcq, wck, wcv, wbk, wbv], axis=1).astype(BF16)
    scale = HEAD_DIM ** -0.5
    ones = lambda k: jnp.ones((k,), F32)
    cg = jnp.concatenate([
        ones(ZBQ), jnp.tile(qn_b.astype(F32), WB_HEADS) * scale,
        jnp.tile(qn_c.astype(F32), NA_HEADS) * scale, jnp.tile(kn_c.astype(F32), NA_HEADS), ones(ZBK - ZCV),
        jnp.tile(kn_b.astype(F32), WB_KV_HEADS), ones(ZC - ZBV)])
    return w, cg.reshape(1, ZC), gate_b_l.astype(F32).reshape(1, -1)


MXU_TILE = 256


def _segment_mean_matrix():
    idx = np.arange(MXU_TILE) // HEAD_DIM
    return jnp.asarray((idx[:, None] == idx[None, :]).astype(np.float32) / HEAD_DIM, dtype=BF16)


def kernel(x, t5_rel_bias, attn_norm_g, w_in, gate_b, conv_dw_w, conv_dw_b, conv_ln_g, conv_ln_b,
           conv_w_out, wb_qn_g, wb_kn_g, wb_sink, wb_w_out, na_qn_g, na_kn_g, na_rpb, na_w_out, w_o,
           ffn_norm_g, ffn_w1, ffn_w3, ffn_w2, moe_w_router, moe_b_router, moe_w1, moe_w3, moe_w2):
    bsz, s, d = x.shape
    depth = w_in.shape[0]
    assert d == D_MODEL and s % BLOCK == 0 and s % (NA_QROWS * GRID_W) == 0
    n = bsz * s
    xf = x.reshape(n, d).astype(F32)
    row = lambda v: v.astype(F32).reshape(1, -1)

    t5_bias = _t5_bias(t5_rel_bias)[np.asarray(WB_HEAD_ORDER)]
    head_rows = lambda w: w.reshape(WB_HEADS, HEAD_DIM, -1)[np.asarray(WB_HEAD_ORDER)].reshape(w.shape)
    na_bias = _na_bias(na_rpb, s // GRID_W)
    bd = _segment_mean_matrix()

    for layer in range(depth):
        w, cg, gb = _prep_inproj(w_in[layer], gate_b[layer], wb_qn_g[layer], wb_kn_g[layer],
                                 na_qn_g[layer], na_kn_g[layer])
        z = _inproj(xf, row(attn_norm_g[layer]), w, cg, gb, bd)
        u = _conv_mixer(z, bsz, s, conv_dw_w[layer].astype(F32), row(conv_dw_b[layer]),
                        row(conv_ln_g[layer]), row(conv_ln_b[layer]))
        ob = _window_attention(z, bsz, s, row(wb_sink[layer]), t5_bias)
        oc = _neighbourhood_attention(z, bsz, s, na_bias, layer)
        weights = (conv_w_out[layer].astype(BF16), head_rows(wb_w_out[layer]).astype(BF16),
                   na_w_out[layer].astype(BF16), w_o[layer].astype(BF16), row(ffn_norm_g[layer]))
        i = layer // 2
        if layer % 2 == 0:
            xf, hf = _merge(u, ob, oc, z, xf, *weights)
            xf = _dense_ffn(hf, xf, ffn_w1[i].astype(BF16), ffn_w3[i].astype(BF16), ffn_w2[i].astype(BF16))
        else:
            wr = jnp.zeros((D_MODEL, LANES), F32).at[:, :N_EXPERTS].set(moe_w_router[i].astype(F32))
            br = jnp.full((1, LANES), -jnp.inf, F32).at[0, :N_EXPERTS].set(moe_b_router[i].astype(F32))
            wr_hi = wr.astype(BF16)
            wr_lo = (wr - wr_hi.astype(F32)).astype(BF16)
            xf, hf, slab = _merge(u, ob, oc, z, xf, *weights, router=(wr_hi, wr_lo, br))
            xf = _moe(hf, xf, slab, moe_w1[i], moe_w3[i], moe_w2[i])
    return xf.reshape(bsz, s, d).astype(x.dtype)
```

```python
import functools
import math

import numpy as np
import jax
import jax.numpy as jnp
from jax import lax
from jax.experimental import pallas as pl
from jax.experimental.pallas import tpu as pltpu

D_MODEL = 1024
HEAD_DIM = 64
CONV_CH = 512
CONV_WIDTH = 31
WB_HEADS = 8
WB_KV_HEADS = 2
WINDOW = 128
BLOCK = 128
NA_HEADS = 8
GRID_W = 64
NA_KH = 8
NA_KW = 16
REL_BUCKETS = 32
REL_MAX_DIST = 128
N_EXPERTS = 8
EPS = 1e-6
NEG_INF = -1e30

LANES = 128

ZG = 0
ZA = 3072
ZBQ = 4096
ZCQ = 4608
ZCK = 5120
ZCV = 5632
ZBK = 6144
ZBV = 6272
ZC = 6400
ZCHUNK = 512
WB_HEAD_ORDER = tuple(h for p in range(WB_HEADS // 2) for h in (p, p + WB_HEADS // 2))

VMEM_LIMIT = 56 * 1024 * 1024

F32 = jnp.float32
BF16 = jnp.bfloat16


def _cparams(grid_rank, vmem=VMEM_LIMIT):
    return pltpu.CompilerParams(dimension_semantics=("arbitrary",) * grid_rank, vmem_limit_bytes=vmem)


def _resident(shape):
    nd = len(shape)
    return pl.BlockSpec(shape, lambda *_: (0,) * nd, pipeline_mode=pl.Buffered(1))


def _dot(a, b):
    return jnp.dot(a, b, preferred_element_type=F32)


def _dot_nt(a, b):
    return lax.dot_general(a, b, (((1,), (1,)), ((), ())), preferred_element_type=F32)


def _t5_bucket_np(rel):
    half = REL_BUCKETS // 2
    max_exact = half // 2
    ret = np.where(rel > 0, half, 0)
    n = np.abs(rel)
    nf = np.maximum(n, 1).astype(np.float32)
    large = max_exact + (np.log(nf / np.float32(max_exact)) / np.float32(math.log(REL_MAX_DIST / max_exact))
                         * np.float32(half - max_exact)).astype(np.int32)
    large = np.minimum(large, half - 1)
    return ret + np.where(n < max_exact, n, large)


def _window_bucket_table():
    col = np.arange(3 * BLOCK)[None, :]
    rel = (col - BLOCK) - np.arange(BLOCK)[:, None]
    band = np.where(np.abs(rel) <= WINDOW, _t5_bucket_np(rel), -1)
    first = np.where(col >= BLOCK, band, -1)
    last = np.where(col < 2 * BLOCK, band, -1)
    return np.stack([first, band, last]).astype(np.int32)


NA_QROWS = 2
NA_KROWS = 10


def _na_tables(rows):
    kh = min(NA_KH, rows)
    col = np.arange(GRID_W)
    col_start = np.clip(col - NA_KW // 2, 0, GRID_W - NA_KW)
    kc = np.arange(GRID_W)[None, :]
    inwin = (kc >= col_start[:, None]) & (kc < col_start[:, None] + NA_KW)
    dmat = np.where(inwin, kc - col[:, None] + NA_KW - 1, -1).astype(np.int32)
    variants, variant_of_pair, kstart = [], [], []
    for i in range(rows // NA_QROWS):
        ks = int(np.clip(NA_QROWS * i - kh // 2, 0, rows - NA_KROWS))
        tab = np.full((NA_QROWS, NA_KROWS), -1, np.int64)
        for a in range(NA_QROWS):
            r = NA_QROWS * i + a
            r0 = int(np.clip(r - kh // 2, 0, rows - kh))
            assert ks <= r0 and r0 + kh <= ks + NA_KROWS
            for kr in range(NA_KROWS):
                if r0 <= ks + kr < r0 + kh:
                    tab[a, kr] = ks + kr - r + NA_KH - 1
        for v, t in enumerate(variants):
            if np.array_equal(t, tab):
                variant_of_pair.append(v)
                break
        else:
            variant_of_pair.append(len(variants))
            variants.append(tab)
        kstart.append(ks)
    return dmat, variants, variant_of_pair, kstart


def _t5_bias_kernel(tab_ref, bucket_ref, o_ref):
    h = pl.program_id(0)
    for v in range(bucket_ref.shape[0]):
        bucket = bucket_ref[v]
        acc = jnp.full(bucket.shape, NEG_INF, F32)
        for b in range(REL_BUCKETS):
            acc = jnp.where(bucket == b, tab_ref[b, h], acc)
        o_ref[0, v] = acc


def _t5_bias(t5_rel_bias):
    bucket = jnp.asarray(_window_bucket_table())
    nv = bucket.shape[0]
    return pl.pallas_call(
        _t5_bias_kernel,
        grid=(WB_HEADS,),
        in_specs=[pl.BlockSpec(memory_space=pltpu.SMEM),
                  pl.BlockSpec((nv, BLOCK, 3 * BLOCK), lambda h: (0, 0, 0))],
        out_specs=pl.BlockSpec((1, nv, BLOCK, 3 * BLOCK), lambda h: (h, 0, 0, 0)),
        out_shape=jax.ShapeDtypeStruct((WB_HEADS, nv, BLOCK, 3 * BLOCK), F32),
        compiler_params=_cparams(1),
        name="t5_bias",
    )(t5_rel_bias.astype(F32), bucket)


def _na_bias_kernel(variants, rpb_ref, dmat_ref, o_ref):
    lh = pl.program_id(0) * NA_HEADS + pl.program_id(1)
    dmat = dmat_ref[...]
    ncol = 2 * NA_KW - 1
    base = []
    for dr in range(2 * NA_KH - 1):
        acc = jnp.full(dmat.shape, NEG_INF, F32)
        for d in range(ncol):
            acc = jnp.where(dmat == d, rpb_ref[lh, dr * ncol + d], acc)
        base.append(acc)
    neg = jnp.full(dmat.shape, NEG_INF, F32)
    for v, tab in enumerate(variants):
        for a in range(NA_QROWS):
            for kr in range(NA_KROWS):
                dr = int(tab[a, kr])
                tile = neg if dr < 0 else base[dr]
                o_ref[0, v, 0, a * GRID_W:(a + 1) * GRID_W, kr * GRID_W:(kr + 1) * GRID_W] = tile


def _na_bias(na_rpb, rows):
    depth = na_rpb.shape[0]
    dmat, variants, _, _ = _na_tables(rows)
    nv = len(variants)
    rpb = jnp.transpose(na_rpb.astype(F32), (0, 3, 1, 2)).reshape(depth * NA_HEADS, -1)
    qn, kn = NA_QROWS * GRID_W, NA_KROWS * GRID_W
    return pl.pallas_call(
        functools.partial(_na_bias_kernel, variants),
        grid=(depth, NA_HEADS),
        in_specs=[pl.BlockSpec(memory_space=pltpu.SMEM),
                  pl.BlockSpec((GRID_W, GRID_W), lambda l, h: (0, 0))],
        out_specs=pl.BlockSpec((1, nv, 1, qn, kn), lambda l, h: (l, 0, h, 0, 0)),
        out_shape=jax.ShapeDtypeStruct((depth, nv, NA_HEADS, qn, kn), F32),
        compiler_params=_cparams(2),
        name="na_bias",
    )(rpb, jnp.asarray(dmat))


def _inproj_kernel(x_ref, ng_ref, w_ref, cg_ref, gb_ref, bd_ref, z_ref, h_ref):
    x = x_ref[...]
    inv = lax.rsqrt(jnp.mean(x * x, axis=-1, keepdims=True) + EPS)
    h_ref[...] = (x * inv * ng_ref[...]).astype(BF16)

    def qknorm(zc, c0):
        w = min(bd_ref.shape[0], zc.shape[1])
        sq = (zc * zc).astype(BF16)
        ms = jnp.concatenate([_dot(sq[:, j:j + w], bd_ref[:w, :w]) for j in range(0, zc.shape[1], w)], axis=1)
        return zc * lax.rsqrt(ms + EPS) * cg_ref[:, c0:c0 + zc.shape[1]]

    for c0 in range(0, ZC, ZCHUNK):
        c1 = min(c0 + ZCHUNK, ZC)
        zc = _dot(h_ref[...], w_ref[:, c0:c1])
        if c0 < ZA:
            out = jax.nn.sigmoid(zc + gb_ref[:, c0:c1])
        elif c0 in (ZBQ, ZCQ, ZCK):
            out = qknorm(zc, c0)
        elif c0 == ZBK:
            nk = ZBV - ZBK
            out = jnp.concatenate([qknorm(zc[:, :nk], c0), zc[:, nk:]], axis=1)
        else:
            out = zc
        z_ref[:, c0:c1] = out.astype(BF16)


def _inproj(x, ng, w, cg, gb, bd, tm=512):
    n = x.shape[0]
    return pl.pallas_call(
        _inproj_kernel,
        grid=(n // tm,),
        in_specs=[pl.BlockSpec((tm, D_MODEL), lambda i: (i, 0)),
                  _resident((1, D_MODEL)), _resident((D_MODEL, ZC)), _resident((1, ZC)),
                  _resident((1, ZA)), _resident(bd.shape)],
        out_specs=pl.BlockSpec((tm, ZC), lambda i: (i, 0)),
        out_shape=jax.ShapeDtypeStruct((n, ZC), BF16),
        scratch_shapes=[pltpu.VMEM((tm, D_MODEL), BF16)],
        compiler_params=_cparams(1),
        name="inproj",
    )(x, ng, w, cg, gb, bd)


SUBLANES = 8
CONV_PAD = 16
CONV_HALF = 256
CONV_ROWS = 64
CONV_SHIFT_ROWS = 104
CONV_NORM_TILES = 4


def _conv_kernel(z_ref, dw_ref, db_ref, lg_ref, lb_ref, u_ref, vs_ref, c_ref):
    s = z_ref.shape[0]
    span = s + 2 * CONV_PAD
    first = CONV_PAD - CONV_WIDTH // 2
    glu_rows = 256

    for half in range(CONV_CH // CONV_HALF):
        l0 = half * CONV_HALF
        vs_ref[0, 0:CONV_PAD, :] = jnp.zeros((CONV_PAD, CONV_HALF), F32)
        vs_ref[0, CONV_PAD + s:span + SUBLANES, :] = jnp.zeros((CONV_PAD + SUBLANES, CONV_HALF), F32)

        def glu(i, carry):
            r = pl.multiple_of(i * glu_rows, glu_rows)
            a = z_ref[pl.ds(r, glu_rows), l0:l0 + CONV_HALF].astype(F32)
            g = z_ref[pl.ds(r, glu_rows), CONV_CH + l0:CONV_CH + l0 + CONV_HALF].astype(F32)
            vs_ref[0, pl.ds(CONV_PAD + r, glu_rows), :] = a * jax.nn.sigmoid(g)
            return carry

        lax.fori_loop(0, s // glu_rows, glu, 0)

        def shift(i, carry):
            r = pl.multiple_of(i * CONV_SHIFT_ROWS, SUBLANES)
            win = vs_ref[0, pl.ds(r, CONV_SHIFT_ROWS + SUBLANES), :]
            for k in range(1, SUBLANES):
                vs_ref[k, pl.ds(r, CONV_SHIFT_ROWS), :] = win[k:k + CONV_SHIFT_ROWS]
            return carry

        assert span % CONV_SHIFT_ROWS == 0
        lax.fori_loop(0, span // CONV_SHIFT_ROWS, shift, 0)

        def conv(i, carry):
            r = pl.multiple_of(i * CONV_ROWS, CONV_ROWS)
            acc = jnp.zeros((CONV_ROWS, CONV_HALF), F32) + db_ref[:, l0:l0 + CONV_HALF]
            for w in range(CONV_WIDTH):
                k, a = (first + w) % SUBLANES, (first + w) // SUBLANES
                tap = vs_ref[k, pl.ds(r + a * SUBLANES, CONV_ROWS), :]
                acc = acc + tap * dw_ref[w:w + 1, l0:l0 + CONV_HALF]
            c_ref[pl.ds(r, CONV_ROWS), l0:l0 + CONV_HALF] = acc
            return carry

        lax.fori_loop(0, s // CONV_ROWS, conv, 0)

    def norm(i, carry):
        for t in range(CONV_NORM_TILES):
            r = pl.multiple_of((i * CONV_NORM_TILES + t) * CONV_ROWS, CONV_ROWS)
            c = c_ref[pl.ds(r, CONV_ROWS), :]
            xc = c - jnp.mean(c, axis=-1, keepdims=True)
            var = jnp.mean(xc * xc, axis=-1, keepdims=True)
            y = xc * lax.rsqrt(var + EPS) * lg_ref[...] + lb_ref[...]
            u_ref[pl.ds(r, CONV_ROWS), :] = (y * jax.nn.sigmoid(y)).astype(BF16)
        return carry

    assert s % (CONV_ROWS * CONV_NORM_TILES) == 0
    lax.fori_loop(0, s // (CONV_ROWS * CONV_NORM_TILES), norm, 0)


def _conv_mixer(z, bsz, s, dw, db, lg, lb):
    return pl.pallas_call(
        _conv_kernel,
        grid=(bsz,),
        in_specs=[pl.BlockSpec((s, 2 * CONV_CH), lambda b: (b, ZA // (2 * CONV_CH))),
                  _resident((CONV_WIDTH, CONV_CH)), _resident((1, CONV_CH)),
                  _resident((1, CONV_CH)), _resident((1, CONV_CH))],
        out_specs=pl.BlockSpec((s, CONV_CH), lambda b: (b, 0)),
        out_shape=jax.ShapeDtypeStruct((bsz * s, CONV_CH), BF16),
        scratch_shapes=[pltpu.VMEM((SUBLANES, s + 2 * CONV_PAD + SUBLANES, CONV_HALF), F32),
                        pltpu.VMEM((s, CONV_CH), F32)],
        compiler_params=_cparams(1),
        name="conv_mixer",
    )(z, dw, db, lg, lb)


def _head_masks():
    lane = lax.broadcasted_iota(jnp.int32, (1, LANES), 1)
    return lane < HEAD_DIM


ATTN_UNROLL = 16


def _softmax_pv(sc, vw, sink=None):
    m = jnp.max(sc, axis=-1, keepdims=True)
    if sink is not None:
        m = jnp.maximum(m, sink)
    p = jnp.exp(sc - m)
    l = jnp.sum(p, axis=-1, keepdims=True)
    if sink is not None:
        l = l + jnp.exp(sink - m)
    return _dot(p.astype(BF16), vw) * (1.0 / l)


def _wattn_kernel(sink_ref, q_ref, k_ref, v_ref, bias_ref, o_ref, kp_ref, vp_ref):
    hp = pl.program_id(1)
    s = q_ref.shape[0]
    nb = s // BLOCK
    lo = _head_masks()
    zero = jnp.zeros((), BF16)

    @pl.when(hp == 0)
    def _():
        pad = jnp.zeros((BLOCK, LANES), BF16)
        for src, dst in ((k_ref, kp_ref), (v_ref, vp_ref)):
            dst[0:BLOCK, :] = pad
            dst[BLOCK:BLOCK + s, :] = src[...]
            dst[BLOCK + s:2 * BLOCK + s, :] = pad

    def group(g, carry):
        rows, scores = [], []
        for u in range(ATTN_UNROLL):
            n = g * ATTN_UNROLL + u
            r = pl.multiple_of(n * BLOCK, BLOCK)
            var = jnp.where(n == 0, 0, jnp.where(n == nb - 1, 2, 1))
            q = q_ref[pl.ds(r, BLOCK), :]
            kw = kp_ref[pl.ds(r, 3 * BLOCK), :]
            rows.append(r)
            for hh in range(2):
                qm = jnp.where(lo if hh == 0 else jnp.logical_not(lo), q, zero)
                scores.append(_dot_nt(qm, kw) + bias_ref[hh, var])
        for u, r in enumerate(rows):
            vw = vp_ref[pl.ds(r, 3 * BLOCK), :]
            outs = [_softmax_pv(scores[2 * u + hh], vw, sink_ref[0, hp + hh * (WB_HEADS // 2)]) for hh in range(2)]
            o_ref[pl.ds(r, BLOCK), :] = jnp.where(lo, outs[0], outs[1]).astype(BF16)
        return carry

    lax.fori_loop(0, nb // ATTN_UNROLL, group, 0)


def _window_attention(z, bsz, s, sink, bias):
    npair = WB_HEADS // 2
    nv = bias.shape[1]
    assert WB_KV_HEADS * HEAD_DIM == LANES and (s // BLOCK) % ATTN_UNROLL == 0 and s // BLOCK >= 2
    return pl.pallas_call(
        _wattn_kernel,
        grid=(bsz, npair),
        in_specs=[pl.BlockSpec(memory_space=pltpu.SMEM),
                  pl.BlockSpec((s, LANES), lambda b, hp: (b, ZBQ // LANES + hp)),
                  pl.BlockSpec((s, LANES), lambda b, hp: (b, ZBK // LANES)),
                  pl.BlockSpec((s, LANES), lambda b, hp: (b, ZBV // LANES)),
                  pl.BlockSpec((2, nv, BLOCK, 3 * BLOCK), lambda b, hp: (hp, 0, 0, 0))],
        out_specs=pl.BlockSpec((s, LANES), lambda b, hp: (b, hp)),
        out_shape=jax.ShapeDtypeStruct((bsz * s, WB_HEADS * HEAD_DIM), BF16),
        scratch_shapes=[pltpu.VMEM((s + 2 * BLOCK, LANES), BF16), pltpu.VMEM((s + 2 * BLOCK, LANES), BF16)],
        compiler_params=_cparams(2),
        name="window_attention",
    )(sink, z, z, z, bias)


def _nattn_kernel(n_edge, n_pairs, q_ref, k_ref, v_ref, bias_ref, o_ref):
    lo = _head_masks()
    zero = jnp.zeros((), BF16)
    qn, kn = NA_QROWS * GRID_W, NA_KROWS * GRID_W
    max_ks = k_ref.shape[0] // GRID_W - NA_KROWS

    def group(g, carry):
        rows, scores = [], []
        for u in range(ATTN_UNROLL):
            i = g * ATTN_UNROLL + u
            r = pl.multiple_of(i * qn, qn)
            ks = jnp.clip(NA_QROWS * i - NA_KH // 2, 0, max_ks)
            rk = pl.multiple_of(ks * GRID_W, LANES)
            var = jnp.where(i < n_edge, i,
                            jnp.where(i >= n_pairs - n_edge, i - (n_pairs - 2 * n_edge - 1), n_edge))
            q = q_ref[pl.ds(r, qn), :]
            kw = k_ref[pl.ds(rk, kn), :]
            rows.append((r, rk))
            for hh in range(2):
                qm = jnp.where(lo if hh == 0 else jnp.logical_not(lo), q, zero)
                scores.append(_dot_nt(qm, kw) + bias_ref[0, var, hh])
        for u, (r, rk) in enumerate(rows):
            vw = v_ref[pl.ds(rk, kn), :]
            outs = [_softmax_pv(scores[2 * u + hh], vw) for hh in range(2)]
            o_ref[pl.ds(r, qn), :] = jnp.where(lo, outs[0], outs[1]).astype(BF16)
        return carry

    lax.fori_loop(0, n_pairs // ATTN_UNROLL, group, 0)


def _neighbourhood_attention(z, bsz, s, bias, layer):
    rows = s // GRID_W
    _, variants, variant_of_pair, kstart = _na_tables(rows)
    n_pairs = rows // NA_QROWS
    nv = len(variants)
    n_edge = (nv - 1) // 2
    assert n_pairs % ATTN_UNROLL == 0
    for i in range(n_pairs):
        want = i if i < n_edge else (i - (n_pairs - 2 * n_edge - 1) if i >= n_pairs - n_edge else n_edge)
        assert variant_of_pair[i] == want
        assert kstart[i] == int(np.clip(NA_QROWS * i - NA_KH // 2, 0, rows - NA_KROWS))
        assert (kstart[i] * GRID_W) % LANES == 0
    npair = NA_HEADS // 2
    qn, kn = NA_QROWS * GRID_W, NA_KROWS * GRID_W
    return pl.pallas_call(
        functools.partial(_nattn_kernel, n_edge, n_pairs),
        grid=(npair, bsz),
        in_specs=[pl.BlockSpec((s, LANES), lambda hp, b: (b, ZCQ // LANES + hp)),
                  pl.BlockSpec((s, LANES), lambda hp, b: (b, ZCK // LANES + hp)),
                  pl.BlockSpec((s, LANES), lambda hp, b: (b, ZCV // LANES + hp)),
                  pl.BlockSpec((1, nv, 2, qn, kn), lambda hp, b: (layer, 0, hp, 0, 0))],
        out_specs=pl.BlockSpec((s, LANES), lambda hp, b: (b, hp)),
        out_shape=jax.ShapeDtypeStruct((bsz * s, NA_HEADS * HEAD_DIM), BF16),
        compiler_params=_cparams(2),
        name="neighbourhood_attention",
    )(z, z, z, bias)


ROUTE_IDX_LANE = N_EXPERTS
ROUTE_W_LANE = N_EXPERTS + 2


def _merge_kernel(with_router, u_ref, ob_ref, oc_ref, g_ref, x_ref, wa_ref, wb_ref, wc_ref, wo_ref,
                  fg_ref, *rest):
    if with_router:
        wrh_ref, wrl_ref, br_ref, xo_ref, hf_ref, slab_ref = rest
    else:
        xo_ref, hf_ref = rest
    d = D_MODEL
    m = g_ref[:, 0:d].astype(F32) * _dot(u_ref[...], wa_ref[...])
    m = m + g_ref[:, d:2 * d].astype(F32) * _dot(ob_ref[...], wb_ref[...])
    m = m + g_ref[:, 2 * d:3 * d].astype(F32) * _dot(oc_ref[...], wc_ref[...])
    xn = x_ref[...] + _dot(m.astype(BF16), wo_ref[...])
    xo_ref[...] = xn
    hf = xn * lax.rsqrt(jnp.mean(xn * xn, axis=-1, keepdims=True) + EPS) * fg_ref[...]
    hf_ref[...] = hf.astype(hf_ref.dtype)
    if with_router:
        hf_hi = hf.astype(BF16)
        hf_lo = (hf - hf_hi.astype(F32)).astype(BF16)
        logits = (_dot(hf_hi, wrh_ref[...]) + _dot(hf_lo, wrh_ref[...]) + _dot(hf_hi, wrl_ref[...])
                  + br_ref[...])
        lane = lax.broadcasted_iota(jnp.int32, logits.shape, 1)
        m1 = jnp.max(logits, axis=-1, keepdims=True)
        i1 = jnp.min(jnp.where(logits == m1, lane, LANES), axis=-1, keepdims=True)
        rest_l = jnp.where(lane == i1, -jnp.inf, logits)
        m2 = jnp.max(rest_l, axis=-1, keepdims=True)
        i2 = jnp.min(jnp.where(rest_l == m2, lane, LANES), axis=-1, keepdims=True)
        e2 = jnp.exp(m2 - m1)
        w1 = 1.0 / (1.0 + e2)
        w2 = e2 / (1.0 + e2)
        slab = jnp.where(lane == ROUTE_IDX_LANE, i1.astype(F32), 0.0)
        slab = jnp.where(lane == ROUTE_IDX_LANE + 1, i2.astype(F32), slab)
        slab = jnp.where(lane == ROUTE_W_LANE, w1, slab)
        slab = jnp.where(lane == ROUTE_W_LANE + 1, w2, slab)
        slab_ref[...] = slab


def _merge(u, ob, oc, z, x, wa, wb, wc, wo, fg, router=None, tm=512):
    n = x.shape[0]
    with_router = router is not None
    row = lambda w: pl.BlockSpec((tm, w), lambda i: (i, 0))
    in_specs = [row(CONV_CH), row(WB_HEADS * HEAD_DIM), row(NA_HEADS * HEAD_DIM),
                pl.BlockSpec((tm, ZA), lambda i: (i, 0)), row(D_MODEL),
                _resident(wa.shape), _resident(wb.shape), _resident(wc.shape), _resident(wo.shape),
                _resident((1, D_MODEL))]
    args = [u, ob, oc, z, x, wa, wb, wc, wo, fg]
    out_specs = [row(D_MODEL), row(D_MODEL)]
    out_shape = [jax.ShapeDtypeStruct((n, D_MODEL), F32),
                 jax.ShapeDtypeStruct((n, D_MODEL), F32 if with_router else BF16)]
    if with_router:
        in_specs += [_resident((D_MODEL, LANES)), _resident((D_MODEL, LANES)), _resident((1, LANES))]
        args += list(router)
        out_specs.append(row(LANES))
        out_shape.append(jax.ShapeDtypeStruct((n, LANES), F32))
    return pl.pallas_call(
        functools.partial(_merge_kernel, with_router),
        grid=(n // tm,),
        in_specs=in_specs, out_specs=out_specs, out_shape=out_shape,
        compiler_params=_cparams(1),
        name="merge_router" if with_router else "merge",
    )(*args)


FFN_CHUNK = 1024


def _ffn_kernel(hf_ref, x_ref, w1_ref, w3_ref, w2_ref, o_ref):
    dff = w1_ref.shape[1]
    hf = hf_ref[...]
    acc = x_ref[...]
    for c0 in range(0, dff, FFN_CHUNK):
        c1 = min(c0 + FFN_CHUNK, dff)
        a = _dot(hf, w1_ref[:, c0:c1])
        b = _dot(hf, w3_ref[:, c0:c1])
        hm = (a * jax.nn.sigmoid(a) * b).astype(BF16)
        acc = acc + _dot(hm, w2_ref[c0:c1, :])
    o_ref[...] = acc


def _dense_ffn(hf, x, w1, w3, w2, tm=512):
    n = x.shape[0]
    row = pl.BlockSpec((tm, D_MODEL), lambda i: (i, 0))
    return pl.pallas_call(
        _ffn_kernel,
        grid=(n // tm,),
        in_specs=[row, row, _resident(w1.shape), _resident(w3.shape), _resident(w2.shape)],
        out_specs=row,
        out_shape=jax.ShapeDtypeStruct((n, D_MODEL), F32),
        input_output_aliases={1: 0},
        compiler_params=_cparams(1),
        name="dense_ffn",
    )(hf, x, w1, w3, w2)


MOE_TILE = 1024
MOE_FF_CHUNK = 1792
ROUTE_ROWS = 512


def _row_copy(src, dst, sem):
    return pltpu.make_async_copy(src, dst, sem)


def _dispatch_kernel(dest_ref, pad_base_ref, pad_cnt_ref, hf_ref, xs_ref, buf_ref, zrow_ref, load_sem,
                     row_sem, zero_sem, zero_tile_sem):
    i = pl.program_id(0)
    last = pl.num_programs(0) - 1
    slot = i % 2

    def load(step, slot_):
        src = hf_ref.at[pl.ds(pl.multiple_of(step * ROUTE_ROWS, ROUTE_ROWS), ROUTE_ROWS)]
        return pltpu.make_async_copy(src, buf_ref.at[slot_], load_sem.at[slot_])

    def drain(slot_):
        for _ in range(2 * ROUTE_ROWS):
            _row_copy(buf_ref.at[slot_, pl.ds(0, 1)], xs_ref.at[pl.ds(0, 1)], row_sem.at[slot_]).wait()

    def zero_row(dst_row):
        return _row_copy(zrow_ref.at[pl.ds(0, 1)], xs_ref.at[pl.ds(dst_row, 1)], zero_sem)

    def zero_rows(dst_row):
        dst = xs_ref.at[pl.ds(pl.multiple_of(dst_row, SUBLANES), SUBLANES)]
        return _row_copy(zrow_ref, dst, zero_tile_sem)

    unused = N_EXPERTS

    @pl.when(i == 0)
    def _():
        load(0, 0).start()
        zrow_ref[...] = jnp.zeros_like(zrow_ref)
        for e in range(N_EXPERTS):
            def zstart(j, carry):
                zero_row(pad_base_ref[e] + j).start()
                return carry

            lax.fori_loop(0, pad_cnt_ref[e], zstart, 0)

        def ztile(j, carry):
            zero_rows(pad_base_ref[unused] + j * SUBLANES).start()
            return carry

        lax.fori_loop(0, pad_cnt_ref[unused] // SUBLANES, ztile, 0)

    load(i, slot).wait()

    @pl.when(i > 0)
    def _():
        drain(1 - slot)

    @pl.when(i < last)
    def _():
        load(i + 1, 1 - slot).start()

    def start(r, carry):
        for k in range(2):
            d = dest_ref[0, 0, k * ROUTE_ROWS + r]
            _row_copy(buf_ref.at[slot, pl.ds(r, 1)], xs_ref.at[pl.ds(d, 1)], row_sem.at[slot]).start(priority=k)
        return carry

    lax.fori_loop(0, ROUTE_ROWS, start, 0, unroll=8)

    @pl.when(i == last)
    def _():
        drain(slot)
        for e in range(N_EXPERTS):
            def zwait(j, carry):
                zero_row(0).wait()
                return carry

            lax.fori_loop(0, pad_cnt_ref[e], zwait, 0)

        def ztile_wait(j, carry):
            zero_rows(0).wait()
            return carry

        lax.fori_loop(0, pad_cnt_ref[unused] // SUBLANES, ztile_wait, 0)


def _dispatch(hf, dest, pad_base, pad_cnt, n_rows):
    n = hf.shape[0]
    steps = n // ROUTE_ROWS
    smem = pl.BlockSpec(memory_space=pltpu.SMEM)
    return pl.pallas_call(
        _dispatch_kernel,
        grid=(steps,),
        in_specs=[pl.BlockSpec((1, 1, 2 * ROUTE_ROWS), lambda i: (i, 0, 0), memory_space=pltpu.SMEM),
                  smem, smem, pl.BlockSpec(memory_space=pl.ANY)],
        out_specs=pl.BlockSpec(memory_space=pl.ANY),
        out_shape=jax.ShapeDtypeStruct((n_rows, D_MODEL), F32),
        scratch_shapes=[pltpu.VMEM((2, ROUTE_ROWS, D_MODEL), F32), pltpu.VMEM((SUBLANES, D_MODEL), F32),
                        pltpu.SemaphoreType.DMA((2,)), pltpu.SemaphoreType.DMA((2,)),
                        pltpu.SemaphoreType.DMA(()), pltpu.SemaphoreType.DMA(())],
        compiler_params=_cparams(1),
        name="moe_dispatch",
    )(dest, pad_base, pad_cnt, hf)


F8 = jnp.float8_e4m3fn
F32_EXP_BIAS = 127
F32_MANT_BITS = 23
F8_TOP_EXP = 7


def _pow2_scale(amax):
    e = jnp.right_shift(lax.bitcast_convert_type(amax, jnp.int32), F32_MANT_BITS) & 0xFF
    se = jnp.clip(2 * F32_EXP_BIAS + F8_TOP_EXP - e, 1, 2 * F32_EXP_BIAS)
    scale = lax.bitcast_convert_type(jnp.left_shift(se, F32_MANT_BITS), F32)
    inv = lax.bitcast_convert_type(jnp.left_shift(2 * F32_EXP_BIAS - se, F32_MANT_BITS), F32)
    return scale, inv


def _quantize_rows(v):
    scale, inv = _pow2_scale(jnp.max(jnp.abs(v), axis=-1, keepdims=True))
    return (v * scale).astype(F8), inv


def _experts_kernel(te_ref, na_ref, winv_ref, xs_ref, w1_ref, w3_ref, w2_ref, ys_ref):
    i = pl.program_id(0)
    active = i < na_ref[0]
    e = te_ref[i]
    dff = w1_ref.shape[2]
    half = MOE_TILE // 2

    @pl.when(active)
    def _():
        xq = [_quantize_rows(xs_ref[h * half:(h + 1) * half, :]) for h in range(2)]
        for c0 in range(0, dff, MOE_FF_CHUNK):
            cols = slice(c0, c0 + MOE_FF_CHUNK)
            gates = [(_dot(xb, w1_ref[0, :, cols]) * (xinv * winv_ref[e, 0]), _dot(xb, w3_ref[0, :, cols]))
                     for xb, xinv in xq]
            for h, (a, b) in enumerate(gates):
                hm, hinv = _quantize_rows(a * jax.nn.sigmoid(a) * b)
                y = _dot(hm, w2_ref[0, cols, :]) * (hinv * xq[h][1] * (winv_ref[e, 1] * winv_ref[e, 2]))
                rows = slice(h * half, (h + 1) * half)
                if c0 == 0:
                    ys_ref[rows, :] = y
                else:
                    ys_ref[rows, :] += y

    @pl.when(jnp.logical_not(active))
    def _():
        ys_ref[...] = jnp.zeros_like(ys_ref)


def _quantize_expert_weights(w):
    scale, inv = _pow2_scale(jnp.max(jnp.abs(w.astype(F32)), axis=(1, 2)))
    return (w * scale[:, None, None]).astype(F8), inv


def _experts(xs, tile_expert, n_active, w1, w3, w2):
    n_rows = xs.shape[0]
    (w1, i1), (w3, i3), (w2, i2) = (_quantize_expert_weights(w) for w in (w1, w3, w2))
    winv = jnp.stack([i1, i3, i2], axis=1)
    dff = w1.shape[2]
    assert dff % MOE_FF_CHUNK == 0
    grid_spec = pltpu.PrefetchScalarGridSpec(
        num_scalar_prefetch=2,
        grid=(n_rows // MOE_TILE,),
        in_specs=[pl.BlockSpec(memory_space=pltpu.SMEM),
                  pl.BlockSpec((MOE_TILE, D_MODEL), lambda i, te, na: (jnp.minimum(i, na[0] - 1), 0)),
                  pl.BlockSpec((1, D_MODEL, dff), lambda i, te, na: (te[i], 0, 0)),
                  pl.BlockSpec((1, D_MODEL, dff), lambda i, te, na: (te[i], 0, 0)),
                  pl.BlockSpec((1, dff, D_MODEL), lambda i, te, na: (te[i], 0, 0))],
        out_specs=pl.BlockSpec((MOE_TILE, D_MODEL), lambda i, te, na: (i, 0)),
    )
    return pl.pallas_call(
        _experts_kernel,
        grid_spec=grid_spec,
        out_shape=jax.ShapeDtypeStruct((n_rows, D_MODEL), F32),
        compiler_params=_cparams(1),
        name="moe_experts",
    )(tile_expert, n_active, winv, xs, w1, w3, w2)


def _combine_kernel(dest_ref, next_dest_ref, x_ref, slab_ref, ys_ref, o_ref, y_ref, sem):
    i = pl.program_id(0)
    slot = i % 2

    def gather(d_ref, slot_):
        def start(r, carry):
            for k in range(2):
                d = d_ref[0, 0, k * ROUTE_ROWS + r]
                _row_copy(ys_ref.at[pl.ds(d, 1)], y_ref.at[slot_, k, pl.ds(r, 1)], sem.at[slot_]).start(priority=k)
            return carry

        lax.fori_loop(0, ROUTE_ROWS, start, 0, unroll=8)

    @pl.when(i == 0)
    def _():
        gather(dest_ref, 0)

    @pl.when(i + 1 < pl.num_programs(0))
    def _():
        gather(next_dest_ref, 1 - slot)

    for _ in range(2 * ROUTE_ROWS):
        _row_copy(ys_ref.at[pl.ds(0, 1)], y_ref.at[slot, 0, pl.ds(0, 1)], sem.at[slot]).wait()
    w0 = slab_ref[:, ROUTE_W_LANE:ROUTE_W_LANE + 1]
    w1 = slab_ref[:, ROUTE_W_LANE + 1:ROUTE_W_LANE + 2]
    o_ref[...] = x_ref[...] + w0 * y_ref[slot, 0] + w1 * y_ref[slot, 1]


def _combine(x, slab, dest, ys):
    n = x.shape[0]
    steps = n // ROUTE_ROWS
    row = pl.BlockSpec((ROUTE_ROWS, D_MODEL), lambda i: (i, 0))
    dest_block = (1, 1, 2 * ROUTE_ROWS)
    return pl.pallas_call(
        _combine_kernel,
        grid=(steps,),
        in_specs=[pl.BlockSpec(dest_block, lambda i: (i, 0, 0), memory_space=pltpu.SMEM),
                  pl.BlockSpec(dest_block, lambda i: (jnp.minimum(i + 1, steps - 1), 0, 0),
                               memory_space=pltpu.SMEM),
                  row, pl.BlockSpec((ROUTE_ROWS, LANES), lambda i: (i, 0)),
                  pl.BlockSpec(memory_space=pl.ANY)],
        out_specs=row,
        out_shape=jax.ShapeDtypeStruct((n, D_MODEL), F32),
        scratch_shapes=[pltpu.VMEM((2, 2, ROUTE_ROWS, D_MODEL), F32), pltpu.SemaphoreType.DMA((2,))],
        input_output_aliases={2: 0},
        compiler_params=_cparams(1),
        name="moe_combine",
    )(dest, dest, x, slab, ys)


def _routing_plan(slab, n_rows):
    n = slab.shape[0]
    idx = slab[:, ROUTE_IDX_LANE:ROUTE_IDX_LANE + 2].astype(jnp.int32).T
    experts = jnp.arange(N_EXPERTS, dtype=jnp.int32)[:, None]
    chosen = [idx[k][None, :] == experts for k in range(2)]
    onehot = chosen[0].astype(jnp.int32) + chosen[1].astype(jnp.int32)
    csum = jnp.cumsum(onehot, axis=1)
    counts = csum[:, -1]
    padded = ((counts + MOE_TILE - 1) // MOE_TILE) * MOE_TILE
    ends = jnp.cumsum(padded)
    offs = ends - padded
    row = offs[:, None] + csum - onehot
    dest = jnp.stack([jnp.sum(jnp.where(c, row, 0), axis=0) for c in chosen])
    steps = n // ROUTE_ROWS
    dest = dest.reshape(2, steps, ROUTE_ROWS).transpose(1, 0, 2).reshape(steps, 1, 2 * ROUTE_ROWS)
    tile_start = jnp.arange(n_rows // MOE_TILE, dtype=jnp.int32) * MOE_TILE
    tile_expert = jnp.minimum(jnp.sum(tile_start[:, None] >= ends[None, :], axis=1), N_EXPERTS - 1)
    n_active = (ends[-1] // MOE_TILE).reshape(1)
    i32 = lambda v: v.astype(jnp.int32)
    pad_base = jnp.concatenate([offs + counts, ends[-1:]])
    pad_cnt = jnp.concatenate([padded - counts, n_rows - ends[-1:]])
    return i32(dest), i32(tile_expert), i32(n_active), i32(pad_base), i32(pad_cnt)


def _moe(hf, x, slab, w1, w3, w2):
    n = x.shape[0]
    n_rows = 2 * n + N_EXPERTS * MOE_TILE
    dest, tile_expert, n_active, pad_base, pad_cnt = _routing_plan(slab, n_rows)
    xs = _dispatch(hf, dest, pad_base, pad_cnt, n_rows)
    ys = _experts(xs, tile_expert, n_active, w1, w3, w2)
    return _combine(x, slab, dest, ys)


def _prep_inproj(w_in_l, gate_b_l, qn_b, kn_b, qn_c, kn_c):
    a_in = 2 * CONV_CH
    b_q = WB_HEADS * HEAD_DIM
    b_kv = WB_KV_HEADS * HEAD_DIM
    c_w = NA_HEADS * HEAD_DIM
    o = 0
    wa = w_in_l[:, o:o + a_in]; o += a_in
    wbq = w_in_l[:, o:o + b_q]; o += b_q
    wbk = w_in_l[:, o:o + b_kv]; o += b_kv
    wbv = w_in_l[:, o:o + b_kv]; o += b_kv
    wcq = w_in_l[:, o:o + c_w]; o += c_w
    wck = w_in_l[:, o:o + c_w]; o += c_w
    wcv = w_in_l[:, o:o + c_w]; o += c_w
    wg = w_in_l[:, o:]

    wbq = wbq.reshape(D_MODEL, WB_HEADS, HEAD_DIM)[:, WB_HEAD_ORDER, :].reshape(D_MODEL, b_q)
    w = jnp.concatenate([wg, wa, wbq, wcq, wck, wcv, wbk, wbv], axis=1).astype(BF16)
    scale = HEAD_DIM ** -0.5
    ones = lambda k: jnp.ones((k,), F32)
    cg = jnp.concatenate([
        ones(ZBQ), jnp.tile(qn_b.astype(F32), WB_HEADS) * scale,
        jnp.tile(qn_c.astype(F32), NA_HEADS) * scale, jnp.tile(kn_c.astype(F32), NA_HEADS), ones(ZBK - ZCV),
        jnp.tile(kn_b.astype(F32), WB_KV_HEADS), ones(ZC - ZBV)])
    return w, cg.reshape(1, ZC), gate_b_l.astype(F32).reshape(1, -1)


MXU_TILE = 256


def _segment_mean_matrix():
    idx = np.arange(MXU_TILE) // HEAD_DIM
    return jnp.asarray((idx[:, None] == idx[None, :]).astype(np.float32) / HEAD_DIM, dtype=BF16)


def kernel(x, t5_rel_bias, attn_norm_g, w_in, gate_b, conv_dw_w, conv_dw_b, conv_ln_g, conv_ln_b,
           conv_w_out, wb_qn_g, wb_kn_g, wb_sink, wb_w_out, na_qn_g, na_kn_g, na_rpb, na_w_out, w_o,
           ffn_norm_g, ffn_w1, ffn_w3, ffn_w2, moe_w_router, moe_b_router, moe_w1, moe_w3, moe_w2):
    bsz, s, d = x.shape
    depth = w_in.shape[0]
    assert d == D_MODEL and s % BLOCK == 0 and s % (NA_QROWS * GRID_W) == 0
    n = bsz * s
    xf = x.reshape(n, d).astype(F32)
    row = lambda v: v.astype(F32).reshape(1, -1)

    t5_bias = _t5_bias(t5_rel_bias)[np.asarray(WB_HEAD_ORDER)]
    head_rows = lambda w: w.reshape(WB_HEADS, HEAD_DIM, -1)[np.asarray(WB_HEAD_ORDER)].reshape(w.shape)
    na_bias = _na_bias(na_rpb, s // GRID_W)
    bd = _segment_mean_matrix()

    for layer in range(depth):
        w, cg, gb = _prep_inproj(w_in[layer], gate_b[layer], wb_qn_g[layer], wb_kn_g[layer],
                                 na_qn_g[layer], na_kn_g[layer])
        z = _inproj(xf, row(attn_norm_g[layer]), w, cg, gb, bd)
        u = _conv_mixer(z, bsz, s, conv_dw_w[layer].astype(F32), row(conv_dw_b[layer]),
                        row(conv_ln_g[layer]), row(conv_ln_b[layer]))
        ob = _window_attention(z, bsz, s, row(wb_sink[layer]), t5_bias)
        oc = _neighbourhood_attention(z, bsz, s, na_bias, layer)
        weights = (conv_w_out[layer].astype(BF16), head_rows(wb_w_out[layer]).astype(BF16),
                   na_w_out[layer].astype(BF16), w_o[layer].astype(BF16), row(ffn_norm_g[layer]))
        i = layer // 2
        if layer % 2 == 0:
            xf, hf = _merge(u, ob, oc, z, xf, *weights)
            xf = _dense_ffn(hf, xf, ffn_w1[i].astype(BF16), ffn_w3[i].astype(BF16), ffn_w2[i].astype(BF16))
        else:
            wr = jnp.zeros((D_MODEL, LANES), F32).at[:, :N_EXPERTS].set(moe_w_router[i].astype(F32))
            br = jnp.full((1, LANES), -jnp.inf, F32).at[0, :N_EXPERTS].set(moe_b_router[i].astype(F32))
            wr_hi = wr.astype(BF16)
            wr_lo = (wr - wr_hi.astype(F32)).astype(BF16)
            xf, hf, slab = _merge(u, ob, oc, z, xf, *weights, router=(wr_hi, wr_lo, br))
            xf = _moe(hf, xf, slab, moe_w1[i], moe_w3[i], moe_w2[i])
    return xf.reshape(bsz, s, d).astype(x.dtype)
```
